```python
import math
import jax, jax.numpy as jnp
from jax import lax
import numpy as np

D_MODEL = 1024
BATCH = 8
SEQ = 8192
DEPTH = 1

GRID_W = 64
N_Q_HEADS = 8
N_KV_HEADS = 2
HEAD_DIM = 128
Q_BLOCK = 128
ROPE_THETA = 10000.0
ROPE_AXIS_DIM = HEAD_DIM // 2
N_FOURIER_GROUPS = 4
FOURIER_GROUP_DIM = 128
FOURIER_DIM = N_FOURIER_GROUPS * FOURIER_GROUP_DIM
N_BRANCHES = 2
Q_DIM = N_Q_HEADS * HEAD_DIM
KV_DIM = N_KV_HEADS * HEAD_DIM
IN_COLS = FOURIER_DIM + Q_DIM + 2 * KV_DIM + N_BRANCHES * D_MODEL
N_EXPERTS = 32
TOP_K = 4
D_EXPERT = 1024
SWIGLU_LIMIT = 7.0
SWIGLU_ALPHA = 1.702
MOE_BLOCK = 128
N_MOD = 6
DEEPNORM_ALPHA = (2 * DEPTH) ** 0.25
DEEPNORM_BETA = (8 * DEPTH) ** -0.25
LN_EPS = 1e-5
ADA_EPS = 1e-6
QK_EPS = 1e-6

kernel_name = "hybrid_fourier_gqa_moe_deepnorm_block"


def layer_norm(x, g=None, b=None, eps=LN_EPS):
    xf = x.astype(jnp.float32)
    mu = jnp.mean(xf, axis=-1, keepdims=True)
    var = jnp.mean(jnp.square(xf - mu), axis=-1, keepdims=True)
    y = (xf - mu) * lax.rsqrt(var + eps)
    if g is not None:
        y = y * g.astype(jnp.float32) + b.astype(jnp.float32)
    return y.astype(x.dtype)


def rms_norm(x, g, eps=QK_EPS):
    xf = x.astype(jnp.float32)
    y = xf * lax.rsqrt(jnp.mean(jnp.square(xf), axis=-1, keepdims=True) + eps)
    return (y * g.astype(jnp.float32)).astype(x.dtype)


def axial_rope_angles(seq):
    rows = seq // GRID_W
    row_ids = jnp.repeat(jnp.arange(rows, dtype=jnp.float32), GRID_W)
    col_ids = jnp.tile(jnp.arange(GRID_W, dtype=jnp.float32), rows)
    freqs = ROPE_THETA ** (-jnp.arange(0, ROPE_AXIS_DIM, 2, dtype=jnp.float32) / ROPE_AXIS_DIM)
    return row_ids[:, None] * freqs, col_ids[:, None] * freqs


def _rotate_half(x, ang):
    cos = jnp.cos(ang)[:, None, :].astype(x.dtype)
    sin = jnp.sin(ang)[:, None, :].astype(x.dtype)
    x1, x2 = jnp.split(x, 2, axis=-1)
    return jnp.concatenate([x1 * cos - x2 * sin, x2 * cos + x1 * sin], axis=-1)


def apply_axial_rope(x, ang_r, ang_c):
    return jnp.concatenate([_rotate_half(x[..., :ROPE_AXIS_DIM], ang_r),
                            _rotate_half(x[..., ROPE_AXIS_DIM:], ang_c)], axis=-1)


def fourier_mix(f):
    b, s, _ = f.shape
    g = f.astype(jnp.float32).reshape(b, s, N_FOURIER_GROUPS, FOURIER_GROUP_DIM)
    y = jnp.fft.fftn(g, axes=(1, 3), norm="ortho").real
    return y.reshape(b, s, FOURIER_DIM).astype(f.dtype)


def blocked_gqa(q, k, v):
    b, s, _, _ = q.shape
    grp = N_Q_HEADS // N_KV_HEADS
    nblk = s // Q_BLOCK
    scale = 1.0 / math.sqrt(HEAD_DIM)
    qb = q.reshape(b, nblk, Q_BLOCK, N_KV_HEADS, grp, HEAD_DIM).transpose(1, 0, 2, 3, 4, 5)

    def one_block(q_blk):
        sc = jnp.einsum('bqhgd,bkhd->bhgqk', q_blk, k, preferred_element_type=jnp.float32) * scale
        p = jax.nn.softmax(sc, axis=-1).astype(v.dtype)
        return jnp.einsum('bhgqk,bkhd->bqhgd', p, v)

    o = lax.map(one_block, qb)
    return o.transpose(1, 0, 2, 3, 4, 5).reshape(b, s, Q_DIM)


def mixing_sublayer(u, w_in, q_norm_g, k_norm_g, w_fourier, w_attn_o, w_out, ang_r, ang_c):
    b, s, _ = u.shape
    z = u @ w_in
    cuts = np.cumsum([FOURIER_DIM, Q_DIM, KV_DIM, KV_DIM]).tolist()
    f, q, k, v, gates = jnp.split(z, cuts, axis=-1)
    y_f = fourier_mix(f) @ w_fourier
    q = apply_axial_rope(rms_norm(q.reshape(b, s, N_Q_HEADS, HEAD_DIM), q_norm_g), ang_r, ang_c)
    k = apply_axial_rope(rms_norm(k.reshape(b, s, N_KV_HEADS, HEAD_DIM), k_norm_g), ang_r, ang_c)
    v = v.reshape(b, s, N_KV_HEADS, HEAD_DIM)
    y_a = blocked_gqa(q, k, v) @ w_attn_o
    g_f, g_a = jnp.split(gates, N_BRANCHES, axis=-1)
    m = jax.nn.sigmoid(g_f) * y_f + jax.nn.sigmoid(g_a) * y_a
    return m @ w_out


def moe_sublayer(u, w_router, b_router, w_gate_up, b_gate_up, w_down, b_down):
    b, s, d = u.shape
    n = b * s
    xf = u.reshape(n, d)
    logits = (xf @ w_router + b_router).astype(jnp.float32)
    top_vals, top_idx = lax.top_k(logits, TOP_K)
    top_w = jax.nn.softmax(top_vals, axis=-1)

    n_slots = n * TOP_K
    e_flat = top_idx.reshape(n_slots).astype(jnp.int32)
    w_flat = top_w.reshape(n_slots)
    tok_flat = jnp.arange(n_slots, dtype=jnp.int32) // TOP_K

    order = jnp.argsort(e_flat)
    sorted_e = e_flat[order]
    counts = jnp.zeros((N_EXPERTS,), jnp.int32).at[e_flat].add(1)
    starts = jnp.cumsum(counts) - counts
    pcounts = ((counts + MOE_BLOCK - 1) // MOE_BLOCK) * MOE_BLOCK
    pends = jnp.cumsum(pcounts)
    pstarts = pends - pcounts
    dest = pstarts[sorted_e] + (jnp.arange(n_slots, dtype=jnp.int32) - starts[sorted_e])

    n_pad = n_slots + N_EXPERTS * MOE_BLOCK
    nblk = n_pad // MOE_BLOCK
    buf_tok = jnp.zeros((n_pad,), jnp.int32).at[dest].set(tok_flat[order])
    buf_w = jnp.zeros((n_pad,), jnp.float32).at[dest].set(w_flat[order])
    buf_valid = jnp.zeros((n_pad,), jnp.bool_).at[dest].set(True)
    block_starts = jnp.arange(nblk, dtype=jnp.int32) * MOE_BLOCK
    block_e = jnp.clip(jnp.searchsorted(pends, block_starts, side='right'), 0, N_EXPERTS - 1)

    def expert_block(args):
        e, rows = args
        xb = xf[rows]
        h = xb @ w_gate_up[e] + b_gate_up[e]
        gate, up = h[:, :D_EXPERT], h[:, D_EXPERT:]
        gate = jnp.minimum(gate, SWIGLU_LIMIT)
        up = jnp.clip(up, -SWIGLU_LIMIT, SWIGLU_LIMIT)
        act = (up + 1.0) * gate * jax.nn.sigmoid(SWIGLU_ALPHA * gate)
        return act @ w_down[e] + b_down[e]

    y = lax.map(expert_block, (block_e, buf_tok.reshape(nblk, MOE_BLOCK))).reshape(n_pad, d)
    contrib = jnp.where(buf_valid[:, None], y.astype(jnp.float32) * buf_w[:, None], 0.0)
    out = jnp.zeros((n, d), jnp.float32).at[buf_tok].add(contrib)
    return out.reshape(b, s, d).astype(u.dtype)


def setup_inputs(seed: int = 0) -> dict:
    key = jax.random.key(seed)
    ks = jax.random.split(key, 20)
    f32 = jnp.float32
    nrm = lambda k, shp: jax.random.normal(k, shp, f32)
    L, D, E, F = DEPTH, D_MODEL, N_EXPERTS, D_EXPERT
    return {
        "x": nrm(ks[0], (BATCH, SEQ, D)),
        "c": nrm(ks[1], (BATCH, D)),
        "w_ada": nrm(ks[2], (L, D, N_MOD * D)) * D ** -0.5,
        "b_ada": nrm(ks[3], (L, N_MOD * D)) * 0.02,
        "w_in": nrm(ks[4], (L, D, IN_COLS)) * D ** -0.5,
        "q_norm_g": 1.0 + 0.02 * nrm(ks[5], (L, HEAD_DIM)),
        "k_norm_g": 1.0 + 0.02 * nrm(ks[6], (L, HEAD_DIM)),
        "w_fourier": nrm(ks[7], (L, FOURIER_DIM, D)) * FOURIER_DIM ** -0.5,
        "w_attn_o": nrm(ks[8], (L, Q_DIM, D)) * Q_DIM ** -0.5,
        "w_out": nrm(ks[9], (L, D, D)) * (D ** -0.5 * DEEPNORM_BETA),
        "ln1_g": 1.0 + 0.02 * nrm(ks[10], (L, D)),
        "ln1_b": 0.02 * nrm(ks[11], (L, D)),
        "w_router": nrm(ks[12], (L, D, E)) * D ** -0.5,
        "b_router": nrm(ks[13], (L, E)) * 0.01,
        "w_gate_up": nrm(ks[14], (L, E, D, 2 * F)) * D ** -0.5,
        "b_gate_up": nrm(ks[15], (L, E, 2 * F)) * 0.02,
        "w_down": nrm(ks[16], (L, E, F, D)) * (F ** -0.5 * DEEPNORM_BETA),
        "b_down": nrm(ks[17], (L, E, D)) * 0.02,
        "ln2_g": 1.0 + 0.02 * nrm(ks[18], (L, D)),
        "ln2_b": 0.02 * nrm(ks[19], (L, D)),
    }


def reference(x, c, w_ada, b_ada, w_in, q_norm_g, k_norm_g, w_fourier, w_attn_o, w_out,
              ln1_g, ln1_b, w_router, b_router, w_gate_up, b_gate_up, w_down, b_down,
              ln2_g, ln2_b):
    seq = x.shape[1]
    ang_r, ang_c = axial_rope_angles(seq)
    cond = jax.nn.silu(c)
    for l in range(DEPTH):
        mod = cond @ w_ada[l] + b_ada[l]
        sh1, sc1, g1, sh2, sc2, g2 = [m[:, None, :] for m in jnp.split(mod, N_MOD, axis=-1)]
        u = layer_norm(x, eps=ADA_EPS) * (1.0 + sc1) + sh1
        h = mixing_sublayer(u, w_in[l], q_norm_g[l], k_norm_g[l], w_fourier[l],
                            w_attn_o[l], w_out[l], ang_r, ang_c)
        x = layer_norm(DEEPNORM_ALPHA * x + g1 * h, ln1_g[l], ln1_b[l])
        u = layer_norm(x, eps=ADA_EPS) * (1.0 + sc2) + sh2
        h = moe_sublayer(u, w_router[l], b_router[l], w_gate_up[l], b_gate_up[l],
                         w_down[l], b_down[l])
        x = layer_norm(DEEPNORM_ALPHA * x + g2 * h, ln2_g[l], ln2_b[l])
    return x
```

```python
import functools
import math

import jax
import jax.numpy as jnp
import numpy as np
from jax import lax
from jax.experimental import pallas as pl
from jax.experimental.pallas import tpu as pltpu

F32 = jnp.float32
BF16 = jnp.bfloat16

D_MODEL = 1024
GRID_W = 64
N_Q_HEADS = 8
N_KV_HEADS = 2
HEAD_DIM = 128
Q_GROUP = N_Q_HEADS // N_KV_HEADS
ROPE_THETA = 10000.0
ROPE_AXIS_DIM = HEAD_DIM // 2
N_FOURIER_GROUPS = 4
FOURIER_GROUP_DIM = 128
FOURIER_DIM = N_FOURIER_GROUPS * FOURIER_GROUP_DIM
Q_DIM = N_Q_HEADS * HEAD_DIM
KV_DIM = N_KV_HEADS * HEAD_DIM
N_EXPERTS = 32
TOP_K = 4
D_EXPERT = 1024
SWIGLU_LIMIT = 7.0
SWIGLU_ALPHA = 1.702
N_MOD = 6
DEPTH = 1
DEEPNORM_ALPHA = (2 * DEPTH) ** 0.25
LN_EPS = 1e-5
ADA_EPS = 1e-6
QK_EPS = 1e-6
LOG2E = 1.4426950408889634

V7X_VMEM_BYTES = 64 * 1024 * 1024
VMEM_LIMIT = V7X_VMEM_BYTES - 8 * 1024 * 1024
LANES = 128

FFT_Q = 128
ROW_TILE = 512
ATTN_TQ = 256
ATTN_TK = 512
MOE_BLOCK = 512


def _cparams(sem):
    return pltpu.CompilerParams(dimension_semantics=sem, vmem_limit_bytes=VMEM_LIMIT)


def _layer_norm(x, eps):
    mu = jnp.mean(x, axis=-1, keepdims=True)
    xc = x - mu
    var = jnp.mean(xc * xc, axis=-1, keepdims=True)
    return xc * lax.rsqrt(var + eps)


def _pack_bf16_pairs(x):
    n = x.shape[1] // 2
    bits = pltpu.bitcast(x.astype(BF16).astype(F32), jnp.uint32)
    word = (bits[:, :n] >> 16) | (bits[:, n:] & jnp.uint32(0xFFFF0000))
    return pltpu.bitcast(word, jnp.int32)


def _unpack_bf16_pairs(w):
    bits = pltpu.bitcast(w, jnp.uint32)
    lo = pltpu.bitcast(bits << 16, F32)
    hi = pltpu.bitcast(bits & jnp.uint32(0xFFFF0000), F32)
    return lo, hi


def _ada_kernel(c_ref, w_ref, b_ref, o_ref):
    c = c_ref[...]
    cond = c * jax.nn.sigmoid(c)
    o_ref[...] = jnp.dot(cond, w_ref[...], preferred_element_type=F32,
                         precision=lax.Precision.HIGHEST) + b_ref[...]


def _ada(c, w_ada, b_ada):
    bsz, d = c.shape
    n = w_ada.shape[1]
    tn = 1536
    return pl.pallas_call(
        _ada_kernel,
        grid=(n // tn,),
        in_specs=[pl.BlockSpec((bsz, d), lambda j: (0, 0)),
                  pl.BlockSpec((d, tn), lambda j: (0, j)),
                  pl.BlockSpec((1, tn), lambda j: (0, j))],
        out_specs=pl.BlockSpec((bsz, tn), lambda j: (0, j)),
        out_shape=jax.ShapeDtypeStruct((bsz, n), F32),
        compiler_params=_cparams(("arbitrary",)),
        name="ada",
    )(c, w_ada, b_ada.reshape(1, n))


def _rope(xh, cos, sin_a, sin_b):
    a = pltpu.roll(xh, 32, 1)
    b = pltpu.roll(xh, 96, 1)
    return xh * cos + b * sin_b + a * sin_a


def _inproj_kernel(x_ref, sc_ref, sh_ref, w_ref, dft_ref, qg_ref, kg_ref,
                   cos_ref, sa_ref, sb_ref,
                   fa_ref, fb_ref, q_ref, k_ref, v_ref, gf_ref, ga_ref):
    x = x_ref[...]
    u = _layer_norm(x, ADA_EPS) * (1.0 + sc_ref[...]) + sh_ref[...]
    ub = u.astype(BF16)
    cos, sa, sb = cos_ref[...], sa_ref[...], sb_ref[...]

    c0 = 0
    zf = jnp.dot(ub, w_ref[:, c0:c0 + FOURIER_DIM], preferred_element_type=F32).astype(BF16)
    for g in range(N_FOURIER_GROUPS):
        sl = slice(g * FOURIER_GROUP_DIM, (g + 1) * FOURIER_GROUP_DIM)
        ab = jnp.dot(zf[:, sl], dft_ref[...], preferred_element_type=F32)
        fa_ref[:, sl] = ab[:, :FOURIER_GROUP_DIM].astype(BF16)
        fb_ref[:, sl] = ab[:, FOURIER_GROUP_DIM:].astype(BF16)
    c0 += FOURIER_DIM

    def norm_rope(z, gain, scale):
        ms = jnp.mean(z * z, axis=-1, keepdims=True)
        zn = z * lax.rsqrt(ms + QK_EPS) * gain
        return (_rope(zn, cos, sa, sb) * scale).astype(BF16)

    q_scale = LOG2E / math.sqrt(HEAD_DIM)
    zq = jnp.dot(ub, w_ref[:, c0:c0 + Q_DIM], preferred_element_type=F32)
    for h in range(N_Q_HEADS):
        sl = slice(h * HEAD_DIM, (h + 1) * HEAD_DIM)
        q_ref[:, sl] = norm_rope(zq[:, sl], qg_ref[...], q_scale)
    c0 += Q_DIM

    zk = jnp.dot(ub, w_ref[:, c0:c0 + KV_DIM], preferred_element_type=F32)
    for h in range(N_KV_HEADS):
        sl = slice(h * HEAD_DIM, (h + 1) * HEAD_DIM)
        k_ref[:, sl] = norm_rope(zk[:, sl], kg_ref[...], 1.0)
    c0 += KV_DIM

    v_ref[...] = jnp.dot(ub, w_ref[:, c0:c0 + KV_DIM], preferred_element_type=F32).astype(BF16)
    c0 += KV_DIM

    zg = jnp.dot(ub, w_ref[:, c0:c0 + D_MODEL], preferred_element_type=F32)
    gf_ref[...] = jax.nn.sigmoid(zg).astype(BF16)
    c0 += D_MODEL
    zg = jnp.dot(ub, w_ref[:, c0:c0 + D_MODEL], preferred_element_type=F32)
    ga_ref[...] = jax.nn.sigmoid(zg).astype(BF16)


def _inproj(x, sc1, sh1, w_in, dft_c, qg, kg, cos, sin_a, sin_b, tm):
    bsz, s, d = x.shape
    ncol = w_in.shape[1]
    row = lambda w: pl.BlockSpec((None, tm, w), lambda b, i: (b, i, 0))
    mod = pl.BlockSpec((None, 1, d), lambda b, i: (b, 0, 0))
    full = lambda shp: pl.BlockSpec(shp, lambda b, i: (0,) * len(shp))
    tab = pl.BlockSpec((tm, HEAD_DIM), lambda b, i: (i, 0))
    outs = [FOURIER_DIM, FOURIER_DIM, Q_DIM, KV_DIM, KV_DIM, D_MODEL, D_MODEL]
    return pl.pallas_call(
        _inproj_kernel,
        grid=(bsz, s // tm),
        in_specs=[row(d), mod, mod, full((d, ncol)), full(dft_c.shape),
                  full((1, HEAD_DIM)), full((1, HEAD_DIM)), tab, tab, tab],
        out_specs=[row(w) for w in outs],
        out_shape=[jax.ShapeDtypeStruct((bsz, s, w), BF16) for w in outs],
        compiler_params=_cparams(("parallel", "parallel")),
        name="inproj",
    )(x, sc1, sh1, w_in, dft_c, qg, kg, cos, sin_a, sin_b)


def _fft1_kernel(a_ref, b_ref, m_ref, cw_ref, sw_ref, zr_ref, zi_ref):
    p = a_ref.shape[0]
    nb, _, c = zr_ref.shape
    ab = jnp.concatenate([a_ref[...], b_ref[...]], axis=0)
    t = jnp.dot(m_ref[...], ab, preferred_element_type=F32)
    for j in range(nb):
        tr = t[:p, j * c:(j + 1) * c]
        ti = t[p:, j * c:(j + 1) * c]
        cw, sw = cw_ref[j], sw_ref[j]
        zr_ref[j] = (tr * cw - ti * sw).astype(BF16)
        zi_ref[j] = (ti * cw + tr * sw).astype(BF16)


def _fft1(fa, fb, m1, cw, sw, p, q):
    bsz, s, c = fa.shape
    nb = 8
    a2 = fa.reshape(bsz, p, q * c)
    b2 = fb.reshape(bsz, p, q * c)
    inp = pl.BlockSpec((None, p, nb * c), lambda b, j: (b, 0, j))
    tw = pl.BlockSpec((nb, p, 1), lambda b, j: (j, 0, 0))
    out = pl.BlockSpec((None, nb, p, c), lambda b, j: (b, j, 0, 0))
    return pl.pallas_call(
        _fft1_kernel,
        grid=(bsz, q // nb),
        in_specs=[inp, inp, pl.BlockSpec(m1.shape, lambda b, j: (0, 0)), tw, tw],
        out_specs=[out, out],
        out_shape=[jax.ShapeDtypeStruct((bsz, q, p, c), BF16)] * 2,
        compiler_params=_cparams(("parallel", "parallel")),
        name="fft1",
    )(a2, b2, m1, cw, sw)


def _fft2_kernel(zr_ref, zi_ref, m_ref, y_ref):
    z = jnp.concatenate([zr_ref[...], zi_ref[...]], axis=0)
    y_ref[...] = jnp.dot(m_ref[...], z, preferred_element_type=F32).astype(BF16)


def _fft2(zr, zi, m2):
    bsz, q, p, c = zr.shape
    ncol = min(4096, p * c)
    z_r = zr.reshape(bsz, q, p * c)
    z_i = zi.reshape(bsz, q, p * c)
    blk = pl.BlockSpec((None, q, ncol), lambda b, j: (b, 0, j))
    y = pl.pallas_call(
        _fft2_kernel,
        grid=(bsz, p * c // ncol),
        in_specs=[blk, blk, pl.BlockSpec(m2.shape, lambda b, j: (0, 0))],
        out_specs=blk,
        out_shape=jax.ShapeDtypeStruct((bsz, q, p * c), BF16),
        compiler_params=_cparams(("parallel", "parallel")),
        name="fft2",
    )(z_r, z_i, m2)
    return y.reshape(bsz, q * p, c)


def _dft_tables(s):
    q = FFT_Q
    p = s // q
    two_pi = 2.0 * math.pi

    def cs(n_rows, n_cols, n):
        r = jnp.arange(n_rows, dtype=jnp.int32)[:, None]
        c = jnp.arange(n_cols, dtype=jnp.int32)[None, :]
        ang = ((r * c) % n).astype(F32) * (two_pi / n)
        return jnp.cos(ang), jnp.sin(ang)

    cc, sc = cs(FOURIER_GROUP_DIM, FOURIER_GROUP_DIM, FOURIER_GROUP_DIM)
    dft_c = jnp.concatenate([cc, sc], axis=1).astype(BF16)
    cp, sp = cs(p, p, p)
    m1 = jnp.concatenate([jnp.concatenate([cp, -sp], axis=1),
                          jnp.concatenate([sp, cp], axis=1)], axis=0).astype(BF16)
    cq, sq = cs(q, q, q)
    scale = 1.0 / math.sqrt(s * FOURIER_GROUP_DIM)
    m2 = (jnp.concatenate([cq, -sq], axis=1) * scale).astype(BF16)
    cw, sw = cs(q, p, s)
    return dft_c, m1, m2, cw[:, :, None], sw[:, :, None], p, q


def _rope_tables(s):
    rows = s // GRID_W
    row_ids = jnp.repeat(jnp.arange(rows, dtype=F32), GRID_W)
    col_ids = jnp.tile(jnp.arange(GRID_W, dtype=F32), rows)
    freqs = ROPE_THETA ** (-jnp.arange(0, ROPE_AXIS_DIM, 2, dtype=F32) / ROPE_AXIS_DIM)
    ang_r = row_ids[:, None] * freqs
    ang_c = col_ids[:, None] * freqs
    cr, sr, cc, sc = jnp.cos(ang_r), jnp.sin(ang_r), jnp.cos(ang_c), jnp.sin(ang_c)
    z = jnp.zeros_like(sr)
    cos = jnp.concatenate([cr, cr, cc, cc], axis=1)
    sin_b = jnp.concatenate([-sr, z, -sc, z], axis=1)
    sin_a = jnp.concatenate([z, sr, z, sc], axis=1)
    return cos, sin_a, sin_b


def _attn_kernel(q_ref, k_ref, v_ref, o_ref, *, tk):
    tq = q_ref.shape[0]
    s = k_ref.shape[0]
    q4 = jnp.concatenate([q_ref[:, g * HEAD_DIM:(g + 1) * HEAD_DIM] for g in range(Q_GROUP)], axis=0)
    rows = Q_GROUP * tq

    def body(c, carry):
        m, l, acc = carry
        off = pl.multiple_of(c * tk, tk)
        kc = k_ref[pl.ds(off, tk), :]
        vc = v_ref[pl.ds(off, tk), :]
        sc = lax.dot_general(q4, kc, (((1,), (1,)), ((), ())), preferred_element_type=F32)
        m_new = jnp.maximum(m, jnp.max(sc, axis=-1, keepdims=True))
        alpha = jnp.exp2(m - m_new)
        p = jnp.exp2(sc - m_new)
        l = alpha * l + jnp.sum(p, axis=-1, keepdims=True)
        acc = alpha * acc + jnp.dot(p.astype(BF16), vc, preferred_element_type=F32)
        return m_new, l, acc

    init = (jnp.full((rows, 1), -jnp.inf, F32), jnp.zeros((rows, 1), F32),
            jnp.zeros((rows, HEAD_DIM), F32))
    _, l, acc = lax.fori_loop(0, s // tk, body, init)
    out = acc / l
    for g in range(Q_GROUP):
        o_ref[:, g * HEAD_DIM:(g + 1) * HEAD_DIM] = out[g * tq:(g + 1) * tq].astype(BF16)


def _attention(q, k, v, tq, tk):
    bsz, s, _ = q.shape
    gw = Q_GROUP * HEAD_DIM
    qspec = pl.BlockSpec((None, tq, gw), lambda b, h, i: (b, i, h))
    kvspec = pl.BlockSpec((None, s, HEAD_DIM), lambda b, h, i: (b, 0, h))
    return pl.pallas_call(
        functools.partial(_attn_kernel, tk=tk),
        grid=(bsz, N_KV_HEADS, s // tq),
        in_specs=[qspec, kvspec, kvspec],
        out_specs=qspec,
        out_shape=jax.ShapeDtypeStruct((bsz, s, Q_DIM), BF16),
        compiler_params=_cparams(("parallel", "parallel", "parallel")),
        name="attn",
    )(q, k, v)


def _mix_kernel(yf_ref, at_ref, gf_ref, ga_ref, x_ref, g1_ref, sc2_ref, sh2_ref,
                wf_ref, wo_ref, wout_ref, l1g_ref, l1b_ref, wr_ref, br_ref, tri_ref,
                x1_ref, u2_ref, idx_ref, wgt_ref, rank_ref, cnt_ref, cnt_acc):
    first = jnp.logical_and(pl.program_id(0) == 0, pl.program_id(1) == 0)

    @pl.when(first)
    def _():
        cnt_acc[...] = jnp.zeros_like(cnt_acc)

    yf = jnp.dot(yf_ref[...], wf_ref[...], preferred_element_type=F32)
    ya = jnp.dot(at_ref[...], wo_ref[...], preferred_element_type=F32)
    m = gf_ref[...].astype(F32) * yf + ga_ref[...].astype(F32) * ya
    h = jnp.dot(m.astype(BF16), wout_ref[...], preferred_element_type=F32)
    r = DEEPNORM_ALPHA * x_ref[...] + g1_ref[...] * h
    x1 = _layer_norm(r, LN_EPS) * l1g_ref[...] + l1b_ref[...]
    x1_ref[...] = x1
    u2 = _layer_norm(x1, ADA_EPS) * (1.0 + sc2_ref[...]) + sh2_ref[...]
    u2_ref[...] = _pack_bf16_pairs(u2)

    nt = (((1,), (1,)), ((), ()))
    u_hi = u2.astype(BF16)
    u_lo = (u2 - u_hi.astype(F32)).astype(BF16)
    w = wr_ref[...]
    w_hi = w.astype(BF16)
    w_lo = (w - w_hi.astype(F32)).astype(BF16)
    logits = (lax.dot_general(w_hi, u_hi, nt, preferred_element_type=F32)
              + lax.dot_general(w_lo, u_hi, nt, preferred_element_type=F32)
              + lax.dot_general(w_hi, u_lo, nt, preferred_element_type=F32)) + br_ref[...]

    ne, tm = logits.shape
    eid = lax.broadcasted_iota(jnp.int32, (ne, tm), 0)
    work = logits
    vals, idxs, hots = [], [], []
    for _ in range(TOP_K):
        mx = jnp.max(work, axis=0, keepdims=True)
        ix = jnp.min(jnp.where(work == mx, eid, ne), axis=0, keepdims=True)
        hot = eid == ix
        work = jnp.where(hot, -jnp.inf, work)
        vals.append(mx)
        idxs.append(ix)
        hots.append(hot)
    ex = [jnp.exp(v - vals[0]) for v in vals]
    den = ex[0] + ex[1] + ex[2] + ex[3]
    wgt_ref[...] = jnp.concatenate([e / den for e in ex], axis=0)
    idx_ref[...] = jnp.concatenate(idxs, axis=0)

    sel = hots[0] | hots[1] | hots[2] | hots[3]
    mask = jnp.where(sel, 1.0, 0.0)
    prefix = jnp.dot(mask.astype(BF16), tri_ref[...], preferred_element_type=F32)
    pos = prefix + cnt_acc[...]
    ranks = [jnp.sum(jnp.where(hot, pos, 0.0), axis=0, keepdims=True) for hot in hots]
    rank_ref[...] = jnp.concatenate(ranks, axis=0).astype(jnp.int32)
    cnt_acc[...] += jnp.sum(mask, axis=1, keepdims=True)
    cnt_ref[...] = cnt_acc[...].astype(jnp.int32)


def _mix(yfm, attn, gf, ga, x, g1, sc2, sh2, wf, wo, wout, l1g, l1b, wr_t, br, tm):
    bsz, s, d = x.shape
    n = bsz * s
    nt = s // tm
    tri = (jnp.arange(tm)[:, None] < jnp.arange(tm)[None, :]).astype(BF16)
    row = lambda w: pl.BlockSpec((None, tm, w), lambda b, i: (b, i, 0))
    mod = pl.BlockSpec((None, 1, d), lambda b, i: (b, 0, 0))
    full = lambda shp: pl.BlockSpec(shp, lambda b, i: (0,) * len(shp))
    tok = pl.BlockSpec((TOP_K, tm), lambda b, i: (0, b * nt + i))
    return pl.pallas_call(
        _mix_kernel,
        grid=(bsz, nt),
        in_specs=[row(FOURIER_DIM), row(Q_DIM), row(d), row(d), row(d), mod, mod, mod,
                  full(wf.shape), full(wo.shape), full(wout.shape), full((1, d)), full((1, d)),
                  full(wr_t.shape), full((N_EXPERTS, 1)), full((tm, tm))],
        out_specs=[row(d), row(d // 2), tok, tok, tok, full((N_EXPERTS, 1))],
        out_shape=[jax.ShapeDtypeStruct((bsz, s, d), F32),
                   jax.ShapeDtypeStruct((bsz, s, d // 2), jnp.int32),
                   jax.ShapeDtypeStruct((TOP_K, n), jnp.int32),
                   jax.ShapeDtypeStruct((TOP_K, n), F32),
                   jax.ShapeDtypeStruct((TOP_K, n), jnp.int32),
                   jax.ShapeDtypeStruct((N_EXPERTS, 1), jnp.int32)],
        scratch_shapes=[pltpu.VMEM((N_EXPERTS, 1), F32)],
        compiler_params=_cparams(("arbitrary", "arbitrary")),
        name="mix",
    )(yfm, attn, gf, ga, x, g1, sc2, sh2, wf, wo, wout, l1g, l1b, wr_t, br, tri)


def _expert_kernel(be_ref, xs_ref, wgu_ref, bgu_ref, wd_ref, bd_ref, ys_ref):
    del be_ref
    lo, hi = _unpack_bf16_pairs(xs_ref[...])
    xb = jnp.concatenate([lo.astype(BF16), hi.astype(BF16)], axis=1)
    h = jnp.dot(xb, wgu_ref[...], preferred_element_type=F32) + bgu_ref[...]
    gate = jnp.minimum(h[:, :D_EXPERT], SWIGLU_LIMIT)
    up = jnp.clip(h[:, D_EXPERT:], -SWIGLU_LIMIT, SWIGLU_LIMIT)
    act = (up + 1.0) * gate * jax.nn.sigmoid(SWIGLU_ALPHA * gate)
    y = jnp.dot(act.astype(BF16), wd_ref[...], preferred_element_type=F32) + bd_ref[...]
    ys_ref[...] = _pack_bf16_pairs(y)


def _experts(block_e, xs, wgu, bgu, wd, bd, tmb):
    n_pad, dh = xs.shape
    d = 2 * dh
    grid_spec = pltpu.PrefetchScalarGridSpec(
        num_scalar_prefetch=1,
        grid=(n_pad // tmb,),
        in_specs=[pl.BlockSpec((tmb, dh), lambda i, be: (i, 0)),
                  pl.BlockSpec((None, d, 2 * D_EXPERT), lambda i, be: (be[i], 0, 0)),
                  pl.BlockSpec((None, 1, 2 * D_EXPERT), lambda i, be: (be[i], 0, 0)),
                  pl.BlockSpec((None, D_EXPERT, d), lambda i, be: (be[i], 0, 0)),
                  pl.BlockSpec((None, 1, d), lambda i, be: (be[i], 0, 0))],
        out_specs=pl.BlockSpec((tmb, dh), lambda i, be: (i, 0)),
    )
    return pl.pallas_call(
        _expert_kernel,
        grid_spec=grid_spec,
        out_shape=jax.ShapeDtypeStruct((n_pad, dh), jnp.int32),
        compiler_params=_cparams(("arbitrary",)),
        name="experts",
    )(block_e, xs, wgu, bgu, wd, bd)


def _final_kernel(yg_ref, w_ref, x1_ref, g2_ref, lg_ref, lb_ref, o_ref):
    w = w_ref[...]
    acc_lo = None
    for j in range(TOP_K):
        lo, hi = _unpack_bf16_pairs(yg_ref[j])
        wj = w[:, j:j + 1]
        acc_lo = lo * wj if acc_lo is None else acc_lo + lo * wj
        acc_hi = hi * wj if j == 0 else acc_hi + hi * wj
    h = jnp.concatenate([acc_lo, acc_hi], axis=1)
    r = DEEPNORM_ALPHA * x1_ref[...] + g2_ref[...] * h
    o_ref[...] = _layer_norm(r, LN_EPS) * lg_ref[...] + lb_ref[...]


def _final(yg, w_tok, x1, g2, lg, lb, tm):
    bsz, s, d = x1.shape
    nt = s // tm
    row = pl.BlockSpec((None, tm, d), lambda b, i: (b, i, 0))
    full = pl.BlockSpec((1, d), lambda b, i: (0, 0))
    return pl.pallas_call(
        _final_kernel,
        grid=(bsz, nt),
        in_specs=[pl.BlockSpec((TOP_K, tm, d // 2), lambda b, i: (0, b * nt + i, 0)),
                  pl.BlockSpec((tm, TOP_K), lambda b, i: (b * nt + i, 0)),
                  row, pl.BlockSpec((None, 1, d), lambda b, i: (b, 0, 0)), full, full],
        out_specs=row,
        out_shape=jax.ShapeDtypeStruct((bsz, s, d), F32),
        compiler_params=_cparams(("parallel", "parallel")),
        name="final",
    )(yg, w_tok, x1, g2, lg, lb)


def _gather_rows(table, idx):
    return jnp.take(table, idx, axis=0)


def kernel(x, c, w_ada, b_ada, w_in, q_norm_g, k_norm_g, w_fourier, w_attn_o, w_out, ln1_g, ln1_b,
           w_router, b_router, w_gate_up, b_gate_up, w_down, b_down, ln2_g, ln2_b):
    bsz, s, d = x.shape
    n = bsz * s
    tm = min(ROW_TILE, s)
    l = 0

    mod = _ada(c, w_ada[l], b_ada[l])
    sh1, sc1, g1, sh2, sc2, g2 = [m[:, None, :] for m in jnp.split(mod, N_MOD, axis=-1)]

    dft_c, m1, m2, cw, sw, p, q = _dft_tables(s)
    cos, sin_a, sin_b = _rope_tables(s)

    fa, fb, qh, kh, vh, gf, ga = _inproj(
        x, sc1, sh1, w_in[l].astype(BF16), dft_c,
        q_norm_g[l].reshape(1, HEAD_DIM), k_norm_g[l].reshape(1, HEAD_DIM), cos, sin_a, sin_b, tm)

    zr, zi = _fft1(fa, fb, m1, cw, sw, p, q)
    yfm = _fft2(zr, zi, m2)

    attn = _attention(qh, kh, vh, min(ATTN_TQ, s), min(ATTN_TK, s))

    x1, u2p, idx_t, wgt_t, rank_t, counts = _mix(
        yfm, attn, gf, ga, x, g1, sc2, sh2,
        w_fourier[l].astype(BF16), w_attn_o[l].astype(BF16), w_out[l].astype(BF16),
        ln1_g[l].reshape(1, d), ln1_b[l].reshape(1, d),
        w_router[l].T, b_router[l].reshape(N_EXPERTS, 1), tm)

    tmb = MOE_BLOCK
    counts = counts.reshape(N_EXPERTS)
    pcounts = ((counts + tmb - 1) // tmb) * tmb
    pends = jnp.cumsum(pcounts)
    pstarts = pends - pcounts
    n_pad = n * TOP_K + N_EXPERTS * tmb
    nblk = n_pad // tmb
    dest_t = pstarts[idx_t] + rank_t
    tok_ids = jnp.broadcast_to(jnp.arange(n, dtype=jnp.int32)[None, :], (TOP_K, n))
    src_tok = jnp.zeros((n_pad,), jnp.int32).at[dest_t.reshape(-1)].set(tok_ids.reshape(-1))
    block_starts = jnp.arange(nblk, dtype=jnp.int32) * tmb
    block_e = jnp.clip(jnp.searchsorted(pends, block_starts, side="right"), 0, N_EXPERTS - 1)
    block_e = block_e.astype(jnp.int32)

    xs = _gather_rows(u2p.reshape(n, d // 2), src_tok)
    ys = _experts(block_e, xs, w_gate_up[l].astype(BF16), b_gate_up[l][:, None, :],
                  w_down[l].astype(BF16), b_down[l][:, None, :], tmb)
    yg = _gather_rows(ys, dest_t.reshape(-1)).reshape(TOP_K, n, d // 2)

    return _final(yg, wgt_t.T, x1, g2, ln2_g[l].reshape(1, d), ln2_b[l].reshape(1, d), tm)
```

```python
import functools
import math

import jax
import jax.numpy as jnp
import numpy as np
from jax import lax
from jax.experimental import pallas as pl
from jax.experimental.pallas import tpu as pltpu
from jax.experimental.pallas import tpu_sc as plsc

F32 = jnp.float32
BF16 = jnp.bfloat16

D_MODEL = 1024
GRID_W = 64
N_Q_HEADS = 8
N_KV_HEADS = 2
HEAD_DIM = 128
Q_GROUP = N_Q_HEADS // N_KV_HEADS
ROPE_THETA = 10000.0
ROPE_AXIS_DIM = HEAD_DIM // 2
N_FOURIER_GROUPS = 4
FOURIER_GROUP_DIM = 128
FOURIER_DIM = N_FOURIER_GROUPS * FOURIER_GROUP_DIM
Q_DIM = N_Q_HEADS * HEAD_DIM
KV_DIM = N_KV_HEADS * HEAD_DIM
N_EXPERTS = 32
TOP_K = 4
D_EXPERT = 1024
SWIGLU_LIMIT = 7.0
SWIGLU_ALPHA = 1.702
N_MOD = 6
DEPTH = 1
DEEPNORM_ALPHA = (2 * DEPTH) ** 0.25
LN_EPS = 1e-5
ADA_EPS = 1e-6
QK_EPS = 1e-6
LOG2E = 1.4426950408889634

V7X_VMEM_BYTES = 64 * 1024 * 1024
VMEM_LIMIT = V7X_VMEM_BYTES - 8 * 1024 * 1024
LANES = 128

FFT_Q = 128
ROW_TILE = 512
ATTN_TQ = 256
ATTN_TK = 512
MOE_BLOCK = 512

SC_CORES = 2
SC_SUBCORES = 16
GATHER_CHUNK = 64


def _cparams(sem):
    return pltpu.CompilerParams(dimension_semantics=sem, vmem_limit_bytes=VMEM_LIMIT)


def _layer_norm(x, eps):
    mu = jnp.mean(x, axis=-1, keepdims=True)
    xc = x - mu
    var = jnp.mean(xc * xc, axis=-1, keepdims=True)
    return xc * lax.rsqrt(var + eps)


def _pack_bf16_pairs(x):
    n = x.shape[1] // 2
    bits = pltpu.bitcast(x.astype(BF16).astype(F32), jnp.uint32)
    word = (bits[:, :n] >> 16) | (bits[:, n:] & jnp.uint32(0xFFFF0000))
    return pltpu.bitcast(word, jnp.int32)


def _unpack_bf16_pairs(w):
    bits = pltpu.bitcast(w, jnp.uint32)
    lo = pltpu.bitcast(bits << 16, F32)
    hi = pltpu.bitcast(bits & jnp.uint32(0xFFFF0000), F32)
    return lo, hi


def _ada_kernel(c_ref, w_ref, b_ref, o_ref):
    c = c_ref[...]
    cond = c * jax.nn.sigmoid(c)
    o_ref[...] = jnp.dot(cond, w_ref[...], preferred_element_type=F32,
                         precision=lax.Precision.HIGHEST) + b_ref[...]


def _ada(c, w_ada, b_ada):
    bsz, d = c.shape
    n = w_ada.shape[1]
    tn = 1536
    return pl.pallas_call(
        _ada_kernel,
        grid=(n // tn,),
        in_specs=[pl.BlockSpec((bsz, d), lambda j: (0, 0)),
                  pl.BlockSpec((d, tn), lambda j: (0, j)),
                  pl.BlockSpec((1, tn), lambda j: (0, j))],
        out_specs=pl.BlockSpec((bsz, tn), lambda j: (0, j)),
        out_shape=jax.ShapeDtypeStruct((bsz, n), F32),
        compiler_params=_cparams(("arbitrary",)),
        name="ada",
    )(c, w_ada, b_ada.reshape(1, n))


def _rope(xh, cos, sin_a, sin_b):
    a = pltpu.roll(xh, 32, 1)
    b = pltpu.roll(xh, 96, 1)
    return xh * cos + b * sin_b + a * sin_a


def _inproj_kernel(x_ref, sc_ref, sh_ref, w_ref, dft_ref, qg_ref, kg_ref,
                   cos_ref, sa_ref, sb_ref,
                   fa_ref, fb_ref, q_ref, k_ref, v_ref, gf_ref, ga_ref):
    x = x_ref[...]
    u = _layer_norm(x, ADA_EPS) * (1.0 + sc_ref[...]) + sh_ref[...]
    ub = u.astype(BF16)
    cos, sa, sb = cos_ref[...], sa_ref[...], sb_ref[...]

    c0 = 0
    zf = jnp.dot(ub, w_ref[:, c0:c0 + FOURIER_DIM], preferred_element_type=F32).astype(BF16)
    for g in range(N_FOURIER_GROUPS):
        sl = slice(g * FOURIER_GROUP_DIM, (g + 1) * FOURIER_GROUP_DIM)
        ab = jnp.dot(zf[:, sl], dft_ref[...], preferred_element_type=F32)
        fa_ref[:, sl] = ab[:, :FOURIER_GROUP_DIM].astype(BF16)
        fb_ref[:, sl] = ab[:, FOURIER_GROUP_DIM:].astype(BF16)
    c0 += FOURIER_DIM

    def norm_rope(z, gain, scale):
        ms = jnp.mean(z * z, axis=-1, keepdims=True)
        zn = z * lax.rsqrt(ms + QK_EPS) * gain
        return (_rope(zn, cos, sa, sb) * scale).astype(BF16)

    q_scale = LOG2E / math.sqrt(HEAD_DIM)
    zq = jnp.dot(ub, w_ref[:, c0:c0 + Q_DIM], preferred_element_type=F32)
    for h in range(N_Q_HEADS):
        sl = slice(h * HEAD_DIM, (h + 1) * HEAD_DIM)
        q_ref[:, sl] = norm_rope(zq[:, sl], qg_ref[...], q_scale)
    c0 += Q_DIM

    zk = jnp.dot(ub, w_ref[:, c0:c0 + KV_DIM], preferred_element_type=F32)
    for h in range(N_KV_HEADS):
        sl = slice(h * HEAD_DIM, (h + 1) * HEAD_DIM)
        k_ref[:, sl] = norm_rope(zk[:, sl], kg_ref[...], 1.0)
    c0 += KV_DIM

    v_ref[...] = jnp.dot(ub, w_ref[:, c0:c0 + KV_DIM], preferred_element_type=F32).astype(BF16)
    c0 += KV_DIM

    zg = jnp.dot(ub, w_ref[:, c0:c0 + D_MODEL], preferred_element_type=F32)
    gf_ref[...] = jax.nn.sigmoid(zg).astype(BF16)
    c0 += D_MODEL
    zg = jnp.dot(ub, w_ref[:, c0:c0 + D_MODEL], preferred_element_type=F32)
    ga_ref[...] = jax.nn.sigmoid(zg).astype(BF16)


def _inproj(x, sc1, sh1, w_in, dft_c, qg, kg, cos, sin_a, sin_b, tm):
    bsz, s, d = x.shape
    ncol = w_in.shape[1]
    row = lambda w: pl.BlockSpec((None, tm, w), lambda b, i: (b, i, 0))
    mod = pl.BlockSpec((None, 1, d), lambda b, i: (b, 0, 0))
    full = lambda shp: pl.BlockSpec(shp, lambda b, i: (0,) * len(shp))
    tab = pl.BlockSpec((tm, HEAD_DIM), lambda b, i: (i, 0))
    outs = [FOURIER_DIM, FOURIER_DIM, Q_DIM, KV_DIM, KV_DIM, D_MODEL, D_MODEL]
    return pl.pallas_call(
        _inproj_kernel,
        grid=(bsz, s // tm),
        in_specs=[row(d), mod, mod, full((d, ncol)), full(dft_c.shape),
                  full((1, HEAD_DIM)), full((1, HEAD_DIM)), tab, tab, tab],
        out_specs=[row(w) for w in outs],
        out_shape=[jax.ShapeDtypeStruct((bsz, s, w), BF16) for w in outs],
        compiler_params=_cparams(("parallel", "parallel")),
        name="inproj",
    )(x, sc1, sh1, w_in, dft_c, qg, kg, cos, sin_a, sin_b)


def _fft1_kernel(a_ref, b_ref, m_ref, cw_ref, sw_ref, zr_ref, zi_ref):
    p = a_ref.shape[0]
    nb, _, c = zr_ref.shape
    ab = jnp.concatenate([a_ref[...], b_ref[...]], axis=0)
    t = jnp.dot(m_ref[...], ab, preferred_element_type=F32)
    for j in range(nb):
        tr = t[:p, j * c:(j + 1) * c]
        ti = t[p:, j * c:(j + 1) * c]
        cw, sw = cw_ref[j], sw_ref[j]
        zr_ref[j] = (tr * cw - ti * sw).astype(BF16)
        zi_ref[j] = (ti * cw + tr * sw).astype(BF16)


def _fft1(fa, fb, m1, cw, sw, p, q):
    bsz, s, c = fa.shape
    nb = 8
    a2 = fa.reshape(bsz, p, q * c)
    b2 = fb.reshape(bsz, p, q * c)
    inp = pl.BlockSpec((None, p, nb * c), lambda b, j: (b, 0, j))
    tw = pl.BlockSpec((nb, p, 1), lambda b, j: (j, 0, 0))
    out = pl.BlockSpec((None, nb, p, c), lambda b, j: (b, j, 0, 0))
    return pl.pallas_call(
        _fft1_kernel,
        grid=(bsz, q // nb),
        in_specs=[inp, inp, pl.BlockSpec(m1.shape, lambda b, j: (0, 0)), tw, tw],
        out_specs=[out, out],
        out_shape=[jax.ShapeDtypeStruct((bsz, q, p, c), BF16)] * 2,
        compiler_params=_cparams(("parallel", "parallel")),
        name="fft1",
    )(a2, b2, m1, cw, sw)


def _fft2_kernel(zr_ref, zi_ref, m_ref, y_ref):
    z = jnp.concatenate([zr_ref[...], zi_ref[...]], axis=0)
    y_ref[...] = jnp.dot(m_ref[...], z, preferred_element_type=F32).astype(BF16)


def _fft2(zr, zi, m2):
    bsz, q, p, c = zr.shape
    ncol = min(4096, p * c)
    z_r = zr.reshape(bsz, q, p * c)
    z_i = zi.reshape(bsz, q, p * c)
    blk = pl.BlockSpec((None, q, ncol), lambda b, j: (b, 0, j))
    y = pl.pallas_call(
        _fft2_kernel,
        grid=(bsz, p * c // ncol),
        in_specs=[blk, blk, pl.BlockSpec(m2.shape, lambda b, j: (0, 0))],
        out_specs=blk,
        out_shape=jax.ShapeDtypeStruct((bsz, q, p * c), BF16),
        compiler_params=_cparams(("parallel", "parallel")),
        name="fft2",
    )(z_r, z_i, m2)
    return y.reshape(bsz, q * p, c)


def _dft_tables(s):
    q = FFT_Q
    p = s // q
    two_pi = 2.0 * math.pi

    def cs(n_rows, n_cols, n):
        r = jnp.arange(n_rows, dtype=jnp.int32)[:, None]
        c = jnp.arange(n_cols, dtype=jnp.int32)[None, :]
        ang = ((r * c) % n).astype(F32) * (two_pi / n)
        return jnp.cos(ang), jnp.sin(ang)

    cc, sc = cs(FOURIER_GROUP_DIM, FOURIER_GROUP_DIM, FOURIER_GROUP_DIM)
    dft_c = jnp.concatenate([cc, sc], axis=1).astype(BF16)
    cp, sp = cs(p, p, p)
    m1 = jnp.concatenate([jnp.concatenate([cp, -sp], axis=1),
                          jnp.concatenate([sp, cp], axis=1)], axis=0).astype(BF16)
    cq, sq = cs(q, q, q)
    scale = 1.0 / math.sqrt(s * FOURIER_GROUP_DIM)
    m2 = (jnp.concatenate([cq, -sq], axis=1) * scale).astype(BF16)
    cw, sw = cs(q, p, s)
    return dft_c, m1, m2, cw[:, :, None], sw[:, :, None], p, q


def _rope_tables(s):
    rows = s // GRID_W
    row_ids = jnp.repeat(jnp.arange(rows, dtype=F32), GRID_W)
    col_ids = jnp.tile(jnp.arange(GRID_W, dtype=F32), rows)
    freqs = ROPE_THETA ** (-jnp.arange(0, ROPE_AXIS_DIM, 2, dtype=F32) / ROPE_AXIS_DIM)
    ang_r = row_ids[:, None] * freqs
    ang_c = col_ids[:, None] * freqs
    cr, sr, cc, sc = jnp.cos(ang_r), jnp.sin(ang_r), jnp.cos(ang_c), jnp.sin(ang_c)
    z = jnp.zeros_like(sr)
    cos = jnp.concatenate([cr, cr, cc, cc], axis=1)
    sin_b = jnp.concatenate([-sr, z, -sc, z], axis=1)
    sin_a = jnp.concatenate([z, sr, z, sc], axis=1)
    return cos, sin_a, sin_b


def _attn_kernel(q_ref, k_ref, v_ref, o_ref, *, tk):
    tq = q_ref.shape[0]
    s = k_ref.shape[0]
    q4 = jnp.concatenate([q_ref[:, g * HEAD_DIM:(g + 1) * HEAD_DIM] for g in range(Q_GROUP)], axis=0)
    rows = Q_GROUP * tq

    def body(c, carry):
        m, l, acc = carry
        off = pl.multiple_of(c * tk, tk)
        kc = k_ref[pl.ds(off, tk), :]
        vc = v_ref[pl.ds(off, tk), :]
        sc = lax.dot_general(q4, kc, (((1,), (1,)), ((), ())), preferred_element_type=F32)
        m_new = jnp.maximum(m, jnp.max(sc, axis=-1, keepdims=True))
        alpha = jnp.exp2(m - m_new)
        p = jnp.exp2(sc - m_new)
        l = alpha * l + jnp.sum(p, axis=-1, keepdims=True)
        acc = alpha * acc + jnp.dot(p.astype(BF16), vc, preferred_element_type=F32)
        return m_new, l, acc

    init = (jnp.full((rows, 1), -jnp.inf, F32), jnp.zeros((rows, 1), F32),
            jnp.zeros((rows, HEAD_DIM), F32))
    _, l, acc = lax.fori_loop(0, s // tk, body, init)
    out = acc / l
    for g in range(Q_GROUP):
        o_ref[:, g * HEAD_DIM:(g + 1) * HEAD_DIM] = out[g * tq:(g + 1) * tq].astype(BF16)


def _attention(q, k, v, tq, tk):
    bsz, s, _ = q.shape
    gw = Q_GROUP * HEAD_DIM
    qspec = pl.BlockSpec((None, tq, gw), lambda b, h, i: (b, i, h))
    kvspec = pl.BlockSpec((None, s, HEAD_DIM), lambda b, h, i: (b, 0, h))
    return pl.pallas_call(
        functools.partial(_attn_kernel, tk=tk),
        grid=(bsz, N_KV_HEADS, s // tq),
        in_specs=[qspec, kvspec, kvspec],
        out_specs=qspec,
        out_shape=jax.ShapeDtypeStruct((bsz, s, Q_DIM), BF16),
        compiler_params=_cparams(("parallel", "parallel", "parallel")),
        name="attn",
    )(q, k, v)


def _mix_kernel(yf_ref, at_ref, gf_ref, ga_ref, x_ref, g1_ref, sc2_ref, sh2_ref,
                wf_ref, wo_ref, wout_ref, l1g_ref, l1b_ref, wr_ref, br_ref, tri_ref,
                x1_ref, u2_ref, idx_ref, wgt_ref, rank_ref, cnt_ref, cnt_acc):
    first = jnp.logical_and(pl.program_id(0) == 0, pl.program_id(1) == 0)

    @pl.when(first)
    def _():
        cnt_acc[...] = jnp.zeros_like(cnt_acc)

    yf = jnp.dot(yf_ref[...], wf_ref[...], preferred_element_type=F32)
    ya = jnp.dot(at_ref[...], wo_ref[...], preferred_element_type=F32)
    m = gf_ref[...].astype(F32) * yf + ga_ref[...].astype(F32) * ya
    h = jnp.dot(m.astype(BF16), wout_ref[...], preferred_element_type=F32)
    r = DEEPNORM_ALPHA * x_ref[...] + g1_ref[...] * h
    x1 = _layer_norm(r, LN_EPS) * l1g_ref[...] + l1b_ref[...]
    x1_ref[...] = x1
    u2 = _layer_norm(x1, ADA_EPS) * (1.0 + sc2_ref[...]) + sh2_ref[...]
    u2_ref[...] = _pack_bf16_pairs(u2)

    nt = (((1,), (1,)), ((), ()))
    u_hi = u2.astype(BF16)
    u_lo = (u2 - u_hi.astype(F32)).astype(BF16)
    w = wr_ref[...]
    w_hi = w.astype(BF16)
    w_lo = (w - w_hi.astype(F32)).astype(BF16)
    logits = (lax.dot_general(w_hi, u_hi, nt, preferred_element_type=F32)
              + lax.dot_general(w_lo, u_hi, nt, preferred_element_type=F32)
              + lax.dot_general(w_hi, u_lo, nt, preferred_element_type=F32)) + br_ref[...]

    ne, tm = logits.shape
    eid = lax.broadcasted_iota(jnp.int32, (ne, tm), 0)
    work = logits
    vals, idxs, hots = [], [], []
    for _ in range(TOP_K):
        mx = jnp.max(work, axis=0, keepdims=True)
        ix = jnp.min(jnp.where(work == mx, eid, ne), axis=0, keepdims=True)
        hot = eid == ix
        work = jnp.where(hot, -jnp.inf, work)
        vals.append(mx)
        idxs.append(ix)
        hots.append(hot)
    ex = [jnp.exp(v - vals[0]) for v in vals]
    den = ex[0] + ex[1] + ex[2] + ex[3]
    wgt_ref[...] = jnp.concatenate([e / den for e in ex], axis=0)
    idx_ref[...] = jnp.concatenate(idxs, axis=0)

    sel = hots[0] | hots[1] | hots[2] | hots[3]
    mask = jnp.where(sel, 1.0, 0.0)
    prefix = jnp.dot(mask.astype(BF16), tri_ref[...], preferred_element_type=F32)
    pos = prefix + cnt_acc[...]
    ranks = [jnp.sum(jnp.where(hot, pos, 0.0), axis=0, keepdims=True) for hot in hots]
    rank_ref[...] = jnp.concatenate(ranks, axis=0).astype(jnp.int32)
    cnt_acc[...] += jnp.sum(mask, axis=1, keepdims=True)
    cnt_ref[...] = cnt_acc[...].astype(jnp.int32)


def _mix(yfm, attn, gf, ga, x, g1, sc2, sh2, wf, wo, wout, l1g, l1b, wr_t, br, tm):
    bsz, s, d = x.shape
    n = bsz * s
    nt = s // tm
    tri = (jnp.arange(tm)[:, None] < jnp.arange(tm)[None, :]).astype(BF16)
    row = lambda w: pl.BlockSpec((None, tm, w), lambda b, i: (b, i, 0))
    mod = pl.BlockSpec((None, 1, d), lambda b, i: (b, 0, 0))
    full = lambda shp: pl.BlockSpec(shp, lambda b, i: (0,) * len(shp))
    tok = pl.BlockSpec((TOP_K, tm), lambda b, i: (0, b * nt + i))
    return pl.pallas_call(
        _mix_kernel,
        grid=(bsz, nt),
        in_specs=[row(FOURIER_DIM), row(Q_DIM), row(d), row(d), row(d), mod, mod, mod,
                  full(wf.shape), full(wo.shape), full(wout.shape), full((1, d)), full((1, d)),
                  full(wr_t.shape), full((N_EXPERTS, 1)), full((tm, tm))],
        out_specs=[row(d), row(d // 2), tok, tok, tok, full((N_EXPERTS, 1))],
        out_shape=[jax.ShapeDtypeStruct((bsz, s, d), F32),
                   jax.ShapeDtypeStruct((bsz, s, d // 2), jnp.int32),
                   jax.ShapeDtypeStruct((TOP_K, n), jnp.int32),
                   jax.ShapeDtypeStruct((TOP_K, n), F32),
                   jax.ShapeDtypeStruct((TOP_K, n), jnp.int32),
                   jax.ShapeDtypeStruct((N_EXPERTS, 1), jnp.int32)],
        scratch_shapes=[pltpu.VMEM((N_EXPERTS, 1), F32)],
        compiler_params=_cparams(("arbitrary", "arbitrary")),
        name="mix",
    )(yfm, attn, gf, ga, x, g1, sc2, sh2, wf, wo, wout, l1g, l1b, wr_t, br, tri)


def _expert_kernel(be_ref, xs_ref, wgu_ref, bgu_ref, wd_ref, bd_ref, ys_ref):
    del be_ref
    lo, hi = _unpack_bf16_pairs(xs_ref[...])
    xb = jnp.concatenate([lo.astype(BF16), hi.astype(BF16)], axis=1)
    h = jnp.dot(xb, wgu_ref[...], preferred_element_type=F32) + bgu_ref[...]
    gate = jnp.minimum(h[:, :D_EXPERT], SWIGLU_LIMIT)
    up = jnp.clip(h[:, D_EXPERT:], -SWIGLU_LIMIT, SWIGLU_LIMIT)
    act = (up + 1.0) * gate * jax.nn.sigmoid(SWIGLU_ALPHA * gate)
    y = jnp.dot(act.astype(BF16), wd_ref[...], preferred_element_type=F32) + bd_ref[...]
    ys_ref[...] = _pack_bf16_pairs(y)


def _experts(block_e, xs, wgu, bgu, wd, bd, tmb):
    n_pad, dh = xs.shape
    d = 2 * dh
    grid_spec = pltpu.PrefetchScalarGridSpec(
        num_scalar_prefetch=1,
        grid=(n_pad // tmb,),
        in_specs=[pl.BlockSpec((tmb, dh), lambda i, be: (i, 0)),
                  pl.BlockSpec((None, d, 2 * D_EXPERT), lambda i, be: (be[i], 0, 0)),
                  pl.BlockSpec((None, 1, 2 * D_EXPERT), lambda i, be: (be[i], 0, 0)),
                  pl.BlockSpec((None, D_EXPERT, d), lambda i, be: (be[i], 0, 0)),
                  pl.BlockSpec((None, 1, d), lambda i, be: (be[i], 0, 0))],
        out_specs=pl.BlockSpec((tmb, dh), lambda i, be: (i, 0)),
    )
    return pl.pallas_call(
        _expert_kernel,
        grid_spec=grid_spec,
        out_shape=jax.ShapeDtypeStruct((n_pad, dh), jnp.int32),
        compiler_params=_cparams(("arbitrary",)),
        name="experts",
    )(block_e, xs, wgu, bgu, wd, bd)


def _final_kernel(yg_ref, w_ref, x1_ref, g2_ref, lg_ref, lb_ref, o_ref):
    w = w_ref[...]
    acc_lo = None
    for j in range(TOP_K):
        lo, hi = _unpack_bf16_pairs(yg_ref[j])
        wj = w[:, j:j + 1]
        acc_lo = lo * wj if acc_lo is None else acc_lo + lo * wj
        acc_hi = hi * wj if j == 0 else acc_hi + hi * wj
    h = jnp.concatenate([acc_lo, acc_hi], axis=1)
    r = DEEPNORM_ALPHA * x1_ref[...] + g2_ref[...] * h
    o_ref[...] = _layer_norm(r, LN_EPS) * lg_ref[...] + lb_ref[...]


def _final(yg, w_tok, x1, g2, lg, lb, tm):
    bsz, s, d = x1.shape
    nt = s // tm
    row = pl.BlockSpec((None, tm, d), lambda b, i: (b, i, 0))
    full = pl.BlockSpec((1, d), lambda b, i: (0, 0))
    return pl.pallas_call(
        _final_kernel,
        grid=(bsz, nt),
        in_specs=[pl.BlockSpec((TOP_K, tm, d // 2), lambda b, i: (0, b * nt + i, 0)),
                  pl.BlockSpec((tm, TOP_K), lambda b, i: (b * nt + i, 0)),
                  row, pl.BlockSpec((None, 1, d), lambda b, i: (b, 0, 0)), full, full],
        out_specs=row,
        out_shape=jax.ShapeDtypeStruct((bsz, s, d), F32),
        compiler_params=_cparams(("parallel", "parallel")),
        name="final",
    )(yg, w_tok, x1, g2, lg, lb)


def _gather_rows(table, idx):
    m = idx.shape[0]
    w = table.shape[1]
    ch = GATHER_CHUNK
    nw = SC_CORES * SC_SUBCORES
    nch = m // (nw * ch)
    assert m == nw * nch * ch and nch % 2 == 0
    per_w = nch * ch
    mesh = plsc.VectorSubcoreMesh(core_axis_name="c", subcore_axis_name="s",
                                  num_cores=SC_CORES, num_subcores=SC_SUBCORES)

    @functools.partial(
        pl.kernel, mesh=mesh,
        out_type=jax.ShapeDtypeStruct((m, w), table.dtype),
        scratch_types=[pltpu.VMEM((nch, ch), jnp.int32), pltpu.VMEM((2, ch, w), table.dtype),
                       pltpu.SemaphoreType.DMA((2,)), pltpu.SemaphoreType.DMA((2,))],
    )
    def gather_kernel(table_hbm, idx_hbm, out_hbm, idx_v, rows_v, gsem, wsem):
        wid = lax.axis_index("s") * SC_CORES + lax.axis_index("c")
        base = wid * per_w
        pltpu.sync_copy(idx_hbm.at[wid], idx_v)

        def gather(j, b):
            return pltpu.make_async_copy(table_hbm.at[idx_v.at[j]], rows_v.at[b], gsem.at[b])

        def write(j, b):
            off = pl.multiple_of(base + j * ch, ch)
            return pltpu.make_async_copy(rows_v.at[b], out_hbm.at[pl.ds(off, ch)], wsem.at[b])

        gather(0, 0).start()

        @pl.loop(0, nch, step=2)
        def _(i):
            for b in range(2):
                j = i + b
                gather(j, b).wait()

                @pl.when(j + 1 < nch)
                def _():
                    @pl.when(j >= 1)
                    def _():
                        write(j - 1, 1 - b).wait()
                    gather(j + 1, 1 - b).start()

                write(j, b).start()

        write(nch - 2, 0).wait()
        write(nch - 1, 1).wait()

    return gather_kernel(table, idx.reshape(nw, nch, ch))


def kernel(x, c, w_ada, b_ada, w_in, q_norm_g, k_norm_g, w_fourier, w_attn_o, w_out, ln1_g, ln1_b,
           w_router, b_router, w_gate_up, b_gate_up, w_down, b_down, ln2_g, ln2_b):
    bsz, s, d = x.shape
    n = bsz * s
    tm = min(ROW_TILE, s)
    l = 0

    mod = _ada(c, w_ada[l], b_ada[l])
    sh1, sc1, g1, sh2, sc2, g2 = [m[:, None, :] for m in jnp.split(mod, N_MOD, axis=-1)]

    dft_c, m1, m2, cw, sw, p, q = _dft_tables(s)
    cos, sin_a, sin_b = _rope_tables(s)

    fa, fb, qh, kh, vh, gf, ga = _inproj(
        x, sc1, sh1, w_in[l].astype(BF16), dft_c,
        q_norm_g[l].reshape(1, HEAD_DIM), k_norm_g[l].reshape(1, HEAD_DIM), cos, sin_a, sin_b, tm)

    zr, zi = _fft1(fa, fb, m1, cw, sw, p, q)
    yfm = _fft2(zr, zi, m2)

    attn = _attention(qh, kh, vh, min(ATTN_TQ, s), min(ATTN_TK, s))

    x1, u2p, idx_t, wgt_t, rank_t, counts = _mix(
        yfm, attn, gf, ga, x, g1, sc2, sh2,
        w_fourier[l].astype(BF16), w_attn_o[l].astype(BF16), w_out[l].astype(BF16),
        ln1_g[l].reshape(1, d), ln1_b[l].reshape(1, d),
        w_router[l].T, b_router[l].reshape(N_EXPERTS, 1), tm)

    tmb = MOE_BLOCK
    counts = counts.reshape(N_EXPERTS)
    pcounts = ((counts + tmb - 1) // tmb) * tmb
    pends = jnp.cumsum(pcounts)
    pstarts = pends - pcounts
    n_pad = n * TOP_K + N_EXPERTS * tmb
    nblk = n_pad // tmb
    dest_t = pstarts[idx_t] + rank_t
    tok_ids = jnp.broadcast_to(jnp.arange(n, dtype=jnp.int32)[None, :], (TOP_K, n))
    src_tok = jnp.zeros((n_pad,), jnp.int32).at[dest_t.reshape(-1)].set(tok_ids.reshape(-1))
    block_starts = jnp.arange(nblk, dtype=jnp.int32) * tmb
    block_e = jnp.clip(jnp.searchsorted(pends, block_starts, side="right"), 0, N_EXPERTS - 1)
    block_e = block_e.astype(jnp.int32)

    xs = _gather_rows(u2p.reshape(n, d // 2), src_tok)
    ys = _experts(block_e, xs, w_gate_up[l].astype(BF16), b_gate_up[l][:, None, :],
                  w_down[l].astype(BF16), b_down[l][:, None, :], tmb)
    yg = _gather_rows(ys, dest_t.reshape(-1)).reshape(TOP_K, n, d // 2)

    return _final(yg, wgt_t.T, x1, g2, ln2_g[l].reshape(1, d), ln2_b[l].reshape(1, d), tm)
```

```python
import functools
import math

import jax
import jax.numpy as jnp
import numpy as np
from jax import lax
from jax.experimental import pallas as pl
from jax.experimental.pallas import tpu as pltpu
from jax.experimental.pallas import tpu_sc as plsc

F32 = jnp.float32
BF16 = jnp.bfloat16

D_MODEL = 1024
GRID_W = 64
N_Q_HEADS = 8
N_KV_HEADS = 2
HEAD_DIM = 128
Q_GROUP = N_Q_HEADS // N_KV_HEADS
ROPE_THETA = 10000.0
ROPE_AXIS_DIM = HEAD_DIM // 2
N_FOURIER_GROUPS = 4
FOURIER_GROUP_DIM = 128
FOURIER_DIM = N_FOURIER_GROUPS * FOURIER_GROUP_DIM
Q_DIM = N_Q_HEADS * HEAD_DIM
KV_DIM = N_KV_HEADS * HEAD_DIM
N_EXPERTS = 32
TOP_K = 4
D_EXPERT = 1024
SWIGLU_LIMIT = 7.0
SWIGLU_ALPHA = 1.702
N_MOD = 6
DEPTH = 1
DEEPNORM_ALPHA = (2 * DEPTH) ** 0.25
LN_EPS = 1e-5
ADA_EPS = 1e-6
QK_EPS = 1e-6
LOG2E = 1.4426950408889634

V7X_VMEM_BYTES = 64 * 1024 * 1024
VMEM_LIMIT = V7X_VMEM_BYTES - 8 * 1024 * 1024
LANES = 128

FFT_Q = 128
ROW_TILE = 512
ATTN_TQ = 256
ATTN_TK = 512
ONES_ROWS = 16
MOE_BLOCK = 512

SC_CORES = 2
SC_SUBCORES = 16
GATHER_CHUNK = 64


def _cparams(sem):
    return pltpu.CompilerParams(dimension_semantics=sem, vmem_limit_bytes=VMEM_LIMIT)


def _layer_norm(x, eps):
    mu = jnp.mean(x, axis=-1, keepdims=True)
    xc = x - mu
    var = jnp.mean(xc * xc, axis=-1, keepdims=True)
    return xc * lax.rsqrt(var + eps)


def _pack_bf16_pairs(x):
    n = x.shape[1] // 2
    bits = pltpu.bitcast(x.astype(BF16).astype(F32), jnp.uint32)
    word = (bits[:, :n] >> 16) | (bits[:, n:] & jnp.uint32(0xFFFF0000))
    return pltpu.bitcast(word, jnp.int32)


def _unpack_bf16_pairs(w):
    bits = pltpu.bitcast(w, jnp.uint32)
    lo = pltpu.bitcast(bits << 16, F32)
    hi = pltpu.bitcast(bits & jnp.uint32(0xFFFF0000), F32)
    return lo, hi


def _ada_kernel(c_ref, w_ref, b_ref, o_ref):
    c = c_ref[...]
    cond = c * jax.nn.sigmoid(c)
    o_ref[...] = jnp.dot(cond, w_ref[...], preferred_element_type=F32,
                         precision=lax.Precision.HIGHEST) + b_ref[...]


def _ada(c, w_ada, b_ada):
    bsz, d = c.shape
    n = w_ada.shape[1]
    tn = 1536
    return pl.pallas_call(
        _ada_kernel,
        grid=(n // tn,),
        in_specs=[pl.BlockSpec((bsz, d), lambda j: (0, 0)),
                  pl.BlockSpec((d, tn), lambda j: (0, j)),
                  pl.BlockSpec((1, tn), lambda j: (0, j))],
        out_specs=pl.BlockSpec((bsz, tn), lambda j: (0, j)),
        out_shape=jax.ShapeDtypeStruct((bsz, n), F32),
        compiler_params=_cparams(("arbitrary",)),
        name="ada",
    )(c, w_ada, b_ada.reshape(1, n))


def _rope(xh, cos, sin_a, sin_b):
    a = pltpu.roll(xh, 32, 1)
    b = pltpu.roll(xh, 96, 1)
    return xh * cos + b * sin_b + a * sin_a


def _inproj_kernel(x_ref, sc_ref, sh_ref, w_ref, dft_ref, qg_ref, kg_ref,
                   cos_ref, sa_ref, sb_ref,
                   fa_ref, fb_ref, q_ref, k_ref, v_ref, gf_ref, ga_ref):
    x = x_ref[...]
    u = _layer_norm(x, ADA_EPS) * (1.0 + sc_ref[...]) + sh_ref[...]
    ub = u.astype(BF16)
    cos, sa, sb = cos_ref[...], sa_ref[...], sb_ref[...]

    c0 = 0
    zf = jnp.dot(ub, w_ref[:, c0:c0 + FOURIER_DIM], preferred_element_type=F32).astype(BF16)
    for g in range(N_FOURIER_GROUPS):
        sl = slice(g * FOURIER_GROUP_DIM, (g + 1) * FOURIER_GROUP_DIM)
        ab = jnp.dot(zf[:, sl], dft_ref[...], preferred_element_type=F32)
        fa_ref[:, sl] = ab[:, :FOURIER_GROUP_DIM].astype(BF16)
        fb_ref[:, sl] = ab[:, FOURIER_GROUP_DIM:].astype(BF16)
    c0 += FOURIER_DIM

    def norm_rope(z, gain, scale):
        ms = jnp.mean(z * z, axis=-1, keepdims=True)
        zn = z * lax.rsqrt(ms + QK_EPS) * gain
        return (_rope(zn, cos, sa, sb) * scale).astype(BF16)

    q_scale = LOG2E / math.sqrt(HEAD_DIM)
    zq = jnp.dot(ub, w_ref[:, c0:c0 + Q_DIM], preferred_element_type=F32)
    for h in range(N_Q_HEADS):
        sl = slice(h * HEAD_DIM, (h + 1) * HEAD_DIM)
        q_ref[:, sl] = norm_rope(zq[:, sl], qg_ref[...], q_scale)
    c0 += Q_DIM

    zk = jnp.dot(ub, w_ref[:, c0:c0 + KV_DIM], preferred_element_type=F32)
    for h in range(N_KV_HEADS):
        sl = slice(h * HEAD_DIM, (h + 1) * HEAD_DIM)
        k_ref[:, sl] = norm_rope(zk[:, sl], kg_ref[...], 1.0)
    c0 += KV_DIM

    v_ref[...] = jnp.dot(ub, w_ref[:, c0:c0 + KV_DIM], preferred_element_type=F32).astype(BF16)
    c0 += KV_DIM

    zg = jnp.dot(ub, w_ref[:, c0:c0 + D_MODEL], preferred_element_type=F32)
    gf_ref[...] = jax.nn.sigmoid(zg).astype(BF16)
    c0 += D_MODEL
    zg = jnp.dot(ub, w_ref[:, c0:c0 + D_MODEL], preferred_element_type=F32)
    ga_ref[...] = jax.nn.sigmoid(zg).astype(BF16)


def _inproj(x, sc1, sh1, w_in, dft_c, qg, kg, cos, sin_a, sin_b, tm):
    bsz, s, d = x.shape
    ncol = w_in.shape[1]
    row = lambda w: pl.BlockSpec((None, tm, w), lambda b, i: (b, i, 0))
    mod = pl.BlockSpec((None, 1, d), lambda b, i: (b, 0, 0))
    full = lambda shp: pl.BlockSpec(shp, lambda b, i: (0,) * len(shp))
    tab = pl.BlockSpec((tm, HEAD_DIM), lambda b, i: (i, 0))
    outs = [FOURIER_DIM, FOURIER_DIM, Q_DIM, KV_DIM, KV_DIM, D_MODEL, D_MODEL]
    return pl.pallas_call(
        _inproj_kernel,
        grid=(bsz, s // tm),
        in_specs=[row(d), mod, mod, full((d, ncol)), full(dft_c.shape),
                  full((1, HEAD_DIM)), full((1, HEAD_DIM)), tab, tab, tab],
        out_specs=[row(w) for w in outs],
        out_shape=[jax.ShapeDtypeStruct((bsz, s, w), BF16) for w in outs],
        compiler_params=_cparams(("parallel", "parallel")),
        name="inproj",
    )(x, sc1, sh1, w_in, dft_c, qg, kg, cos, sin_a, sin_b)


def _fft1_kernel(a_ref, b_ref, m_ref, cw_ref, sw_ref, zr_ref, zi_ref):
    p = a_ref.shape[0]
    nb, _, c = zr_ref.shape
    ab = jnp.concatenate([a_ref[...], b_ref[...]], axis=0)
    t = jnp.dot(m_ref[...], ab, preferred_element_type=F32)
    for j in range(nb):
        tr = t[:p, j * c:(j + 1) * c]
        ti = t[p:, j * c:(j + 1) * c]
        cw, sw = cw_ref[j], sw_ref[j]
        zr_ref[j] = (tr * cw - ti * sw).astype(BF16)
        zi_ref[j] = (ti * cw + tr * sw).astype(BF16)


def _fft1(fa, fb, m1, cw, sw, p, q):
    bsz, s, c = fa.shape
    nb = 8
    a2 = fa.reshape(bsz, p, q * c)
    b2 = fb.reshape(bsz, p, q * c)
    inp = pl.BlockSpec((None, p, nb * c), lambda b, j: (b, 0, j))
    tw = pl.BlockSpec((nb, p, 1), lambda b, j: (j, 0, 0))
    out = pl.BlockSpec((None, nb, p, c), lambda b, j: (b, j, 0, 0))
    return pl.pallas_call(
        _fft1_kernel,
        grid=(bsz, q // nb),
        in_specs=[inp, inp, pl.BlockSpec(m1.shape, lambda b, j: (0, 0)), tw, tw],
        out_specs=[out, out],
        out_shape=[jax.ShapeDtypeStruct((bsz, q, p, c), BF16)] * 2,
        compiler_params=_cparams(("parallel", "parallel")),
        name="fft1",
    )(a2, b2, m1, cw, sw)


def _fft2_kernel(zr_ref, zi_ref, m_ref, y_ref):
    z = jnp.concatenate([zr_ref[...], zi_ref[...]], axis=0)
    y_ref[...] = jnp.dot(m_ref[...], z, preferred_element_type=F32).astype(BF16)


def _fft2(zr, zi, m2):
    bsz, q, p, c = zr.shape
    ncol = min(4096, p * c)
    z_r = zr.reshape(bsz, q, p * c)
    z_i = zi.reshape(bsz, q, p * c)
    blk = pl.BlockSpec((None, q, ncol), lambda b, j: (b, 0, j))
    y = pl.pallas_call(
        _fft2_kernel,
        grid=(bsz, p * c // ncol),
        in_specs=[blk, blk, pl.BlockSpec(m2.shape, lambda b, j: (0, 0))],
        out_specs=blk,
        out_shape=jax.ShapeDtypeStruct((bsz, q, p * c), BF16),
        compiler_params=_cparams(("parallel", "parallel")),
        name="fft2",
    )(z_r, z_i, m2)
    return y.reshape(bsz, q * p, c)


def _dft_tables(s):
    q = FFT_Q
    p = s // q
    two_pi = 2.0 * math.pi

    def cs(n_rows, n_cols, n):
        r = jnp.arange(n_rows, dtype=jnp.int32)[:, None]
        c = jnp.arange(n_cols, dtype=jnp.int32)[None, :]
        ang = ((r * c) % n).astype(F32) * (two_pi / n)
        return jnp.cos(ang), jnp.sin(ang)

    cc, sc = cs(FOURIER_GROUP_DIM, FOURIER_GROUP_DIM, FOURIER_GROUP_DIM)
    dft_c = jnp.concatenate([cc, sc], axis=1).astype(BF16)
    cp, sp = cs(p, p, p)
    m1 = jnp.concatenate([jnp.concatenate([cp, -sp], axis=1),
                          jnp.concatenate([sp, cp], axis=1)], axis=0).astype(BF16)
    cq, sq = cs(q, q, q)
    scale = 1.0 / math.sqrt(s * FOURIER_GROUP_DIM)
    m2 = (jnp.concatenate([cq, -sq], axis=1) * scale).astype(BF16)
    cw, sw = cs(q, p, s)
    return dft_c, m1, m2, cw[:, :, None], sw[:, :, None], p, q


def _rope_tables(s):
    rows = s // GRID_W
    row_ids = jnp.repeat(jnp.arange(rows, dtype=F32), GRID_W)
    col_ids = jnp.tile(jnp.arange(GRID_W, dtype=F32), rows)
    freqs = ROPE_THETA ** (-jnp.arange(0, ROPE_AXIS_DIM, 2, dtype=F32) / ROPE_AXIS_DIM)
    ang_r = row_ids[:, None] * freqs
    ang_c = col_ids[:, None] * freqs
    cr, sr, cc, sc = jnp.cos(ang_r), jnp.sin(ang_r), jnp.cos(ang_c), jnp.sin(ang_c)
    z = jnp.zeros_like(sr)
    cos = jnp.concatenate([cr, cr, cc, cc], axis=1)
    sin_b = jnp.concatenate([-sr, z, -sc, z], axis=1)
    sin_a = jnp.concatenate([z, sr, z, sc], axis=1)
    return cos, sin_a, sin_b


def _attn_kernel(q_ref, k_ref, vt_ref, o_ref, m_ref, acc_ref, al0, al1, s0, s1, p0, p1):
    al_ref, s_ref, p_ref = (al0, al1), (s0, s1), (p0, p1)
    nchunk = vt_ref.shape[0]
    tk = vt_ref.shape[2]
    nt = (((1,), (1,)), ((), ()))

    def scores_into(c, slot):
        off = pl.multiple_of(c * tk, tk)
        kc = k_ref[pl.ds(off, tk), :]
        for g in range(Q_GROUP):
            s_ref[slot][g] = lax.dot_general(kc, q_ref[:, g * HEAD_DIM:(g + 1) * HEAD_DIM], nt,
                                             preferred_element_type=F32)

    def softmax(slot):
        for g in range(Q_GROUP):
            sc = s_ref[slot][g]
            m_old = m_ref[g]
            m_new = jnp.maximum(m_old, jnp.max(sc, axis=0, keepdims=True))
            al_ref[slot][g] = jnp.exp2(m_old - m_new)
            m_ref[g] = m_new
            p_ref[slot][g] = jnp.exp2((sc - m_new).astype(BF16))

    def pv_update(c, slot):
        vc = vt_ref[c]
        for g in range(Q_GROUP):
            acc_ref[g] = al_ref[slot][g] * acc_ref[g] + jnp.dot(vc, p_ref[slot][g],
                                                                preferred_element_type=F32)

    def step(c, slot, prefetch_scores=True):
        if prefetch_scores:
            scores_into(c + 1, 1 - slot)
        pv_update(c - 1, 1 - slot)
        softmax(slot)

    m_ref[...] = jnp.full(m_ref.shape, -jnp.inf, F32)
    acc_ref[...] = jnp.zeros(acc_ref.shape, F32)
    scores_into(0, 0)
    softmax(0)
    scores_into(1, 1)

    def pair(i, carry):
        step(2 * i + 1, 1)
        step(2 * i + 2, 0)
        return carry

    lax.fori_loop(0, (nchunk - 2) // 2, pair, 0)
    last = nchunk - 1
    step(last, 1, prefetch_scores=False)
    pv_update(last, 1)
    for g in range(Q_GROUP):
        acc = acc_ref[g]
        out_t = acc[:HEAD_DIM] / acc[HEAD_DIM:HEAD_DIM + 1]
        o_ref[:, g * HEAD_DIM:(g + 1) * HEAD_DIM] = out_t.T.astype(BF16)


def _attention(q, k, v, tq, tk):
    bsz, s, _ = q.shape
    assert (s // tk) % 2 == 0
    gw = Q_GROUP * HEAD_DIM
    rows = HEAD_DIM + ONES_ROWS
    vt = v.reshape(bsz, s // tk, tk, N_KV_HEADS, HEAD_DIM).transpose(0, 3, 1, 4, 2)
    vt = jnp.concatenate([vt, jnp.ones((bsz, N_KV_HEADS, s // tk, ONES_ROWS, tk), BF16)], axis=3)
    qspec = pl.BlockSpec((None, tq, gw), lambda b, h, i: (b, i, h))
    kspec = pl.BlockSpec((None, s, HEAD_DIM), lambda b, h, i: (b, 0, h))
    vspec = pl.BlockSpec((None, None, s // tk, rows, tk), lambda b, h, i: (b, h, 0, 0, 0))
    return pl.pallas_call(
        _attn_kernel,
        grid=(bsz, N_KV_HEADS, s // tq),
        in_specs=[qspec, kspec, vspec],
        out_specs=qspec,
        out_shape=jax.ShapeDtypeStruct((bsz, s, Q_DIM), BF16),
        scratch_shapes=[pltpu.VMEM((Q_GROUP, 1, tq), F32), pltpu.VMEM((Q_GROUP, rows, tq), F32)]
        + [pltpu.VMEM((Q_GROUP, 1, tq), F32)] * 2 + [pltpu.VMEM((Q_GROUP, tk, tq), F32)] * 2
        + [pltpu.VMEM((Q_GROUP, tk, tq), BF16)] * 2,
        compiler_params=_cparams(("parallel", "parallel", "parallel")),
        name="attn",
    )(q, k, vt)


def _mix_kernel(yf_ref, at_ref, gf_ref, ga_ref, x_ref, g1_ref, sc2_ref, sh2_ref,
                wf_ref, wo_ref, wout_ref, l1g_ref, l1b_ref, wr_ref, br_ref, tri_ref,
                x1_ref, u2_ref, idx_ref, wgt_ref, rank_ref, cnt_ref, cnt_acc):
    first = jnp.logical_and(pl.program_id(0) == 0, pl.program_id(1) == 0)

    @pl.when(first)
    def _():
        cnt_acc[...] = jnp.zeros_like(cnt_acc)

    yf = jnp.dot(yf_ref[...], wf_ref[...], preferred_element_type=F32)
    ya = jnp.dot(at_ref[...], wo_ref[...], preferred_element_type=F32)
    m = gf_ref[...].astype(F32) * yf + ga_ref[...].astype(F32) * ya
    h = jnp.dot(m.astype(BF16), wout_ref[...], preferred_element_type=F32)
    r = DEEPNORM_ALPHA * x_ref[...] + g1_ref[...] * h
    x1 = _layer_norm(r, LN_EPS) * l1g_ref[...] + l1b_ref[...]
    x1_ref[...] = x1
    u2 = _layer_norm(x1, ADA_EPS) * (1.0 + sc2_ref[...]) + sh2_ref[...]
    u2_ref[...] = _pack_bf16_pairs(u2)

    nt = (((1,), (1,)), ((), ()))
    u_hi = u2.astype(BF16)
    u_lo = (u2 - u_hi.astype(F32)).astype(BF16)
    w = wr_ref[...]
    w_hi = w.astype(BF16)
    w_lo = (w - w_hi.astype(F32)).astype(BF16)
    logits = (lax.dot_general(w_hi, u_hi, nt, preferred_element_type=F32)
              + lax.dot_general(w_lo, u_hi, nt, preferred_element_type=F32)
              + lax.dot_general(w_hi, u_lo, nt, preferred_element_type=F32)) + br_ref[...]

    ne, tm = logits.shape
    eid = lax.broadcasted_iota(jnp.int32, (ne, tm), 0)
    work = logits
    vals, idxs, hots = [], [], []
    for _ in range(TOP_K):
        mx = jnp.max(work, axis=0, keepdims=True)
        ix = jnp.min(jnp.where(work == mx, eid, ne), axis=0, keepdims=True)
        hot = eid == ix
        work = jnp.where(hot, -jnp.inf, work)
        vals.append(mx)
        idxs.append(ix)
        hots.append(hot)
    ex = [jnp.exp(v - vals[0]) for v in vals]
    den = ex[0] + ex[1] + ex[2] + ex[3]
    wgt_ref[...] = jnp.concatenate([e / den for e in ex], axis=0)
    idx_ref[...] = jnp.concatenate(idxs, axis=0)

    sel = hots[0] | hots[1] | hots[2] | hots[3]
    mask = jnp.where(sel, 1.0, 0.0)
    prefix = jnp.dot(mask.astype(BF16), tri_ref[...], preferred_element_type=F32)
    pos = prefix + cnt_acc[...]
    ranks = [jnp.sum(jnp.where(hot, pos, 0.0), axis=0, keepdims=True) for hot in hots]
    rank_ref[...] = jnp.concatenate(ranks, axis=0).astype(jnp.int32)
    cnt_acc[...] += jnp.sum(mask, axis=1, keepdims=True)
    cnt_ref[...] = cnt_acc[...].astype(jnp.int32)


def _mix(yfm, attn, gf, ga, x, g1, sc2, sh2, wf, wo, wout, l1g, l1b, wr_t, br, tm):
    bsz, s, d = x.shape
    n = bsz * s
    nt = s // tm
    tri = (jnp.arange(tm)[:, None] < jnp.arange(tm)[None, :]).astype(BF16)
    row = lambda w: pl.BlockSpec((None, tm, w), lambda b, i: (b, i, 0))
    mod = pl.BlockSpec((None, 1, d), lambda b, i: (b, 0, 0))
    full = lambda shp: pl.BlockSpec(shp, lambda b, i: (0,) * len(shp))
    tok = pl.BlockSpec((TOP_K, tm), lambda b, i: (0, b * nt + i))
    return pl.pallas_call(
        _mix_kernel,
        grid=(bsz, nt),
        in_specs=[row(FOURIER_DIM), row(Q_DIM), row(d), row(d), row(d), mod, mod, mod,
                  full(wf.shape), full(wo.shape), full(wout.shape), full((1, d)), full((1, d)),
                  full(wr_t.shape), full((N_EXPERTS, 1)), full((tm, tm))],
        out_specs=[row(d), row(d // 2), tok, tok, tok, full((N_EXPERTS, 1))],
        out_shape=[jax.ShapeDtypeStruct((bsz, s, d), F32),
                   jax.ShapeDtypeStruct((bsz, s, d // 2), jnp.int32),
                   jax.ShapeDtypeStruct((TOP_K, n), jnp.int32),
                   jax.ShapeDtypeStruct((TOP_K, n), F32),
                   jax.ShapeDtypeStruct((TOP_K, n), jnp.int32),
                   jax.ShapeDtypeStruct((N_EXPERTS, 1), jnp.int32)],
        scratch_shapes=[pltpu.VMEM((N_EXPERTS, 1), F32)],
        compiler_params=_cparams(("arbitrary", "arbitrary")),
        name="mix",
    )(yfm, attn, gf, ga, x, g1, sc2, sh2, wf, wo, wout, l1g, l1b, wr_t, br, tri)


def _expert_kernel(be_ref, nv_ref, new_ref, xs_ref, wgu_ref, bgu_ref, wd_ref, bd_ref, ys_ref,
                   wgu_bf, wd_bf):
    del be_ref
    i = pl.program_id(0)
    nvalid = nv_ref[i]

    @pl.when(new_ref[i] == 1)
    def _():
        wgu_bf[...] = wgu_ref[...].astype(BF16)
        wd_bf[...] = wd_ref[...].astype(BF16)

    @pl.when(nvalid > 0)
    def _():
        xw = xs_ref[...]
        row = lax.broadcasted_iota(jnp.int32, xw.shape, 0)
        lo, hi = _unpack_bf16_pairs(jnp.where(row < nvalid, xw, 0))
        xb = jnp.concatenate([lo.astype(BF16), hi.astype(BF16)], axis=1)
        h = jnp.dot(xb, wgu_bf[...], preferred_element_type=F32) + bgu_ref[...]
        gate = jnp.minimum(h[:, :D_EXPERT], SWIGLU_LIMIT)
        up = jnp.clip(h[:, D_EXPERT:], -SWIGLU_LIMIT, SWIGLU_LIMIT)
        act = (up + 1.0) * gate * jax.nn.sigmoid(SWIGLU_ALPHA * gate)
        y = jnp.dot(act.astype(BF16), wd_bf[...], preferred_element_type=F32) + bd_ref[...]
        ys_ref[...] = _pack_bf16_pairs(y)

    @pl.when(nvalid == 0)
    def _():
        ys_ref[...] = jnp.zeros_like(ys_ref)


def _experts(block_e, block_nvalid, block_new, xs, wgu, bgu, wd, bd, tmb):
    n_pad, dh = xs.shape
    d = 2 * dh
    grid_spec = pltpu.PrefetchScalarGridSpec(
        num_scalar_prefetch=3,
        grid=(n_pad // tmb,),
        in_specs=[pl.BlockSpec((tmb, dh), lambda i, be, nv, nw: (i, 0)),
                  pl.BlockSpec((None, d, 2 * D_EXPERT), lambda i, be, nv, nw: (be[i], 0, 0)),
                  pl.BlockSpec((None, 1, 2 * D_EXPERT), lambda i, be, nv, nw: (be[i], 0, 0)),
                  pl.BlockSpec((None, D_EXPERT, d), lambda i, be, nv, nw: (be[i], 0, 0)),
                  pl.BlockSpec((None, 1, d), lambda i, be, nv, nw: (be[i], 0, 0))],
        out_specs=pl.BlockSpec((tmb, dh), lambda i, be, nv, nw: (i, 0)),
        scratch_shapes=[pltpu.VMEM((d, 2 * D_EXPERT), BF16), pltpu.VMEM((D_EXPERT, d), BF16)],
    )
    return pl.pallas_call(
        _expert_kernel,
        grid_spec=grid_spec,
        out_shape=jax.ShapeDtypeStruct((n_pad, dh), jnp.int32),
        compiler_params=_cparams(("arbitrary",)),
        name="experts",
    )(block_e, block_nvalid, block_new, xs, wgu, bgu, wd, bd)


def _final_kernel(yg_ref, w_ref, x1_ref, g2_ref, lg_ref, lb_ref, o_ref):
    w = w_ref[...]
    acc_lo = None
    for j in range(TOP_K):
        lo, hi = _unpack_bf16_pairs(yg_ref[j])
        wj = w[:, j:j + 1]
        acc_lo = lo * wj if acc_lo is None else acc_lo + lo * wj
        acc_hi = hi * wj if j == 0 else acc_hi + hi * wj
    h = jnp.concatenate([acc_lo, acc_hi], axis=1)
    r = DEEPNORM_ALPHA * x1_ref[...] + g2_ref[...] * h
    o_ref[...] = _layer_norm(r, LN_EPS) * lg_ref[...] + lb_ref[...]


def _final(yg, w_tok, x1, g2, lg, lb, tm):
    bsz, s, d = x1.shape
    nt = s // tm
    row = pl.BlockSpec((None, tm, d), lambda b, i: (b, i, 0))
    full = pl.BlockSpec((1, d), lambda b, i: (0, 0))
    return pl.pallas_call(
        _final_kernel,
        grid=(bsz, nt),
        in_specs=[pl.BlockSpec((TOP_K, tm, d // 2), lambda b, i: (0, b * nt + i, 0)),
                  pl.BlockSpec((tm, TOP_K), lambda b, i: (b * nt + i, 0)),
                  row, pl.BlockSpec((None, 1, d), lambda b, i: (b, 0, 0)), full, full],
        out_specs=row,
        out_shape=jax.ShapeDtypeStruct((bsz, s, d), F32),
        compiler_params=_cparams(("parallel", "parallel")),
        name="final",
    )(yg, w_tok, x1, g2, lg, lb)


def _scatter_rows(rows, dest, m_out):
    n, w = rows.shape
    nj = dest.shape[0]
    ch = GATHER_CHUNK
    nw = SC_CORES * SC_SUBCORES
    nch = n // (nw * ch)
    assert n == nw * nch * ch and nch % 2 == 0
    per_w = nch * ch
    mesh = plsc.VectorSubcoreMesh(core_axis_name="c", subcore_axis_name="s",
                                  num_cores=SC_CORES, num_subcores=SC_SUBCORES)

    @functools.partial(
        pl.kernel, mesh=mesh,
        out_type=jax.ShapeDtypeStruct((m_out, w), rows.dtype),
        scratch_types=[pltpu.VMEM((nj, nch, ch), jnp.int32), pltpu.VMEM((2, ch, w), rows.dtype),
                       pltpu.SemaphoreType.DMA((2,)), pltpu.SemaphoreType.DMA((2,))],
    )
    def scatter_kernel(rows_hbm, dest_hbm, out_hbm, idx_v, rows_v, rsem, ssem):
        wid = lax.axis_index("s") * SC_CORES + lax.axis_index("c")
        base = wid * per_w
        for j in range(nj):
            pltpu.sync_copy(dest_hbm.at[j, wid], idx_v.at[j])

        def read(i, b):
            off = pl.multiple_of(base + i * ch, ch)
            return pltpu.make_async_copy(rows_hbm.at[pl.ds(off, ch)], rows_v.at[b], rsem.at[b])

        def scatter(i, b, j):
            return pltpu.make_async_copy(rows_v.at[b], out_hbm.at[idx_v.at[j, i]], ssem.at[b])

        read(0, 0).start()

        @pl.loop(0, nch, step=2)
        def _(i0):
            for b in range(2):
                i = i0 + b
                read(i, b).wait()

                @pl.when(i + 1 < nch)
                def _():
                    @pl.when(i >= 1)
                    def _():
                        for j in range(nj):
                            scatter(i - 1, 1 - b, j).wait()
                    read(i + 1, 1 - b).start()

                for j in range(nj):
                    scatter(i, b, j).start()

        for b in range(2):
            for j in range(nj):
                scatter(nch - 2 + b, b, j).wait()

    return scatter_kernel(rows, dest.reshape(nj, nw, nch, ch))


def _gather_rows(table, idx):
    m = idx.shape[0]
    w = table.shape[1]
    ch = GATHER_CHUNK
    nw = SC_CORES * SC_SUBCORES
    nch = m // (nw * ch)
    assert m == nw * nch * ch and nch % 2 == 0
    per_w = nch * ch
    mesh = plsc.VectorSubcoreMesh(core_axis_name="c", subcore_axis_name="s",
                                  num_cores=SC_CORES, num_subcores=SC_SUBCORES)

    @functools.partial(
        pl.kernel, mesh=mesh,
        out_type=jax.ShapeDtypeStruct((m, w), table.dtype),
        scratch_types=[pltpu.VMEM((nch, ch), jnp.int32), pltpu.VMEM((2, ch, w), table.dtype),
                       pltpu.SemaphoreType.DMA((2,)), pltpu.SemaphoreType.DMA((2,))],
    )
    def gather_kernel(table_hbm, idx_hbm, out_hbm, idx_v, rows_v, gsem, wsem):
        wid = lax.axis_index("s") * SC_CORES + lax.axis_index("c")
        base = wid * per_w
        pltpu.sync_copy(idx_hbm.at[wid], idx_v)

        def gather(j, b):
            return pltpu.make_async_copy(table_hbm.at[idx_v.at[j]], rows_v.at[b], gsem.at[b])

        def write(j, b):
            off = pl.multiple_of(base + j * ch, ch)
            return pltpu.make_async_copy(rows_v.at[b], out_hbm.at[pl.ds(off, ch)], wsem.at[b])

        gather(0, 0).start()

        @pl.loop(0, nch, step=2)
        def _(i):
            for b in range(2):
                j = i + b
                gather(j, b).wait()

                @pl.when(j + 1 < nch)
                def _():
                    @pl.when(j >= 1)
                    def _():
                        write(j - 1, 1 - b).wait()
                    gather(j + 1, 1 - b).start()

                write(j, b).start()

        write(nch - 2, 0).wait()
        write(nch - 1, 1).wait()

    return gather_kernel(table, idx.reshape(nw, nch, ch))


def kernel(x, c, w_ada, b_ada, w_in, q_norm_g, k_norm_g, w_fourier, w_attn_o, w_out, ln1_g, ln1_b,
           w_router, b_router, w_gate_up, b_gate_up, w_down, b_down, ln2_g, ln2_b):
    bsz, s, d = x.shape
    n = bsz * s
    tm = min(ROW_TILE, s)
    l = 0

    mod = _ada(c, w_ada[l], b_ada[l])
    sh1, sc1, g1, sh2, sc2, g2 = [m[:, None, :] for m in jnp.split(mod, N_MOD, axis=-1)]

    dft_c, m1, m2, cw, sw, p, q = _dft_tables(s)
    cos, sin_a, sin_b = _rope_tables(s)

    fa, fb, qh, kh, vh, gf, ga = _inproj(
        x, sc1, sh1, w_in[l].astype(BF16), dft_c,
        q_norm_g[l].reshape(1, HEAD_DIM), k_norm_g[l].reshape(1, HEAD_DIM), cos, sin_a, sin_b, tm)

    zr, zi = _fft1(fa, fb, m1, cw, sw, p, q)
    yfm = _fft2(zr, zi, m2)

    attn = _attention(qh, kh, vh, min(ATTN_TQ, s), min(ATTN_TK, s))

    x1, u2p, idx_t, wgt_t, rank_t, counts = _mix(
        yfm, attn, gf, ga, x, g1, sc2, sh2,
        w_fourier[l].astype(BF16), w_attn_o[l].astype(BF16), w_out[l].astype(BF16),
        ln1_g[l].reshape(1, d), ln1_b[l].reshape(1, d),
        w_router[l].T, b_router[l].reshape(N_EXPERTS, 1), tm)

    tmb = MOE_BLOCK
    counts = counts.reshape(N_EXPERTS)
    pcounts = ((counts + tmb - 1) // tmb) * tmb
    pends = jnp.cumsum(pcounts)
    pstarts = pends - pcounts
    n_pad = n * TOP_K + N_EXPERTS * tmb
    nblk = n_pad // tmb
    eids = jnp.arange(N_EXPERTS, dtype=jnp.int32)
    start_of = jnp.sum(jnp.where(idx_t[None] == eids[:, None, None], pstarts[:, None, None], 0), axis=0)
    dest_t = start_of + rank_t
    block_starts = jnp.arange(nblk, dtype=jnp.int32) * tmb
    block_e = jnp.sum((block_starts[:, None] >= pends[None, :]).astype(jnp.int32), axis=1)
    block_e = jnp.minimum(block_e, N_EXPERTS - 1)
    block_nvalid = jnp.clip(pstarts[block_e] + counts[block_e] - block_starts, 0, tmb)
    block_new = jnp.concatenate([jnp.ones((1,), jnp.int32),
                                 (block_e[1:] != block_e[:-1]).astype(jnp.int32)])

    xs = _scatter_rows(u2p.reshape(n, d // 2), dest_t, n_pad)
    ys = _experts(block_e, block_nvalid, block_new, xs, w_gate_up[l], b_gate_up[l][:, None, :],
                  w_down[l], b_down[l][:, None, :], tmb)
    yg = _gather_rows(ys, dest_t.reshape(-1)).reshape(TOP_K, n, d // 2)

    return _final(yg, wgt_t.T, x1, g2, ln2_g[l].reshape(1, d), ln2_b[l].reshape(1, d), tm)
```

```python
import functools
import math

import jax
import jax.numpy as jnp
import numpy as np
from jax import lax
from jax.experimental import pallas as pl
from jax.experimental.pallas import tpu as pltpu
from jax.experimental.pallas import tpu_sc as plsc

F32 = jnp.float32
BF16 = jnp.bfloat16

D_MODEL = 1024
GRID_W = 64
N_Q_HEADS = 8
N_KV_HEADS = 2
HEAD_DIM = 128
Q_GROUP = N_Q_HEADS // N_KV_HEADS
ROPE_THETA = 10000.0
ROPE_AXIS_DIM = HEAD_DIM // 2
N_FOURIER_GROUPS = 4
FOURIER_GROUP_DIM = 128
FOURIER_DIM = N_FOURIER_GROUPS * FOURIER_GROUP_DIM
Q_DIM = N_Q_HEADS * HEAD_DIM
KV_DIM = N_KV_HEADS * HEAD_DIM
N_EXPERTS = 32
TOP_K = 4
D_EXPERT = 1024
SWIGLU_LIMIT = 7.0
SWIGLU_ALPHA = 1.702
N_MOD = 6
DEPTH = 1
DEEPNORM_ALPHA = (2 * DEPTH) ** 0.25
LN_EPS = 1e-5
ADA_EPS = 1e-6
QK_EPS = 1e-6
LOG2E = 1.4426950408889634

V7X_VMEM_BYTES = 64 * 1024 * 1024
VMEM_LIMIT = V7X_VMEM_BYTES - 8 * 1024 * 1024
LANES = 128
BF16_SUBLANES = 16

FFT_Q = 128
ROW_TILE = 512
ATTN_TQ = 256
ATTN_TK = 512
ONES_ROWS = 16
MOE_BLOCK = 512

SC_CORES = 2
SC_SUBCORES = 16
GATHER_CHUNK = 64


def _cparams(sem):
    return pltpu.CompilerParams(dimension_semantics=sem, vmem_limit_bytes=VMEM_LIMIT)


def _layer_norm(x, eps):
    mu = jnp.mean(x, axis=-1, keepdims=True)
    xc = x - mu
    var = jnp.mean(xc * xc, axis=-1, keepdims=True)
    return xc * lax.rsqrt(var + eps)


def _pack_bf16_pairs(x):
    n = x.shape[1] // 2
    bits = pltpu.bitcast(x.astype(BF16).astype(F32), jnp.uint32)
    word = (bits[:, :n] >> 16) | (bits[:, n:] & jnp.uint32(0xFFFF0000))
    return pltpu.bitcast(word, jnp.int32)


def _unpack_bf16_pairs(w):
    bits = pltpu.bitcast(w, jnp.uint32)
    lo = pltpu.bitcast(bits << 16, F32)
    hi = pltpu.bitcast(bits & jnp.uint32(0xFFFF0000), F32)
    return lo, hi


def _ada_kernel(c_ref, w_ref, b_ref, o_ref):
    c = c_ref[...]
    cond = c * jax.nn.sigmoid(c)
    o_ref[...] = jnp.dot(cond, w_ref[...], preferred_element_type=F32,
                         precision=lax.Precision.HIGHEST) + b_ref[...]


def _ada(c, w_ada, b_ada):
    bsz, d = c.shape
    n = w_ada.shape[1]
    tn = 1536
    return pl.pallas_call(
        _ada_kernel,
        grid=(n // tn,),
        in_specs=[pl.BlockSpec((bsz, d), lambda j: (0, 0)),
                  pl.BlockSpec((d, tn), lambda j: (0, j)),
                  pl.BlockSpec((1, tn), lambda j: (0, j))],
        out_specs=pl.BlockSpec((bsz, tn), lambda j: (0, j)),
        out_shape=jax.ShapeDtypeStruct((bsz, n), F32),
        compiler_params=_cparams(("arbitrary",)),
        name="ada",
    )(c, w_ada, b_ada.reshape(1, n))


def _rope(xh, cos, sin_a, sin_b):
    a = pltpu.roll(xh, 32, 1)
    b = pltpu.roll(xh, 96, 1)
    return xh * cos + b * sin_b + a * sin_a


def _inproj_kernel(x_ref, sc_ref, sh_ref, w_ref, dft_ref, qg_ref, kg_ref,
                   cos_ref, sa_ref, sb_ref,
                   fa_ref, fb_ref, q_ref, k_ref, v_ref, gf_ref, ga_ref):
    x = x_ref[...]
    u = _layer_norm(x, ADA_EPS) * (1.0 + sc_ref[...]) + sh_ref[...]
    ub = u.astype(BF16)
    cos, sa, sb = cos_ref[...], sa_ref[...], sb_ref[...]

    c0 = 0
    zf = jnp.dot(ub, w_ref[:, c0:c0 + FOURIER_DIM], preferred_element_type=F32).astype(BF16)
    for g in range(N_FOURIER_GROUPS):
        sl = slice(g * FOURIER_GROUP_DIM, (g + 1) * FOURIER_GROUP_DIM)
        ab = jnp.dot(zf[:, sl], dft_ref[...], preferred_element_type=F32)
        fa_ref[:, sl] = ab[:, :FOURIER_GROUP_DIM].astype(BF16)
        fb_ref[:, sl] = ab[:, FOURIER_GROUP_DIM:].astype(BF16)
    c0 += FOURIER_DIM

    def norm_rope(z, gain, scale):
        ms = jnp.mean(z * z, axis=-1, keepdims=True)
        zn = z * lax.rsqrt(ms + QK_EPS) * gain
        return (_rope(zn, cos, sa, sb) * scale).astype(BF16)

    q_scale = LOG2E / math.sqrt(HEAD_DIM)
    zq = jnp.dot(ub, w_ref[:, c0:c0 + Q_DIM], preferred_element_type=F32)
    for h in range(N_Q_HEADS):
        sl = slice(h * HEAD_DIM, (h + 1) * HEAD_DIM)
        q_ref[:, sl] = norm_rope(zq[:, sl], qg_ref[...], q_scale)
    c0 += Q_DIM

    zk = jnp.dot(ub, w_ref[:, c0:c0 + KV_DIM], preferred_element_type=F32)
    for h in range(N_KV_HEADS):
        sl = slice(h * HEAD_DIM, (h + 1) * HEAD_DIM)
        k_ref[:, sl] = norm_rope(zk[:, sl], kg_ref[...], 1.0)
    c0 += KV_DIM

    v_ref[...] = jnp.dot(ub, w_ref[:, c0:c0 + KV_DIM], preferred_element_type=F32).astype(BF16)
    c0 += KV_DIM

    zg = jnp.dot(ub, w_ref[:, c0:c0 + D_MODEL], preferred_element_type=F32)
    gf_ref[...] = jax.nn.sigmoid(zg).astype(BF16)
    c0 += D_MODEL
    zg = jnp.dot(ub, w_ref[:, c0:c0 + D_MODEL], preferred_element_type=F32)
    ga_ref[...] = jax.nn.sigmoid(zg).astype(BF16)


def _inproj(x, sc1, sh1, w_in, dft_c, qg, kg, cos, sin_a, sin_b, tm):
    bsz, s, d = x.shape
    ncol = w_in.shape[1]
    row = lambda w: pl.BlockSpec((None, tm, w), lambda b, i: (b, i, 0))
    mod = pl.BlockSpec((None, 1, d), lambda b, i: (b, 0, 0))
    full = lambda shp: pl.BlockSpec(shp, lambda b, i: (0,) * len(shp))
    tab = pl.BlockSpec((tm, HEAD_DIM), lambda b, i: (i, 0))
    outs = [FOURIER_DIM, FOURIER_DIM, Q_DIM, KV_DIM, KV_DIM, D_MODEL, D_MODEL]
    return pl.pallas_call(
        _inproj_kernel,
        grid=(bsz, s // tm),
        in_specs=[row(d), mod, mod, full((d, ncol)), full(dft_c.shape),
                  full((1, HEAD_DIM)), full((1, HEAD_DIM)), tab, tab, tab],
        out_specs=[row(w) for w in outs],
        out_shape=[jax.ShapeDtypeStruct((bsz, s, w), BF16) for w in outs],
        compiler_params=_cparams(("parallel", "parallel")),
        name="inproj",
    )(x, sc1, sh1, w_in, dft_c, qg, kg, cos, sin_a, sin_b)


def _fft1_kernel(a_ref, b_ref, m_ref, cw_ref, sw_ref, zr_ref, zi_ref, a32, b32):
    p, nb, _ = a_ref.shape
    a32[...] = a_ref[...].astype(F32)
    b32[...] = b_ref[...].astype(F32)
    for j in range(nb):
        ab = jnp.concatenate([a32[:, j, :], b32[:, j, :]], axis=0).astype(BF16)
        t = jnp.dot(m_ref[...], ab, preferred_element_type=F32)
        tr, ti = t[:p], t[p:]
        cw, sw = cw_ref[j], sw_ref[j]
        zr_ref[j] = (tr * cw - ti * sw).astype(BF16)
        zi_ref[j] = (ti * cw + tr * sw).astype(BF16)


def _fft1(fa, fb, m1, cw, sw, p, q):
    bsz, s, c = fa.shape
    nb = BF16_SUBLANES
    inp = pl.BlockSpec((None, p, nb, c), lambda b, j: (b, 0, j, 0))
    tw = pl.BlockSpec((nb, p, 1), lambda b, j: (j, 0, 0))
    out = pl.BlockSpec((None, nb, p, c), lambda b, j: (b, j, 0, 0))
    return pl.pallas_call(
        _fft1_kernel,
        grid=(bsz, q // nb),
        in_specs=[inp, inp, pl.BlockSpec(m1.shape, lambda b, j: (0, 0)), tw, tw],
        out_specs=[out, out],
        out_shape=[jax.ShapeDtypeStruct((bsz, q, p, c), BF16)] * 2,
        scratch_shapes=[pltpu.VMEM((p, nb, c), F32)] * 2,
        compiler_params=_cparams(("parallel", "parallel")),
        name="fft1",
    )(fa.reshape(bsz, p, q, c), fb.reshape(bsz, p, q, c), m1, cw, sw)


def _fft2_kernel(zr_ref, zi_ref, m_ref, y_ref, zr32, zi32, y32):
    zr32[...] = zr_ref[...].astype(F32)
    zi32[...] = zi_ref[...].astype(F32)
    for j in range(zr_ref.shape[1]):
        z = jnp.concatenate([zr32[:, j, :], zi32[:, j, :]], axis=0).astype(BF16)
        y32[:, j, :] = jnp.dot(m_ref[...], z, preferred_element_type=F32)
    y_ref[...] = y32[...].astype(BF16)


def _fft2(zr, zi, m2):
    bsz, q, p, c = zr.shape
    pc = BF16_SUBLANES
    blk = pl.BlockSpec((None, q, pc, c), lambda b, j: (b, 0, j, 0))
    y = pl.pallas_call(
        _fft2_kernel,
        grid=(bsz, p // pc),
        in_specs=[blk, blk, pl.BlockSpec(m2.shape, lambda b, j: (0, 0))],
        out_specs=blk,
        out_shape=jax.ShapeDtypeStruct((bsz, q, p, c), BF16),
        scratch_shapes=[pltpu.VMEM((q, pc, c), F32)] * 3,
        compiler_params=_cparams(("parallel", "parallel")),
        name="fft2",
    )(zr, zi, m2)
    return y.reshape(bsz, q * p, c)


def _dft_tables(s):
    q = FFT_Q
    p = s // q
    two_pi = 2.0 * math.pi

    def cs(n_rows, n_cols, n):
        r = jnp.arange(n_rows, dtype=jnp.int32)[:, None]
        c = jnp.arange(n_cols, dtype=jnp.int32)[None, :]
        ang = ((r * c) % n).astype(F32) * (two_pi / n)
        return jnp.cos(ang), jnp.sin(ang)

    cc, sc = cs(FOURIER_GROUP_DIM, FOURIER_GROUP_DIM, FOURIER_GROUP_DIM)
    dft_c = jnp.concatenate([cc, sc], axis=1).astype(BF16)
    cp, sp = cs(p, p, p)
    m1 = jnp.concatenate([jnp.concatenate([cp, -sp], axis=1),
                          jnp.concatenate([sp, cp], axis=1)], axis=0).astype(BF16)
    cq, sq = cs(q, q, q)
    scale = 1.0 / math.sqrt(s * FOURIER_GROUP_DIM)
    m2 = (jnp.concatenate([cq, -sq], axis=1) * scale).astype(BF16)
    cw, sw = cs(q, p, s)
    return dft_c, m1, m2, cw[:, :, None], sw[:, :, None], p, q


def _rope_tables(s):
    rows = s // GRID_W
    row_ids = jnp.repeat(jnp.arange(rows, dtype=F32), GRID_W)
    col_ids = jnp.tile(jnp.arange(GRID_W, dtype=F32), rows)
    freqs = ROPE_THETA ** (-jnp.arange(0, ROPE_AXIS_DIM, 2, dtype=F32) / ROPE_AXIS_DIM)
    ang_r = row_ids[:, None] * freqs
    ang_c = col_ids[:, None] * freqs
    cr, sr, cc, sc = jnp.cos(ang_r), jnp.sin(ang_r), jnp.cos(ang_c), jnp.sin(ang_c)
    z = jnp.zeros_like(sr)
    cos = jnp.concatenate([cr, cr, cc, cc], axis=1)
    sin_b = jnp.concatenate([-sr, z, -sc, z], axis=1)
    sin_a = jnp.concatenate([z, sr, z, sc], axis=1)
    return cos, sin_a, sin_b


def _attn_kernel(q_ref, k_ref, vt_ref, o_ref, m_ref, acc_ref, al0, al1, s0, s1, p0, p1, qt_ref):
    al_ref, s_ref, p_ref = (al0, al1), (s0, s1), (p0, p1)
    nchunk = vt_ref.shape[0]
    tk = vt_ref.shape[2]

    heads = range(Q_GROUP)
    for g in heads:
        qt_ref[g] = q_ref[:, g * HEAD_DIM:(g + 1) * HEAD_DIM].astype(F32).T.astype(BF16)

    def scores_into(c, slot, hs=heads):
        off = pl.multiple_of(c * tk, tk)
        kc = k_ref[pl.ds(off, tk), :]
        for g in hs:
            s_ref[slot][g] = jnp.dot(kc, qt_ref[g], preferred_element_type=F32)

    def softmax(slot, hs=heads):
        for g in hs:
            sc = s_ref[slot][g]
            m_old = m_ref[g]
            m_new = jnp.maximum(m_old, jnp.max(sc, axis=0, keepdims=True))
            al_ref[slot][g] = jnp.exp2(m_old - m_new)
            m_ref[g] = m_new
            p_ref[slot][g] = jnp.exp2((sc - m_new).astype(BF16))

    def pv_update(c, slot, hs=heads):
        vc = vt_ref[c]
        for g in hs:
            acc_ref[g] = al_ref[slot][g] * acc_ref[g] + jnp.dot(vc, p_ref[slot][g],
                                                                preferred_element_type=F32)

    def step(c, slot, prefetch_scores=True):
        if prefetch_scores:
            scores_into(c + 1, 1 - slot)
        pv_update(c - 1, 1 - slot)
        softmax(slot)

    m_ref[...] = jnp.full(m_ref.shape, -jnp.inf, F32)
    acc_ref[...] = jnp.zeros(acc_ref.shape, F32)
    scores_into(0, 0)
    softmax(0)
    scores_into(1, 1)

    def pair(i, carry):
        step(2 * i + 1, 1)
        step(2 * i + 2, 0)
        return carry

    lax.fori_loop(0, (nchunk - 2) // 2, pair, 0)
    last = nchunk - 1
    step(last, 1, prefetch_scores=False)
    pv_update(last, 1)
    for g in range(Q_GROUP):
        acc = acc_ref[g]
        out_t = acc[:HEAD_DIM] / acc[HEAD_DIM:HEAD_DIM + 1]
        o_ref[:, g * HEAD_DIM:(g + 1) * HEAD_DIM] = out_t.T.astype(BF16)


def _attention(q, k, v, tq, tk):
    bsz, s, _ = q.shape
    assert (s // tk) % 2 == 0
    gw = Q_GROUP * HEAD_DIM
    rows = HEAD_DIM + ONES_ROWS
    vt = v.reshape(bsz, s // tk, tk, N_KV_HEADS, HEAD_DIM).transpose(0, 3, 1, 4, 2)
    vt = jnp.concatenate([vt, jnp.ones((bsz, N_KV_HEADS, s // tk, ONES_ROWS, tk), BF16)], axis=3)
    qspec = pl.BlockSpec((None, tq, gw), lambda b, h, i: (b, i, h))
    kspec = pl.BlockSpec((None, s, HEAD_DIM), lambda b, h, i: (b, 0, h))
    vspec = pl.BlockSpec((None, None, s // tk, rows, tk), lambda b, h, i: (b, h, 0, 0, 0))
    return pl.pallas_call(
        _attn_kernel,
        grid=(bsz, N_KV_HEADS, s // tq),
        in_specs=[qspec, kspec, vspec],
        out_specs=qspec,
        out_shape=jax.ShapeDtypeStruct((bsz, s, Q_DIM), BF16),
        scratch_shapes=[pltpu.VMEM((Q_GROUP, 1, tq), F32), pltpu.VMEM((Q_GROUP, rows, tq), F32)]
        + [pltpu.VMEM((Q_GROUP, 1, tq), F32)] * 2 + [pltpu.VMEM((Q_GROUP, tk, tq), F32)] * 2
        + [pltpu.VMEM((Q_GROUP, tk, tq), BF16)] * 2 + [pltpu.VMEM((Q_GROUP, HEAD_DIM, tq), BF16)],
        compiler_params=_cparams(("parallel", "parallel", "parallel")),
        name="attn",
    )(q, k, vt)


def _mix_kernel(yf_ref, at_ref, gf_ref, ga_ref, x_ref, g1_ref, sc2_ref, sh2_ref,
                wf_ref, wo_ref, wout_ref, l1g_ref, l1b_ref, wr_ref, br_ref, tri_ref,
                x1_ref, u2_ref, idx_ref, wgt_ref, rank_ref, cnt_ref, cnt_acc):
    first = jnp.logical_and(pl.program_id(0) == 0, pl.program_id(1) == 0)

    @pl.when(first)
    def _():
        cnt_acc[...] = jnp.zeros_like(cnt_acc)

    yf = jnp.dot(yf_ref[...], wf_ref[...], preferred_element_type=F32)
    ya = jnp.dot(at_ref[...], wo_ref[...], preferred_element_type=F32)
    m = gf_ref[...].astype(F32) * yf + ga_ref[...].astype(F32) * ya
    h = jnp.dot(m.astype(BF16), wout_ref[...], preferred_element_type=F32)
    r = DEEPNORM_ALPHA * x_ref[...] + g1_ref[...] * h
    x1 = _layer_norm(r, LN_EPS) * l1g_ref[...] + l1b_ref[...]
    x1_ref[...] = x1
    u2 = _layer_norm(x1, ADA_EPS) * (1.0 + sc2_ref[...]) + sh2_ref[...]
    u2_ref[...] = _pack_bf16_pairs(u2)

    nt = (((1,), (1,)), ((), ()))
    u_hi = u2.astype(BF16)
    u_lo = (u2 - u_hi.astype(F32)).astype(BF16)
    w = wr_ref[...]
    w_hi = w.astype(BF16)
    w_lo = (w - w_hi.astype(F32)).astype(BF16)
    logits = (lax.dot_general(w_hi, u_hi, nt, preferred_element_type=F32)
              + lax.dot_general(w_lo, u_hi, nt, preferred_element_type=F32)
              + lax.dot_general(w_hi, u_lo, nt, preferred_element_type=F32)) + br_ref[...]

    ne, tm = logits.shape
    eid = lax.broadcasted_iota(jnp.int32, (ne, tm), 0)
    work = logits
    vals, idxs, hots = [], [], []
    for _ in range(TOP_K):
        mx = jnp.max(work, axis=0, keepdims=True)
        ix = jnp.min(jnp.where(work == mx, eid, ne), axis=0, keepdims=True)
        hot = eid == ix
        work = jnp.where(hot, -jnp.inf, work)
        vals.append(mx)
        idxs.append(ix)
        hots.append(hot)
    ex = [jnp.exp(v - vals[0]) for v in vals]
    den = ex[0] + ex[1] + ex[2] + ex[3]
    wgt_ref[...] = jnp.concatenate([e / den for e in ex], axis=0)
    idx_ref[...] = jnp.concatenate(idxs, axis=0)

    sel = hots[0] | hots[1] | hots[2] | hots[3]
    mask = jnp.where(sel, 1.0, 0.0)
    prefix = jnp.dot(mask.astype(BF16), tri_ref[...], preferred_element_type=F32)
    pos = prefix + cnt_acc[...]
    ranks = [jnp.sum(jnp.where(hot, pos, 0.0), axis=0, keepdims=True) for hot in hots]
    rank_ref[...] = jnp.concatenate(ranks, axis=0).astype(jnp.int32)
    cnt_acc[...] += jnp.sum(mask, axis=1, keepdims=True)
    cnt_ref[...] = cnt_acc[...].astype(jnp.int32)


def _mix(yfm, attn, gf, ga, x, g1, sc2, sh2, wf, wo, wout, l1g, l1b, wr_t, br, tm):
    bsz, s, d = x.shape
    n = bsz * s
    nt = s // tm
    tri = (jnp.arange(tm)[:, None] < jnp.arange(tm)[None, :]).astype(BF16)
    row = lambda w: pl.BlockSpec((None, tm, w), lambda b, i: (b, i, 0))
    mod = pl.BlockSpec((None, 1, d), lambda b, i: (b, 0, 0))
    full = lambda shp: pl.BlockSpec(shp, lambda b, i: (0,) * len(shp))
    tok = pl.BlockSpec((TOP_K, tm), lambda b, i: (0, b * nt + i))
    return pl.pallas_call(
        _mix_kernel,
        grid=(bsz, nt),
        in_specs=[row(FOURIER_DIM), row(Q_DIM), row(d), row(d), row(d), mod, mod, mod,
                  full(wf.shape), full(wo.shape), full(wout.shape), full((1, d)), full((1, d)),
                  full(wr_t.shape), full((N_EXPERTS, 1)), full((tm, tm))],
        out_specs=[row(d), row(d // 2), tok, tok, tok, full((N_EXPERTS, 1))],
        out_shape=[jax.ShapeDtypeStruct((bsz, s, d), F32),
                   jax.ShapeDtypeStruct((bsz, s, d // 2), jnp.int32),
                   jax.ShapeDtypeStruct((TOP_K, n), jnp.int32),
                   jax.ShapeDtypeStruct((TOP_K, n), F32),
                   jax.ShapeDtypeStruct((TOP_K, n), jnp.int32),
                   jax.ShapeDtypeStruct((N_EXPERTS, 1), jnp.int32)],
        scratch_shapes=[pltpu.VMEM((N_EXPERTS, 1), F32)],
        compiler_params=_cparams(("arbitrary", "arbitrary")),
        name="mix",
    )(yfm, attn, gf, ga, x, g1, sc2, sh2, wf, wo, wout, l1g, l1b, wr_t, br, tri)


def _expert_kernel(be_ref, nv_ref, new_ref, xs_ref, wgu_ref, bgu_ref, wd_ref, bd_ref, ys_ref,
                   wgu_bf, wd_bf):
    del be_ref
    i = pl.program_id(0)
    nvalid = nv_ref[i]

    @pl.when(new_ref[i] == 1)
    def _():
        wgu_bf[...] = wgu_ref[...].astype(BF16)
        wd_bf[...] = wd_ref[...].astype(BF16)

    @pl.when(nvalid > 0)
    def _():
        xw = xs_ref[...]
        row = lax.broadcasted_iota(jnp.int32, xw.shape, 0)
        lo, hi = _unpack_bf16_pairs(jnp.where(row < nvalid, xw, 0))
        xb = jnp.concatenate([lo.astype(BF16), hi.astype(BF16)], axis=1)
        h = jnp.dot(xb, wgu_bf[...], preferred_element_type=F32) + bgu_ref[...]
        gate = jnp.minimum(h[:, :D_EXPERT], SWIGLU_LIMIT)
        up = jnp.clip(h[:, D_EXPERT:], -SWIGLU_LIMIT, SWIGLU_LIMIT)
        act = (up + 1.0) * gate * jax.nn.sigmoid(SWIGLU_ALPHA * gate)
        y = jnp.dot(act.astype(BF16), wd_bf[...], preferred_element_type=F32) + bd_ref[...]
        ys_ref[...] = _pack_bf16_pairs(y)

    @pl.when(nvalid == 0)
    def _():
        ys_ref[...] = jnp.zeros_like(ys_ref)


def _experts(block_e, block_nvalid, block_new, xs, wgu, bgu, wd, bd, tmb):
    n_pad, dh = xs.shape
    d = 2 * dh
    grid_spec = pltpu.PrefetchScalarGridSpec(
        num_scalar_prefetch=3,
        grid=(n_pad // tmb,),
        in_specs=[pl.BlockSpec((tmb, dh), lambda i, be, nv, nw: (i, 0)),
                  pl.BlockSpec((None, d, 2 * D_EXPERT), lambda i, be, nv, nw: (be[i], 0, 0)),
                  pl.BlockSpec((None, 1, 2 * D_EXPERT), lambda i, be, nv, nw: (be[i], 0, 0)),
                  pl.BlockSpec((None, D_EXPERT, d), lambda i, be, nv, nw: (be[i], 0, 0)),
                  pl.BlockSpec((None, 1, d), lambda i, be, nv, nw: (be[i], 0, 0))],
        out_specs=pl.BlockSpec((tmb, dh), lambda i, be, nv, nw: (i, 0)),
        scratch_shapes=[pltpu.VMEM((d, 2 * D_EXPERT), BF16), pltpu.VMEM((D_EXPERT, d), BF16)],
    )
    return pl.pallas_call(
        _expert_kernel,
        grid_spec=grid_spec,
        out_shape=jax.ShapeDtypeStruct((n_pad, dh), jnp.int32),
        compiler_params=_cparams(("arbitrary",)),
        name="experts",
    )(block_e, block_nvalid, block_new, xs, wgu, bgu, wd, bd)


def _final_kernel(yg_ref, w_ref, x1_ref, g2_ref, lg_ref, lb_ref, o_ref):
    w = w_ref[...]
    acc_lo = None
    for j in range(TOP_K):
        lo, hi = _unpack_bf16_pairs(yg_ref[j])
        wj = w[:, j:j + 1]
        acc_lo = lo * wj if acc_lo is None else acc_lo + lo * wj
        acc_hi = hi * wj if j == 0 else acc_hi + hi * wj
    h = jnp.concatenate([acc_lo, acc_hi], axis=1)
    r = DEEPNORM_ALPHA * x1_ref[...] + g2_ref[...] * h
    o_ref[...] = _layer_norm(r, LN_EPS) * lg_ref[...] + lb_ref[...]


def _final(yg, w_tok, x1, g2, lg, lb, tm):
    bsz, s, d = x1.shape
    nt = s // tm
    row = pl.BlockSpec((None, tm, d), lambda b, i: (b, i, 0))
    full = pl.BlockSpec((1, d), lambda b, i: (0, 0))
    return pl.pallas_call(
        _final_kernel,
        grid=(bsz, nt),
        in_specs=[pl.BlockSpec((TOP_K, tm, d // 2), lambda b, i: (0, b * nt + i, 0)),
                  pl.BlockSpec((tm, TOP_K), lambda b, i: (b * nt + i, 0)),
                  row, pl.BlockSpec((None, 1, d), lambda b, i: (b, 0, 0)), full, full],
        out_specs=row,
        out_shape=jax.ShapeDtypeStruct((bsz, s, d), F32),
        compiler_params=_cparams(("parallel", "parallel")),
        name="final",
    )(yg, w_tok, x1, g2, lg, lb)


def _scatter_rows(rows, dest, m_out):
    n, w = rows.shape
    nj = dest.shape[0]
    ch = GATHER_CHUNK
    nw = SC_CORES * SC_SUBCORES
    nch = n // (nw * ch)
    assert n == nw * nch * ch and nch % 2 == 0
    per_w = nch * ch
    mesh = plsc.VectorSubcoreMesh(core_axis_name="c", subcore_axis_name="s",
                                  num_cores=SC_CORES, num_subcores=SC_SUBCORES)

    @functools.partial(
        pl.kernel, mesh=mesh,
        out_type=jax.ShapeDtypeStruct((m_out, w), rows.dtype),
        scratch_types=[pltpu.VMEM((nj, nch, ch), jnp.int32), pltpu.VMEM((2, ch, w), rows.dtype),
                       pltpu.SemaphoreType.DMA((2,)), pltpu.SemaphoreType.DMA((2,))],
    )
    def scatter_kernel(rows_hbm, dest_hbm, out_hbm, idx_v, rows_v, rsem, ssem):
        wid = lax.axis_index("s") * SC_CORES + lax.axis_index("c")
        base = wid * per_w
        for j in range(nj):
            pltpu.sync_copy(dest_hbm.at[j, wid], idx_v.at[j])

        def read(i, b):
            off = pl.multiple_of(base + i * ch, ch)
            return pltpu.make_async_copy(rows_hbm.at[pl.ds(off, ch)], rows_v.at[b], rsem.at[b])

        def scatter(i, b, j):
            return pltpu.make_async_copy(rows_v.at[b], out_hbm.at[idx_v.at[j, i]], ssem.at[b])

        read(0, 0).start()

        @pl.loop(0, nch, step=2)
        def _(i0):
            for b in range(2):
                i = i0 + b
                read(i, b).wait()

                @pl.when(i + 1 < nch)
                def _():
                    @pl.when(i >= 1)
                    def _():
                        for j in range(nj):
                            scatter(i - 1, 1 - b, j).wait()
                    read(i + 1, 1 - b).start()

                for j in range(nj):
                    scatter(i, b, j).start()

        for b in range(2):
            for j in range(nj):
                scatter(nch - 2 + b, b, j).wait()

    return scatter_kernel(rows, dest.reshape(nj, nw, nch, ch))


def _gather_rows(table, idx):
    m = idx.shape[0]
    w = table.shape[1]
    ch = GATHER_CHUNK
    nw = SC_CORES * SC_SUBCORES
    nch = m // (nw * ch)
    assert m == nw * nch * ch and nch % 2 == 0
    per_w = nch * ch
    mesh = plsc.VectorSubcoreMesh(core_axis_name="c", subcore_axis_name="s",
                                  num_cores=SC_CORES, num_subcores=SC_SUBCORES)

    @functools.partial(
        pl.kernel, mesh=mesh,
        out_type=jax.ShapeDtypeStruct((m, w), table.dtype),
        scratch_types=[pltpu.VMEM((nch, ch), jnp.int32), pltpu.VMEM((2, ch, w), table.dtype),
                       pltpu.SemaphoreType.DMA((2,)), pltpu.SemaphoreType.DMA((2,))],
    )
    def gather_kernel(table_hbm, idx_hbm, out_hbm, idx_v, rows_v, gsem, wsem):
        wid = lax.axis_index("s") * SC_CORES + lax.axis_index("c")
        base = wid * per_w
        pltpu.sync_copy(idx_hbm.at[wid], idx_v)

        def gather(j, b):
            return pltpu.make_async_copy(table_hbm.at[idx_v.at[j]], rows_v.at[b], gsem.at[b])

        def write(j, b):
            off = pl.multiple_of(base + j * ch, ch)
            return pltpu.make_async_copy(rows_v.at[b], out_hbm.at[pl.ds(off, ch)], wsem.at[b])

        gather(0, 0).start()

        @pl.loop(0, nch, step=2)
        def _(i):
            for b in range(2):
                j = i + b
                gather(j, b).wait()

                @pl.when(j + 1 < nch)
                def _():
                    @pl.when(j >= 1)
                    def _():
                        write(j - 1, 1 - b).wait()
                    gather(j + 1, 1 - b).start()

                write(j, b).start()

        write(nch - 2, 0).wait()
        write(nch - 1, 1).wait()

    return gather_kernel(table, idx.reshape(nw, nch, ch))


def kernel(x, c, w_ada, b_ada, w_in, q_norm_g, k_norm_g, w_fourier, w_attn_o, w_out, ln1_g, ln1_b,
           w_router, b_router, w_gate_up, b_gate_up, w_down, b_down, ln2_g, ln2_b):
    bsz, s, d = x.shape
    n = bsz * s
    tm = min(ROW_TILE, s)
    l = 0

    mod = _ada(c, w_ada[l], b_ada[l])
    sh1, sc1, g1, sh2, sc2, g2 = [m[:, None, :] for m in jnp.split(mod, N_MOD, axis=-1)]

    dft_c, m1, m2, cw, sw, p, q = _dft_tables(s)
    cos, sin_a, sin_b = _rope_tables(s)

    fa, fb, qh, kh, vh, gf, ga = _inproj(
        x, sc1, sh1, w_in[l].astype(BF16), dft_c,
        q_norm_g[l].reshape(1, HEAD_DIM), k_norm_g[l].reshape(1, HEAD_DIM), cos, sin_a, sin_b, tm)

    zr, zi = _fft1(fa, fb, m1, cw, sw, p, q)
    yfm = _fft2(zr, zi, m2)

    attn = _attention(qh, kh, vh, min(ATTN_TQ, s), min(ATTN_TK, s))

    x1, u2p, idx_t, wgt_t, rank_t, counts = _mix(
        yfm, attn, gf, ga, x, g1, sc2, sh2,
        w_fourier[l].astype(BF16), w_attn_o[l].astype(BF16), w_out[l].astype(BF16),
        ln1_g[l].reshape(1, d), ln1_b[l].reshape(1, d),
        w_router[l].T, b_router[l].reshape(N_EXPERTS, 1), tm)

    tmb = MOE_BLOCK
    counts = counts.reshape(N_EXPERTS)
    pcounts = ((counts + tmb - 1) // tmb) * tmb
    pends = jnp.cumsum(pcounts)
    pstarts = pends - pcounts
    n_pad = n * TOP_K + N_EXPERTS * tmb
    nblk = n_pad // tmb
    eids = jnp.arange(N_EXPERTS, dtype=jnp.int32)
    start_of = jnp.sum(jnp.where(idx_t[None] == eids[:, None, None], pstarts[:, None, None], 0), axis=0)
    dest_t = start_of + rank_t
    block_starts = jnp.arange(nblk, dtype=jnp.int32) * tmb
    block_e = jnp.sum((block_starts[:, None] >= pends[None, :]).astype(jnp.int32), axis=1)
    block_e = jnp.minimum(block_e, N_EXPERTS - 1)
    block_nvalid = jnp.clip(pstarts[block_e] + counts[block_e] - block_starts, 0, tmb)
    block_new = jnp.concatenate([jnp.ones((1,), jnp.int32),
                                 (block_e[1:] != block_e[:-1]).astype(jnp.int32)])

    xs = _scatter_rows(u2p.reshape(n, d // 2), dest_t, n_pad)
    ys = _experts(block_e, block_nvalid, block_new, xs, w_gate_up[l], b_gate_up[l][:, None, :],
                  w_down[l], b_down[l][:, None, :], tmb)
    yg = _gather_rows(ys, dest_t.reshape(-1)).reshape(TOP_K, n, d // 2)

    return _final(yg, wgt_t.T, x1, g2, ln2_g[l].reshape(1, d), ln2_b[l].reshape(1, d), tm)
```

```python
import functools
import math

import jax
import jax.numpy as jnp
import numpy as np
from jax import lax
from jax.experimental import pallas as pl
from jax.experimental.pallas import tpu as pltpu
from jax.experimental.pallas import tpu_sc as plsc

F32 = jnp.float32
BF16 = jnp.bfloat16

D_MODEL = 1024
GRID_W = 64
N_Q_HEADS = 8
N_KV_HEADS = 2
HEAD_DIM = 128
Q_GROUP = N_Q_HEADS // N_KV_HEADS
ROPE_THETA = 10000.0
ROPE_AXIS_DIM = HEAD_DIM // 2
N_FOURIER_GROUPS = 4
FOURIER_GROUP_DIM = 128
FOURIER_DIM = N_FOURIER_GROUPS * FOURIER_GROUP_DIM
Q_DIM = N_Q_HEADS * HEAD_DIM
KV_DIM = N_KV_HEADS * HEAD_DIM
N_EXPERTS = 32
TOP_K = 4
D_EXPERT = 1024
SWIGLU_LIMIT = 7.0
SWIGLU_ALPHA = 1.702
N_MOD = 6
DEPTH = 1
DEEPNORM_ALPHA = (2 * DEPTH) ** 0.25
LN_EPS = 1e-5
ADA_EPS = 1e-6
QK_EPS = 1e-6
LOG2E = 1.4426950408889634

V7X_VMEM_BYTES = 64 * 1024 * 1024
VMEM_LIMIT = V7X_VMEM_BYTES - 8 * 1024 * 1024
LANES = 128
BF16_SUBLANES = 16

FFT_Q = 128
ROW_TILE = 512
ATTN_TQ = 256
ATTN_TK = 512
ONES_ROWS = 16
ATTN_UNROLL = 2
MOE_BLOCK = 512

SC_CORES = 2
SC_SUBCORES = 16
GATHER_CHUNK = 64


def _cparams(sem):
    return pltpu.CompilerParams(dimension_semantics=sem, vmem_limit_bytes=VMEM_LIMIT)


def _layer_norm(x, eps):
    mu = jnp.mean(x, axis=-1, keepdims=True)
    xc = x - mu
    var = jnp.mean(xc * xc, axis=-1, keepdims=True)
    return xc * lax.rsqrt(var + eps)


def _pack_bf16_pairs(x):
    n = x.shape[1] // 2
    bits = pltpu.bitcast(x.astype(BF16).astype(F32), jnp.uint32)
    word = (bits[:, :n] >> 16) | (bits[:, n:] & jnp.uint32(0xFFFF0000))
    return pltpu.bitcast(word, jnp.int32)


def _unpack_bf16_pairs(w):
    bits = pltpu.bitcast(w, jnp.uint32)
    lo = pltpu.bitcast(bits << 16, F32)
    hi = pltpu.bitcast(bits & jnp.uint32(0xFFFF0000), F32)
    return lo, hi


def _ada_kernel(c_ref, w_ref, b_ref, o_ref):
    c = c_ref[...]
    cond = c * jax.nn.sigmoid(c)
    o_ref[...] = jnp.dot(cond, w_ref[...], preferred_element_type=F32,
                         precision=lax.Precision.HIGHEST) + b_ref[...]


def _ada(c, w_ada, b_ada):
    bsz, d = c.shape
    n = w_ada.shape[1]
    tn = 1536
    return pl.pallas_call(
        _ada_kernel,
        grid=(n // tn,),
        in_specs=[pl.BlockSpec((bsz, d), lambda j: (0, 0)),
                  pl.BlockSpec((d, tn), lambda j: (0, j)),
                  pl.BlockSpec((1, tn), lambda j: (0, j))],
        out_specs=pl.BlockSpec((bsz, tn), lambda j: (0, j)),
        out_shape=jax.ShapeDtypeStruct((bsz, n), F32),
        compiler_params=_cparams(("arbitrary",)),
        name="ada",
    )(c, w_ada, b_ada.reshape(1, n))


def _rope(xh, cos, sin_a, sin_b):
    a = pltpu.roll(xh, 32, 1)
    b = pltpu.roll(xh, 96, 1)
    return xh * cos + b * sin_b + a * sin_a


def _inproj_kernel(x_ref, sc_ref, sh_ref, w_ref, dft_ref, qg_ref, kg_ref,
                   cos_ref, sa_ref, sb_ref,
                   fa_ref, fb_ref, q_ref, k_ref, v_ref, gf_ref, ga_ref):
    x = x_ref[...]
    u = _layer_norm(x, ADA_EPS) * (1.0 + sc_ref[...]) + sh_ref[...]
    ub = u.astype(BF16)
    cos, sa, sb = cos_ref[...], sa_ref[...], sb_ref[...]

    c0 = 0
    zf = jnp.dot(ub, w_ref[:, c0:c0 + FOURIER_DIM], preferred_element_type=F32).astype(BF16)
    for g in range(N_FOURIER_GROUPS):
        sl = slice(g * FOURIER_GROUP_DIM, (g + 1) * FOURIER_GROUP_DIM)
        ab = jnp.dot(zf[:, sl], dft_ref[...], preferred_element_type=F32)
        fa_ref[:, sl] = ab[:, :FOURIER_GROUP_DIM].astype(BF16)
        fb_ref[:, sl] = ab[:, FOURIER_GROUP_DIM:].astype(BF16)
    c0 += FOURIER_DIM

    def norm_rope(z, gain, scale):
        ms = jnp.mean(z * z, axis=-1, keepdims=True)
        zn = z * lax.rsqrt(ms + QK_EPS) * gain
        return (_rope(zn, cos, sa, sb) * scale).astype(BF16)

    q_scale = LOG2E / math.sqrt(HEAD_DIM)
    zq = jnp.dot(ub, w_ref[:, c0:c0 + Q_DIM], preferred_element_type=F32)
    for h in range(N_Q_HEADS):
        sl = slice(h * HEAD_DIM, (h + 1) * HEAD_DIM)
        q_ref[:, sl] = norm_rope(zq[:, sl], qg_ref[...], q_scale)
    c0 += Q_DIM

    zk = jnp.dot(ub, w_ref[:, c0:c0 + KV_DIM], preferred_element_type=F32)
    for h in range(N_KV_HEADS):
        sl = slice(h * HEAD_DIM, (h + 1) * HEAD_DIM)
        k_ref[:, sl] = norm_rope(zk[:, sl], kg_ref[...], 1.0)
    c0 += KV_DIM

    v_ref[...] = jnp.dot(ub, w_ref[:, c0:c0 + KV_DIM], preferred_element_type=F32).astype(BF16)
    c0 += KV_DIM

    zg = jnp.dot(ub, w_ref[:, c0:c0 + D_MODEL], preferred_element_type=F32)
    gf_ref[...] = jax.nn.sigmoid(zg).astype(BF16)
    c0 += D_MODEL
    zg = jnp.dot(ub, w_ref[:, c0:c0 + D_MODEL], preferred_element_type=F32)
    ga_ref[...] = jax.nn.sigmoid(zg).astype(BF16)


def _inproj(x, sc1, sh1, w_in, dft_c, qg, kg, cos, sin_a, sin_b, tm):
    bsz, s, d = x.shape
    ncol = w_in.shape[1]
    row = lambda w: pl.BlockSpec((None, tm, w), lambda b, i: (b, i, 0))
    mod = pl.BlockSpec((None, 1, d), lambda b, i: (b, 0, 0))
    full = lambda shp: pl.BlockSpec(shp, lambda b, i: (0,) * len(shp))
    tab = pl.BlockSpec((tm, HEAD_DIM), lambda b, i: (i, 0))
    outs = [FOURIER_DIM, FOURIER_DIM, Q_DIM, KV_DIM, KV_DIM, D_MODEL, D_MODEL]
    return pl.pallas_call(
        _inproj_kernel,
        grid=(bsz, s // tm),
        in_specs=[row(d), mod, mod, full((d, ncol)), full(dft_c.shape),
                  full((1, HEAD_DIM)), full((1, HEAD_DIM)), tab, tab, tab],
        out_specs=[row(w) for w in outs],
        out_shape=[jax.ShapeDtypeStruct((bsz, s, w), BF16) for w in outs],
        compiler_params=_cparams(("parallel", "parallel")),
        name="inproj",
    )(x, sc1, sh1, w_in, dft_c, qg, kg, cos, sin_a, sin_b)


def _fft1_kernel(a_ref, b_ref, m_ref, cw_ref, sw_ref, zr_ref, zi_ref, a32, b32):
    p, nb, _ = a_ref.shape
    a32[...] = a_ref[...].astype(F32)
    b32[...] = b_ref[...].astype(F32)
    for j in range(nb):
        ab = jnp.concatenate([a32[:, j, :], b32[:, j, :]], axis=0).astype(BF16)
        t = jnp.dot(m_ref[...], ab, preferred_element_type=F32)
        tr, ti = t[:p], t[p:]
        cw, sw = cw_ref[j], sw_ref[j]
        zr_ref[j] = (tr * cw - ti * sw).astype(BF16)
        zi_ref[j] = (ti * cw + tr * sw).astype(BF16)


def _fft1(fa, fb, m1, cw, sw, p, q):
    bsz, s, c = fa.shape
    nb = BF16_SUBLANES
    inp = pl.BlockSpec((None, p, nb, c), lambda b, j: (b, 0, j, 0))
    tw = pl.BlockSpec((nb, p, 1), lambda b, j: (j, 0, 0))
    out = pl.BlockSpec((None, nb, p, c), lambda b, j: (b, j, 0, 0))
    return pl.pallas_call(
        _fft1_kernel,
        grid=(bsz, q // nb),
        in_specs=[inp, inp, pl.BlockSpec(m1.shape, lambda b, j: (0, 0)), tw, tw],
        out_specs=[out, out],
        out_shape=[jax.ShapeDtypeStruct((bsz, q, p, c), BF16)] * 2,
        scratch_shapes=[pltpu.VMEM((p, nb, c), F32)] * 2,
        compiler_params=_cparams(("parallel", "parallel")),
        name="fft1",
    )(fa.reshape(bsz, p, q, c), fb.reshape(bsz, p, q, c), m1, cw, sw)


def _fft2_kernel(zr_ref, zi_ref, m_ref, y_ref, zr32, zi32, y32):
    zr32[...] = zr_ref[...].astype(F32)
    zi32[...] = zi_ref[...].astype(F32)
    for j in range(zr_ref.shape[1]):
        z = jnp.concatenate([zr32[:, j, :], zi32[:, j, :]], axis=0).astype(BF16)
        y32[:, j, :] = jnp.dot(m_ref[...], z, preferred_element_type=F32)
    y_ref[...] = y32[...].astype(BF16)


def _fft2(zr, zi, m2):
    bsz, q, p, c = zr.shape
    pc = BF16_SUBLANES
    blk = pl.BlockSpec((None, q, pc, c), lambda b, j: (b, 0, j, 0))
    y = pl.pallas_call(
        _fft2_kernel,
        grid=(bsz, p // pc),
        in_specs=[blk, blk, pl.BlockSpec(m2.shape, lambda b, j: (0, 0))],
        out_specs=blk,
        out_shape=jax.ShapeDtypeStruct((bsz, q, p, c), BF16),
        scratch_shapes=[pltpu.VMEM((q, pc, c), F32)] * 3,
        compiler_params=_cparams(("parallel", "parallel")),
        name="fft2",
    )(zr, zi, m2)
    return y.reshape(bsz, q * p, c)


def _dft_tables(s):
    q = FFT_Q
    p = s // q
    two_pi = 2.0 * math.pi

    def cs(n_rows, n_cols, n):
        r = jnp.arange(n_rows, dtype=jnp.int32)[:, None]
        c = jnp.arange(n_cols, dtype=jnp.int32)[None, :]
        ang = ((r * c) % n).astype(F32) * (two_pi / n)
        return jnp.cos(ang), jnp.sin(ang)

    cc, sc = cs(FOURIER_GROUP_DIM, FOURIER_GROUP_DIM, FOURIER_GROUP_DIM)
    dft_c = jnp.concatenate([cc, sc], axis=1).astype(BF16)
    cp, sp = cs(p, p, p)
    m1 = jnp.concatenate([jnp.concatenate([cp, -sp], axis=1),
                          jnp.concatenate([sp, cp], axis=1)], axis=0).astype(BF16)
    cq, sq = cs(q, q, q)
    scale = 1.0 / math.sqrt(s * FOURIER_GROUP_DIM)
    m2 = (jnp.concatenate([cq, -sq], axis=1) * scale).astype(BF16)
    cw, sw = cs(q, p, s)
    return dft_c, m1, m2, cw[:, :, None], sw[:, :, None], p, q


def _rope_tables(s):
    rows = s // GRID_W
    row_ids = jnp.repeat(jnp.arange(rows, dtype=F32), GRID_W)
    col_ids = jnp.tile(jnp.arange(GRID_W, dtype=F32), rows)
    freqs = ROPE_THETA ** (-jnp.arange(0, ROPE_AXIS_DIM, 2, dtype=F32) / ROPE_AXIS_DIM)
    ang_r = row_ids[:, None] * freqs
    ang_c = col_ids[:, None] * freqs
    cr, sr, cc, sc = jnp.cos(ang_r), jnp.sin(ang_r), jnp.cos(ang_c), jnp.sin(ang_c)
    z = jnp.zeros_like(sr)
    cos = jnp.concatenate([cr, cr, cc, cc], axis=1)
    sin_b = jnp.concatenate([-sr, z, -sc, z], axis=1)
    sin_a = jnp.concatenate([z, sr, z, sc], axis=1)
    return cos, sin_a, sin_b


def _attn_kernel(q_ref, k_ref, vt_ref, o_ref, m_ref, acc_ref, al0, al1, s0, s1, p0, p1, qt_ref, mx0, mx1):
    al_ref, s_ref, p_ref, mx_ref = (al0, al1), (s0, s1), (p0, p1), (mx0, mx1)
    nchunk = vt_ref.shape[0]
    tk = vt_ref.shape[2]

    heads = range(Q_GROUP)
    for g in heads:
        qt_ref[g] = q_ref[:, g * HEAD_DIM:(g + 1) * HEAD_DIM].astype(F32).T.astype(BF16)

    def scores_into(c, slot, hs=heads):
        off = pl.multiple_of(c * tk, tk)
        kc = k_ref[pl.ds(off, tk), :]
        for g in hs:
            sc = jnp.dot(kc, qt_ref[g], preferred_element_type=F32)
            s_ref[slot][g] = sc
            parts = [jnp.max(sc[r * (tk // 4):(r + 1) * (tk // 4)], axis=0, keepdims=True) for r in range(4)]
            mx_ref[slot][g] = jnp.maximum(jnp.maximum(parts[0], parts[1]), jnp.maximum(parts[2], parts[3]))

    def softmax(slot, hs=heads):
        for g in hs:
            sc = s_ref[slot][g]
            m_old = m_ref[g]
            m_new = jnp.maximum(m_old, mx_ref[slot][g])
            al_ref[slot][g] = jnp.exp2(m_old - m_new)
            m_ref[g] = m_new
            p_ref[slot][g] = jnp.exp2(sc - m_new).astype(BF16)

    def pv_update(c, slot, hs=heads):
        vc = vt_ref[c]
        for g in hs:
            acc_ref[g] = al_ref[slot][g] * acc_ref[g] + jnp.dot(vc, p_ref[slot][g],
                                                                preferred_element_type=F32)

    def step(c, slot, prefetch_scores=True):
        if prefetch_scores:
            scores_into(c + 1, 1 - slot)
        pv_update(c - 1, 1 - slot)
        softmax(slot)

    m_ref[...] = jnp.full(m_ref.shape, -jnp.inf, F32)
    acc_ref[...] = jnp.zeros(acc_ref.shape, F32)
    scores_into(0, 0)
    softmax(0)
    scores_into(1, 1)

    nloop = (nchunk - 2) // ATTN_UNROLL

    def trip(i, carry):
        for j in range(ATTN_UNROLL):
            step(ATTN_UNROLL * i + 1 + j, (1 + j) % 2)
        return carry

    lax.fori_loop(0, nloop, trip, 0)
    for c in range(nloop * ATTN_UNROLL + 1, nchunk - 1):
        step(c, c % 2)
    last = nchunk - 1
    step(last, last % 2, prefetch_scores=False)
    pv_update(last, last % 2)
    for g in range(Q_GROUP):
        acc = acc_ref[g]
        out_t = acc[:HEAD_DIM] / acc[HEAD_DIM:HEAD_DIM + 1]
        o_ref[:, g * HEAD_DIM:(g + 1) * HEAD_DIM] = out_t.T.astype(BF16)


def _attention(q, k, v, tq, tk):
    bsz, s, _ = q.shape
    assert (s // tk) % 2 == 0
    gw = Q_GROUP * HEAD_DIM
    rows = HEAD_DIM + ONES_ROWS
    vt = v.reshape(bsz, s // tk, tk, N_KV_HEADS, HEAD_DIM).transpose(0, 3, 1, 4, 2)
    vt = jnp.concatenate([vt, jnp.ones((bsz, N_KV_HEADS, s // tk, ONES_ROWS, tk), BF16)], axis=3)
    qspec = pl.BlockSpec((None, tq, gw), lambda b, h, i: (b, i, h))
    kspec = pl.BlockSpec((None, s, HEAD_DIM), lambda b, h, i: (b, 0, h))
    vspec = pl.BlockSpec((None, None, s // tk, rows, tk), lambda b, h, i: (b, h, 0, 0, 0))
    return pl.pallas_call(
        _attn_kernel,
        grid=(bsz, N_KV_HEADS, s // tq),
        in_specs=[qspec, kspec, vspec],
        out_specs=qspec,
        out_shape=jax.ShapeDtypeStruct((bsz, s, Q_DIM), BF16),
        scratch_shapes=[pltpu.VMEM((Q_GROUP, 1, tq), F32), pltpu.VMEM((Q_GROUP, rows, tq), F32)]
        + [pltpu.VMEM((Q_GROUP, 1, tq), F32)] * 2 + [pltpu.VMEM((Q_GROUP, tk, tq), F32)] * 2
        + [pltpu.VMEM((Q_GROUP, tk, tq), BF16)] * 2 + [pltpu.VMEM((Q_GROUP, HEAD_DIM, tq), BF16)]
        + [pltpu.VMEM((Q_GROUP, 1, tq), F32)] * 2,
        compiler_params=_cparams(("parallel", "parallel", "parallel")),
        name="attn",
    )(q, k, vt)


def _mix_kernel(yf_ref, at_ref, gf_ref, ga_ref, x_ref, g1_ref, sc2_ref, sh2_ref,
                wf_ref, wo_ref, wout_ref, l1g_ref, l1b_ref, wr_ref, br_ref, tri_ref,
                x1_ref, u2_ref, idx_ref, wgt_ref, rank_ref, cnt_ref, cnt_acc):
    first = jnp.logical_and(pl.program_id(0) == 0, pl.program_id(1) == 0)

    @pl.when(first)
    def _():
        cnt_acc[...] = jnp.zeros_like(cnt_acc)

    yf = jnp.dot(yf_ref[...], wf_ref[...], preferred_element_type=F32)
    ya = jnp.dot(at_ref[...], wo_ref[...], preferred_element_type=F32)
    m = gf_ref[...].astype(F32) * yf + ga_ref[...].astype(F32) * ya
    h = jnp.dot(m.astype(BF16), wout_ref[...], preferred_element_type=F32)
    r = DEEPNORM_ALPHA * x_ref[...] + g1_ref[...] * h
    x1 = _layer_norm(r, LN_EPS) * l1g_ref[...] + l1b_ref[...]
    x1_ref[...] = x1
    u2 = _layer_norm(x1, ADA_EPS) * (1.0 + sc2_ref[...]) + sh2_ref[...]
    u2_ref[...] = _pack_bf16_pairs(u2)

    nt = (((1,), (1,)), ((), ()))
    u_hi = u2.astype(BF16)
    u_lo = (u2 - u_hi.astype(F32)).astype(BF16)
    w = wr_ref[...]
    w_hi = w.astype(BF16)
    w_lo = (w - w_hi.astype(F32)).astype(BF16)
    logits = (lax.dot_general(w_hi, u_hi, nt, preferred_element_type=F32)
              + lax.dot_general(w_lo, u_hi, nt, preferred_element_type=F32)
              + lax.dot_general(w_hi, u_lo, nt, preferred_element_type=F32)) + br_ref[...]

    ne, tm = logits.shape
    eid = lax.broadcasted_iota(jnp.int32, (ne, tm), 0)
    work = logits
    vals, idxs, hots = [], [], []
    for _ in range(TOP_K):
        mx = jnp.max(work, axis=0, keepdims=True)
        ix = jnp.min(jnp.where(work == mx, eid, ne), axis=0, keepdims=True)
        hot = eid == ix
        work = jnp.where(hot, -jnp.inf, work)
        vals.append(mx)
        idxs.append(ix)
        hots.append(hot)
    ex = [jnp.exp(v - vals[0]) for v in vals]
    den = ex[0] + ex[1] + ex[2] + ex[3]
    wgt_ref[...] = jnp.concatenate([e / den for e in ex], axis=0)
    idx_ref[...] = jnp.concatenate(idxs, axis=0)

    sel = hots[0] | hots[1] | hots[2] | hots[3]
    mask = jnp.where(sel, 1.0, 0.0)
    prefix = jnp.dot(mask.astype(BF16), tri_ref[...], preferred_element_type=F32)
    pos = prefix + cnt_acc[...]
    ranks = [jnp.sum(jnp.where(hot, pos, 0.0), axis=0, keepdims=True) for hot in hots]
    rank_ref[...] = jnp.concatenate(ranks, axis=0).astype(jnp.int32)
    cnt_acc[...] += jnp.sum(mask, axis=1, keepdims=True)
    cnt_ref[...] = cnt_acc[...].astype(jnp.int32)


def _mix(yfm, attn, gf, ga, x, g1, sc2, sh2, wf, wo, wout, l1g, l1b, wr_t, br, tm):
    bsz, s, d = x.shape
    n = bsz * s
    nt = s // tm
    tri = (jnp.arange(tm)[:, None] < jnp.arange(tm)[None, :]).astype(BF16)
    row = lambda w: pl.BlockSpec((None, tm, w), lambda b, i: (b, i, 0))
    mod = pl.BlockSpec((None, 1, d), lambda b, i: (b, 0, 0))
    full = lambda shp: pl.BlockSpec(shp, lambda b, i: (0,) * len(shp))
    tok = pl.BlockSpec((TOP_K, tm), lambda b, i: (0, b * nt + i))
    return pl.pallas_call(
        _mix_kernel,
        grid=(bsz, nt),
        in_specs=[row(FOURIER_DIM), row(Q_DIM), row(d), row(d), row(d), mod, mod, mod,
                  full(wf.shape), full(wo.shape), full(wout.shape), full((1, d)), full((1, d)),
                  full(wr_t.shape), full((N_EXPERTS, 1)), full((tm, tm))],
        out_specs=[row(d), row(d // 2), tok, tok, tok, full((N_EXPERTS, 1))],
        out_shape=[jax.ShapeDtypeStruct((bsz, s, d), F32),
                   jax.ShapeDtypeStruct((bsz, s, d // 2), jnp.int32),
                   jax.ShapeDtypeStruct((TOP_K, n), jnp.int32),
                   jax.ShapeDtypeStruct((TOP_K, n), F32),
                   jax.ShapeDtypeStruct((TOP_K, n), jnp.int32),
                   jax.ShapeDtypeStruct((N_EXPERTS, 1), jnp.int32)],
        scratch_shapes=[pltpu.VMEM((N_EXPERTS, 1), F32)],
        compiler_params=_cparams(("arbitrary", "arbitrary")),
        name="mix",
    )(yfm, attn, gf, ga, x, g1, sc2, sh2, wf, wo, wout, l1g, l1b, wr_t, br, tri)


def _expert_kernel(be_ref, nv_ref, new_ref, xs_ref, wgu_ref, bgu_ref, wd_ref, bd_ref, ys_ref,
                   wgu_bf, wd_bf):
    del be_ref
    i = pl.program_id(0)
    nvalid = nv_ref[i]

    @pl.when(new_ref[i] == 1)
    def _():
        wgu_bf[...] = wgu_ref[...].astype(BF16)
        wd_bf[...] = wd_ref[...].astype(BF16)

    @pl.when(nvalid > 0)
    def _():
        xw = xs_ref[...]
        row = lax.broadcasted_iota(jnp.int32, xw.shape, 0)
        lo, hi = _unpack_bf16_pairs(jnp.where(row < nvalid, xw, 0))
        xb = jnp.concatenate([lo.astype(BF16), hi.astype(BF16)], axis=1)
        h = jnp.dot(xb, wgu_bf[...], preferred_element_type=F32) + bgu_ref[...]
        gate = jnp.minimum(h[:, :D_EXPERT], SWIGLU_LIMIT)
        up = jnp.clip(h[:, D_EXPERT:], -SWIGLU_LIMIT, SWIGLU_LIMIT)
        act = (up + 1.0) * gate * jax.nn.sigmoid(SWIGLU_ALPHA * gate)
        y = jnp.dot(act.astype(BF16), wd_bf[...], preferred_element_type=F32) + bd_ref[...]
        ys_ref[...] = _pack_bf16_pairs(y)

    @pl.when(nvalid == 0)
    def _():
        ys_ref[...] = jnp.zeros_like(ys_ref)


def _experts(block_e, block_nvalid, block_new, xs, wgu, bgu, wd, bd, tmb):
    n_pad, dh = xs.shape
    d = 2 * dh
    grid_spec = pltpu.PrefetchScalarGridSpec(
        num_scalar_prefetch=3,
        grid=(n_pad // tmb,),
        in_specs=[pl.BlockSpec((tmb, dh), lambda i, be, nv, nw: (i, 0)),
                  pl.BlockSpec((None, d, 2 * D_EXPERT), lambda i, be, nv, nw: (be[i], 0, 0)),
                  pl.BlockSpec((None, 1, 2 * D_EXPERT), lambda i, be, nv, nw: (be[i], 0, 0)),
                  pl.BlockSpec((None, D_EXPERT, d), lambda i, be, nv, nw: (be[i], 0, 0)),
                  pl.BlockSpec((None, 1, d), lambda i, be, nv, nw: (be[i], 0, 0))],
        out_specs=pl.BlockSpec((tmb, dh), lambda i, be, nv, nw: (i, 0)),
        scratch_shapes=[pltpu.VMEM((d, 2 * D_EXPERT), BF16), pltpu.VMEM((D_EXPERT, d), BF16)],
    )
    return pl.pallas_call(
        _expert_kernel,
        grid_spec=grid_spec,
        out_shape=jax.ShapeDtypeStruct((n_pad, dh), jnp.int32),
        compiler_params=_cparams(("arbitrary",)),
        name="experts",
    )(block_e, block_nvalid, block_new, xs, wgu, bgu, wd, bd)


def _final_kernel(yg_ref, w_ref, x1_ref, g2_ref, lg_ref, lb_ref, o_ref):
    w = w_ref[...]
    acc_lo = None
    for j in range(TOP_K):
        lo, hi = _unpack_bf16_pairs(yg_ref[j])
        wj = w[:, j:j + 1]
        acc_lo = lo * wj if acc_lo is None else acc_lo + lo * wj
        acc_hi = hi * wj if j == 0 else acc_hi + hi * wj
    h = jnp.concatenate([acc_lo, acc_hi], axis=1)
    r = DEEPNORM_ALPHA * x1_ref[...] + g2_ref[...] * h
    o_ref[...] = _layer_norm(r, LN_EPS) * lg_ref[...] + lb_ref[...]


def _final(yg, w_tok, x1, g2, lg, lb, tm):
    bsz, s, d = x1.shape
    nt = s // tm
    row = pl.BlockSpec((None, tm, d), lambda b, i: (b, i, 0))
    full = pl.BlockSpec((1, d), lambda b, i: (0, 0))
    return pl.pallas_call(
        _final_kernel,
        grid=(bsz, nt),
        in_specs=[pl.BlockSpec((TOP_K, tm, d // 2), lambda b, i: (0, b * nt + i, 0)),
                  pl.BlockSpec((tm, TOP_K), lambda b, i: (b * nt + i, 0)),
                  row, pl.BlockSpec((None, 1, d), lambda b, i: (b, 0, 0)), full, full],
        out_specs=row,
        out_shape=jax.ShapeDtypeStruct((bsz, s, d), F32),
        compiler_params=_cparams(("parallel", "parallel")),
        name="final",
    )(yg, w_tok, x1, g2, lg, lb)


def _scatter_rows(rows, dest, m_out):
    n, w = rows.shape
    nj = dest.shape[0]
    ch = GATHER_CHUNK
    nw = SC_CORES * SC_SUBCORES
    nch = n // (nw * ch)
    assert n == nw * nch * ch and nch % 2 == 0
    per_w = nch * ch
    mesh = plsc.VectorSubcoreMesh(core_axis_name="c", subcore_axis_name="s",
                                  num_cores=SC_CORES, num_subcores=SC_SUBCORES)

    @functools.partial(
        pl.kernel, mesh=mesh,
        out_type=jax.ShapeDtypeStruct((m_out, w), rows.dtype),
        scratch_types=[pltpu.VMEM((nj, nch, ch), jnp.int32), pltpu.VMEM((2, ch, w), rows.dtype),
                       pltpu.SemaphoreType.DMA((2,)), pltpu.SemaphoreType.DMA((2,))],
    )
    def scatter_kernel(rows_hbm, dest_hbm, out_hbm, idx_v, rows_v, rsem, ssem):
        wid = lax.axis_index("s") * SC_CORES + lax.axis_index("c")
        base = wid * per_w
        for j in range(nj):
            pltpu.sync_copy(dest_hbm.at[j, wid], idx_v.at[j])

        def read(i, b):
            off = pl.multiple_of(base + i * ch, ch)
            return pltpu.make_async_copy(rows_hbm.at[pl.ds(off, ch)], rows_v.at[b], rsem.at[b])

        def scatter(i, b, j):
            return pltpu.make_async_copy(rows_v.at[b], out_hbm.at[idx_v.at[j, i]], ssem.at[b])

        read(0, 0).start()

        @pl.loop(0, nch, step=2)
        def _(i0):
            for b in range(2):
                i = i0 + b
                read(i, b).wait()

                @pl.when(i + 1 < nch)
                def _():
                    @pl.when(i >= 1)
                    def _():
                        for j in range(nj):
                            scatter(i - 1, 1 - b, j).wait()
                    read(i + 1, 1 - b).start()

                for j in range(nj):
                    scatter(i, b, j).start()

        for b in range(2):
            for j in range(nj):
                scatter(nch - 2 + b, b, j).wait()

    return scatter_kernel(rows, dest.reshape(nj, nw, nch, ch))


def _gather_rows(table, idx):
    m = idx.shape[0]
    w = table.shape[1]
    ch = GATHER_CHUNK
    nw = SC_CORES * SC_SUBCORES
    nch = m // (nw * ch)
    assert m == nw * nch * ch and nch % 2 == 0
    per_w = nch * ch
    mesh = plsc.VectorSubcoreMesh(core_axis_name="c", subcore_axis_name="s",
                                  num_cores=SC_CORES, num_subcores=SC_SUBCORES)

    @functools.partial(
        pl.kernel, mesh=mesh,
        out_type=jax.ShapeDtypeStruct((m, w), table.dtype),
        scratch_types=[pltpu.VMEM((nch, ch), jnp.int32), pltpu.VMEM((2, ch, w), table.dtype),
                       pltpu.SemaphoreType.DMA((2,)), pltpu.SemaphoreType.DMA((2,))],
    )
    def gather_kernel(table_hbm, idx_hbm, out_hbm, idx_v, rows_v, gsem, wsem):
        wid = lax.axis_index("s") * SC_CORES + lax.axis_index("c")
        base = wid * per_w
        pltpu.sync_copy(idx_hbm.at[wid], idx_v)

        def gather(j, b):
            return pltpu.make_async_copy(table_hbm.at[idx_v.at[j]], rows_v.at[b], gsem.at[b])

        def write(j, b):
            off = pl.multiple_of(base + j * ch, ch)
            return pltpu.make_async_copy(rows_v.at[b], out_hbm.at[pl.ds(off, ch)], wsem.at[b])

        gather(0, 0).start()

        @pl.loop(0, nch, step=2)
        def _(i):
            for b in range(2):
                j = i + b
                gather(j, b).wait()

                @pl.when(j + 1 < nch)
                def _():
                    @pl.when(j >= 1)
                    def _():
                        write(j - 1, 1 - b).wait()
                    gather(j + 1, 1 - b).start()

                write(j, b).start()

        write(nch - 2, 0).wait()
        write(nch - 1, 1).wait()

    return gather_kernel(table, idx.reshape(nw, nch, ch))


def kernel(x, c, w_ada, b_ada, w_in, q_norm_g, k_norm_g, w_fourier, w_attn_o, w_out, ln1_g, ln1_b,
           w_router, b_router, w_gate_up, b_gate_up, w_down, b_down, ln2_g, ln2_b):
    bsz, s, d = x.shape
    n = bsz * s
    tm = min(ROW_TILE, s)
    l = 0

    mod = _ada(c, w_ada[l], b_ada[l])
    sh1, sc1, g1, sh2, sc2, g2 = [m[:, None, :] for m in jnp.split(mod, N_MOD, axis=-1)]

    dft_c, m1, m2, cw, sw, p, q = _dft_tables(s)
    cos, sin_a, sin_b = _rope_tables(s)

    fa, fb, qh, kh, vh, gf, ga = _inproj(
        x, sc1, sh1, w_in[l].astype(BF16), dft_c,
        q_norm_g[l].reshape(1, HEAD_DIM), k_norm_g[l].reshape(1, HEAD_DIM), cos, sin_a, sin_b, tm)

    zr, zi = _fft1(fa, fb, m1, cw, sw, p, q)
    yfm = _fft2(zr, zi, m2)

    attn = _attention(qh, kh, vh, min(ATTN_TQ, s), min(ATTN_TK, s))

    x1, u2p, idx_t, wgt_t, rank_t, counts = _mix(
        yfm, attn, gf, ga, x, g1, sc2, sh2,
        w_fourier[l].astype(BF16), w_attn_o[l].astype(BF16), w_out[l].astype(BF16),
        ln1_g[l].reshape(1, d), ln1_b[l].reshape(1, d),
        w_router[l].T, b_router[l].reshape(N_EXPERTS, 1), tm)

    tmb = MOE_BLOCK
    counts = counts.reshape(N_EXPERTS)
    pcounts = ((counts + tmb - 1) // tmb) * tmb
    pends = jnp.cumsum(pcounts)
    pstarts = pends - pcounts
    n_pad = n * TOP_K + N_EXPERTS * tmb
    nblk = n_pad // tmb
    eids = jnp.arange(N_EXPERTS, dtype=jnp.int32)
    start_of = jnp.sum(jnp.where(idx_t[None] == eids[:, None, None], pstarts[:, None, None], 0), axis=0)
    dest_t = start_of + rank_t
    block_starts = jnp.arange(nblk, dtype=jnp.int32) * tmb
    block_e = jnp.sum((block_starts[:, None] >= pends[None, :]).astype(jnp.int32), axis=1)
    block_e = jnp.minimum(block_e, N_EXPERTS - 1)
    block_nvalid = jnp.clip(pstarts[block_e] + counts[block_e] - block_starts, 0, tmb)
    block_new = jnp.concatenate([jnp.ones((1,), jnp.int32),
                                 (block_e[1:] != block_e[:-1]).astype(jnp.int32)])

    xs = _scatter_rows(u2p.reshape(n, d // 2), dest_t, n_pad)
    ys = _experts(block_e, block_nvalid, block_new, xs, w_gate_up[l], b_gate_up[l][:, None, :],
                  w_down[l], b_down[l][:, None, :], tmb)
    yg = _gather_rows(ys, dest_t.reshape(-1)).reshape(TOP_K, n, d // 2)

    return _final(yg, wgt_t.T, x1, g2, ln2_g[l].reshape(1, d), ln2_b[l].reshape(1, d), tm)
```

```python
import functools
import math

import jax
import jax.numpy as jnp
import numpy as np
from jax import lax
from jax.experimental import pallas as pl
from jax.experimental.pallas import tpu as pltpu
from jax.experimental.pallas import tpu_sc as plsc

F32 = jnp.float32
BF16 = jnp.bfloat16

D_MODEL = 1024
GRID_W = 64
N_Q_HEADS = 8
N_KV_HEADS = 2
HEAD_DIM = 128
Q_GROUP = N_Q_HEADS // N_KV_HEADS
ROPE_THETA = 10000.0
ROPE_AXIS_DIM = HEAD_DIM // 2
N_FOURIER_GROUPS = 4
FOURIER_GROUP_DIM = 128
FOURIER_DIM = N_FOURIER_GROUPS * FOURIER_GROUP_DIM
Q_DIM = N_Q_HEADS * HEAD_DIM
KV_DIM = N_KV_HEADS * HEAD_DIM
N_EXPERTS = 32
TOP_K = 4
D_EXPERT = 1024
SWIGLU_LIMIT = 7.0
SWIGLU_ALPHA = 1.702
N_MOD = 6
DEPTH = 1
DEEPNORM_ALPHA = (2 * DEPTH) ** 0.25
LN_EPS = 1e-5
ADA_EPS = 1e-6
QK_EPS = 1e-6
LOG2E = 1.4426950408889634

V7X_VMEM_BYTES = 64 * 1024 * 1024
VMEM_LIMIT = V7X_VMEM_BYTES - 8 * 1024 * 1024
LANES = 128
BF16_SUBLANES = 16

FFT_Q = 128
ROW_TILE = 512
ATTN_TQ = 256
ONES_ROWS = 16
ATTN_UNROLL = 2
MIX_ROW_GROUPS = 2
MOE_BLOCK = 512
EXPERT_CHUNKS = 4

SC_CORES = 2
SC_SUBCORES = 16
GATHER_CHUNK = 64


def _cparams(sem):
    return pltpu.CompilerParams(dimension_semantics=sem, vmem_limit_bytes=VMEM_LIMIT)


def _layer_norm(x, eps):
    mu = jnp.mean(x, axis=-1, keepdims=True)
    xc = x - mu
    var = jnp.mean(xc * xc, axis=-1, keepdims=True)
    return xc * lax.rsqrt(var + eps)


def _pack_bf16_pairs(x):
    n = x.shape[1] // 2
    bits = pltpu.bitcast(x.astype(BF16).astype(F32), jnp.uint32)
    word = (bits[:, :n] >> 16) | (bits[:, n:] & jnp.uint32(0xFFFF0000))
    return pltpu.bitcast(word, jnp.int32)


def _unpack_bf16_pairs(w):
    bits = pltpu.bitcast(w, jnp.uint32)
    lo = pltpu.bitcast(bits << 16, F32)
    hi = pltpu.bitcast(bits & jnp.uint32(0xFFFF0000), F32)
    return lo, hi


def _ada_kernel(c_ref, w_ref, b_ref, o_ref):
    c = c_ref[...]
    cond = c * jax.nn.sigmoid(c)
    o_ref[...] = jnp.dot(cond, w_ref[...], preferred_element_type=F32,
                         precision=lax.Precision.HIGHEST) + b_ref[...]


def _ada(c, w_ada, b_ada):
    bsz, d = c.shape
    n = w_ada.shape[1]
    tn = 1536
    return pl.pallas_call(
        _ada_kernel,
        grid=(n // tn,),
        in_specs=[pl.BlockSpec((bsz, d), lambda j: (0, 0)),
                  pl.BlockSpec((d, tn), lambda j: (0, j)),
                  pl.BlockSpec((1, tn), lambda j: (0, j))],
        out_specs=pl.BlockSpec((bsz, tn), lambda j: (0, j)),
        out_shape=jax.ShapeDtypeStruct((bsz, n), F32),
        compiler_params=_cparams(("arbitrary",)),
        name="ada",
    )(c, w_ada, b_ada.reshape(1, n))


def _rope(xh, cos, sin_a, sin_b):
    a = pltpu.roll(xh, 32, 1)
    b = pltpu.roll(xh, 96, 1)
    return xh * cos + b * sin_b + a * sin_a


def _inproj_kernel(x_ref, sc_ref, sh_ref, w_ref, dft_ref, qg_ref, kg_ref,
                   cos_ref, sa_ref, sb_ref,
                   fa_ref, fb_ref, q_ref, k_ref, vt_ref, gf_ref, ga_ref):
    x = x_ref[...]
    u = _layer_norm(x, ADA_EPS) * (1.0 + sc_ref[...]) + sh_ref[...]
    ub = u.astype(BF16)
    cos, sa, sb = cos_ref[...], sa_ref[...], sb_ref[...]

    c0 = 0
    zf = jnp.dot(ub, w_ref[:, c0:c0 + FOURIER_DIM], preferred_element_type=F32).astype(BF16)
    for g in range(N_FOURIER_GROUPS):
        sl = slice(g * FOURIER_GROUP_DIM, (g + 1) * FOURIER_GROUP_DIM)
        ab = jnp.dot(zf[:, sl], dft_ref[...], preferred_element_type=F32)
        fa_ref[:, sl] = ab[:, :FOURIER_GROUP_DIM].astype(BF16)
        fb_ref[:, sl] = ab[:, FOURIER_GROUP_DIM:].astype(BF16)
    c0 += FOURIER_DIM

    def norm_rope(z, gain, scale):
        ms = jnp.mean(z * z, axis=-1, keepdims=True)
        zn = z * lax.rsqrt(ms + QK_EPS) * gain
        return (_rope(zn, cos, sa, sb) * scale).astype(BF16)

    q_scale = LOG2E / math.sqrt(HEAD_DIM)
    zq = jnp.dot(ub, w_ref[:, c0:c0 + Q_DIM], preferred_element_type=F32)
    for h in range(N_Q_HEADS):
        sl = slice(h * HEAD_DIM, (h + 1) * HEAD_DIM)
        q_ref[:, sl] = norm_rope(zq[:, sl], qg_ref[...], q_scale)
    c0 += Q_DIM

    zk = jnp.dot(ub, w_ref[:, c0:c0 + KV_DIM], preferred_element_type=F32)
    for h in range(N_KV_HEADS):
        sl = slice(h * HEAD_DIM, (h + 1) * HEAD_DIM)
        k_ref[:, sl] = norm_rope(zk[:, sl], kg_ref[...], 1.0)
    c0 += KV_DIM

    zv = jnp.dot(ub, w_ref[:, c0:c0 + KV_DIM], preferred_element_type=F32)
    for h in range(N_KV_HEADS):
        vt_ref[h, :HEAD_DIM, :] = zv[:, h * HEAD_DIM:(h + 1) * HEAD_DIM].T.astype(BF16)
        vt_ref[h, HEAD_DIM:, :] = jnp.ones((ONES_ROWS, zv.shape[0]), BF16)
    c0 += KV_DIM

    zg = jnp.dot(ub, w_ref[:, c0:c0 + D_MODEL], preferred_element_type=F32)
    gf_ref[...] = jax.nn.sigmoid(zg).astype(BF16)
    c0 += D_MODEL
    zg = jnp.dot(ub, w_ref[:, c0:c0 + D_MODEL], preferred_element_type=F32)
    ga_ref[...] = jax.nn.sigmoid(zg).astype(BF16)


def _inproj(x, sc1, sh1, w_in, dft_c, qg, kg, cos, sin_a, sin_b, tm):
    bsz, s, d = x.shape
    ncol = w_in.shape[1]
    row = lambda w: pl.BlockSpec((None, tm, w), lambda b, i: (b, i, 0))
    mod = pl.BlockSpec((None, 1, d), lambda b, i: (b, 0, 0))
    full = lambda shp: pl.BlockSpec(shp, lambda b, i: (0,) * len(shp))
    tab = pl.BlockSpec((tm, HEAD_DIM), lambda b, i: (i, 0))
    vt_rows = HEAD_DIM + ONES_ROWS
    vt_spec = pl.BlockSpec((None, N_KV_HEADS, None, vt_rows, tm), lambda b, i: (b, 0, i, 0, 0))
    vt_shape = jax.ShapeDtypeStruct((bsz, N_KV_HEADS, s // tm, vt_rows, tm), BF16)
    rows_out = lambda w: (row(w), jax.ShapeDtypeStruct((bsz, s, w), BF16))
    outs = [rows_out(FOURIER_DIM), rows_out(FOURIER_DIM), rows_out(Q_DIM), rows_out(KV_DIM),
            (vt_spec, vt_shape), rows_out(D_MODEL), rows_out(D_MODEL)]
    return pl.pallas_call(
        _inproj_kernel,
        grid=(bsz, s // tm),
        in_specs=[row(d), mod, mod, full((d, ncol)), full(dft_c.shape),
                  full((1, HEAD_DIM)), full((1, HEAD_DIM)), tab, tab, tab],
        out_specs=[o[0] for o in outs],
        out_shape=[o[1] for o in outs],
        compiler_params=_cparams(("parallel", "parallel")),
        name="inproj",
    )(x, sc1, sh1, w_in, dft_c, qg, kg, cos, sin_a, sin_b)


def _fft1_kernel(a_ref, b_ref, m_ref, cw_ref, sw_ref, zr_ref, zi_ref, a32, b32):
    p, nb, _ = a_ref.shape
    a32[...] = a_ref[...].astype(F32)
    b32[...] = b_ref[...].astype(F32)
    for j in range(nb):
        ab = jnp.concatenate([a32[:, j, :], b32[:, j, :]], axis=0).astype(BF16)
        t = jnp.dot(m_ref[...], ab, preferred_element_type=F32)
        tr, ti = t[:p], t[p:]
        cw, sw = cw_ref[j], sw_ref[j]
        zr_ref[j] = (tr * cw - ti * sw).astype(BF16)
        zi_ref[j] = (ti * cw + tr * sw).astype(BF16)


def _fft1(fa, fb, m1, cw, sw, p, q):
    bsz, s, c = fa.shape
    nb = BF16_SUBLANES
    inp = pl.BlockSpec((None, p, nb, c), lambda b, j: (b, 0, j, 0))
    tw = pl.BlockSpec((nb, p, 1), lambda b, j: (j, 0, 0))
    out = pl.BlockSpec((None, nb, p, c), lambda b, j: (b, j, 0, 0))
    return pl.pallas_call(
        _fft1_kernel,
        grid=(bsz, q // nb),
        in_specs=[inp, inp, pl.BlockSpec(m1.shape, lambda b, j: (0, 0)), tw, tw],
        out_specs=[out, out],
        out_shape=[jax.ShapeDtypeStruct((bsz, q, p, c), BF16)] * 2,
        scratch_shapes=[pltpu.VMEM((p, nb, c), F32)] * 2,
        compiler_params=_cparams(("parallel", "parallel")),
        name="fft1",
    )(fa.reshape(bsz, p, q, c), fb.reshape(bsz, p, q, c), m1, cw, sw)


def _fft2_kernel(zr_ref, zi_ref, m_ref, y_ref, zr32, zi32, y32):
    zr32[...] = zr_ref[...].astype(F32)
    zi32[...] = zi_ref[...].astype(F32)
    for j in range(zr_ref.shape[1]):
        z = jnp.concatenate([zr32[:, j, :], zi32[:, j, :]], axis=0).astype(BF16)
        y32[:, j, :] = jnp.dot(m_ref[...], z, preferred_element_type=F32)
    y_ref[...] = y32[...].astype(BF16)


def _fft2(zr, zi, m2):
    bsz, q, p, c = zr.shape
    pc = BF16_SUBLANES
    blk = pl.BlockSpec((None, q, pc, c), lambda b, j: (b, 0, j, 0))
    y = pl.pallas_call(
        _fft2_kernel,
        grid=(bsz, p // pc),
        in_specs=[blk, blk, pl.BlockSpec(m2.shape, lambda b, j: (0, 0))],
        out_specs=blk,
        out_shape=jax.ShapeDtypeStruct((bsz, q, p, c), BF16),
        scratch_shapes=[pltpu.VMEM((q, pc, c), F32)] * 3,
        compiler_params=_cparams(("parallel", "parallel")),
        name="fft2",
    )(zr, zi, m2)
    return y.reshape(bsz, q * p, c)


def _dft_tables(s):
    q = FFT_Q
    p = s // q

    def cs(n_rows, n_cols, n):
        r = np.arange(n_rows, dtype=np.int64)[:, None]
        c = np.arange(n_cols, dtype=np.int64)[None, :]
        ang = ((r * c) % n).astype(np.float64) * (2.0 * math.pi / n)
        return np.cos(ang).astype(np.float32), np.sin(ang).astype(np.float32)

    cc, sc = cs(FOURIER_GROUP_DIM, FOURIER_GROUP_DIM, FOURIER_GROUP_DIM)
    dft_c = jnp.asarray(np.concatenate([cc, sc], axis=1), dtype=BF16)
    cp, sp = cs(p, p, p)
    m1 = jnp.asarray(np.block([[cp, -sp], [sp, cp]]), dtype=BF16)
    cq, sq = cs(q, q, q)
    scale = np.float32(1.0 / math.sqrt(s * FOURIER_GROUP_DIM))
    m2 = jnp.asarray(np.concatenate([cq, -sq], axis=1) * scale, dtype=BF16)
    cw, sw = cs(q, p, s)
    return dft_c, m1, m2, jnp.asarray(cw[:, :, None]), jnp.asarray(sw[:, :, None]), p, q


def _rope_tables(s):
    f32 = np.float32
    rows = s // GRID_W
    row_ids = np.repeat(np.arange(rows, dtype=f32), GRID_W)
    col_ids = np.tile(np.arange(GRID_W, dtype=f32), rows)
    freqs = (f32(ROPE_THETA) ** (-np.arange(0, ROPE_AXIS_DIM, 2, dtype=f32) / f32(ROPE_AXIS_DIM))).astype(f32)
    ang_r = (row_ids[:, None] * freqs).astype(f32)
    ang_c = (col_ids[:, None] * freqs).astype(f32)
    cr, sr, cc, sc = np.cos(ang_r), np.sin(ang_r), np.cos(ang_c), np.sin(ang_c)
    z = np.zeros_like(sr)
    cos = np.concatenate([cr, cr, cc, cc], axis=1)
    sin_b = np.concatenate([-sr, z, -sc, z], axis=1)
    sin_a = np.concatenate([z, sr, z, sc], axis=1)
    return jnp.asarray(cos, F32), jnp.asarray(sin_a, F32), jnp.asarray(sin_b, F32)


def _attn_kernel(q_ref, k_ref, vt_ref, o_ref, m_ref, acc_ref, al0, al1, s0, s1, p0, p1, qt_ref, mx0, mx1):
    al_ref, s_ref, p_ref, mx_ref = (al0, al1), (s0, s1), (p0, p1), (mx0, mx1)
    nchunk = vt_ref.shape[0]
    tk = vt_ref.shape[2]

    heads = range(Q_GROUP)
    for g in heads:
        qt_ref[g] = q_ref[:, g * HEAD_DIM:(g + 1) * HEAD_DIM].astype(F32).T.astype(BF16)

    def scores_into(c, slot, hs=heads):
        off = pl.multiple_of(c * tk, tk)
        kc = k_ref[pl.ds(off, tk), :]
        for g in hs:
            sc = jnp.dot(kc, qt_ref[g], preferred_element_type=F32)
            s_ref[slot][g] = sc
            parts = [jnp.max(sc[r * (tk // 4):(r + 1) * (tk // 4)], axis=0, keepdims=True) for r in range(4)]
            mx_ref[slot][g] = jnp.maximum(jnp.maximum(parts[0], parts[1]), jnp.maximum(parts[2], parts[3]))

    def softmax(slot, hs=heads):
        for g in hs:
            sc = s_ref[slot][g]
            m_old = m_ref[g]
            m_new = jnp.maximum(m_old, mx_ref[slot][g])
            al_ref[slot][g] = jnp.exp2(m_old - m_new)
            m_ref[g] = m_new
            p_ref[slot][g] = jnp.exp2(sc - m_new).astype(BF16)

    def pv_update(c, slot, hs=heads):
        vc = vt_ref[c]
        for g in hs:
            acc_ref[g] = al_ref[slot][g] * acc_ref[g] + jnp.dot(vc, p_ref[slot][g],
                                                                preferred_element_type=F32)

    def step(c, slot, prefetch_scores=True):
        if prefetch_scores:
            scores_into(c + 1, 1 - slot)
        pv_update(c - 1, 1 - slot)
        softmax(slot)

    m_ref[...] = jnp.full(m_ref.shape, -jnp.inf, F32)
    acc_ref[...] = jnp.zeros(acc_ref.shape, F32)
    scores_into(0, 0)
    softmax(0)
    scores_into(1, 1)

    nloop = (nchunk - 2) // ATTN_UNROLL

    def trip(i, carry):
        for j in range(ATTN_UNROLL):
            step(ATTN_UNROLL * i + 1 + j, (1 + j) % 2)
        return carry

    lax.fori_loop(0, nloop, trip, 0)
    for c in range(nloop * ATTN_UNROLL + 1, nchunk - 1):
        step(c, c % 2)
    last = nchunk - 1
    step(last, last % 2, prefetch_scores=False)
    pv_update(last, last % 2)
    for g in range(Q_GROUP):
        acc = acc_ref[g]
        out_t = acc[:HEAD_DIM] / acc[HEAD_DIM:HEAD_DIM + 1]
        o_ref[:, g * HEAD_DIM:(g + 1) * HEAD_DIM] = out_t.T.astype(BF16)


def _attention(q, k, vt, tq):
    bsz, s, _ = q.shape
    rows, tk = vt.shape[3], vt.shape[4]
    assert (s // tk) % 2 == 0
    gw = Q_GROUP * HEAD_DIM
    qspec = pl.BlockSpec((None, tq, gw), lambda b, h, i: (b, i, h))
    kspec = pl.BlockSpec((None, s, HEAD_DIM), lambda b, h, i: (b, 0, h))
    vspec = pl.BlockSpec((None, None, s // tk, rows, tk), lambda b, h, i: (b, h, 0, 0, 0))
    return pl.pallas_call(
        _attn_kernel,
        grid=(bsz, N_KV_HEADS, s // tq),
        in_specs=[qspec, kspec, vspec],
        out_specs=qspec,
        out_shape=jax.ShapeDtypeStruct((bsz, s, Q_DIM), BF16),
        scratch_shapes=[pltpu.VMEM((Q_GROUP, 1, tq), F32), pltpu.VMEM((Q_GROUP, rows, tq), F32)]
        + [pltpu.VMEM((Q_GROUP, 1, tq), F32)] * 2 + [pltpu.VMEM((Q_GROUP, tk, tq), F32)] * 2
        + [pltpu.VMEM((Q_GROUP, tk, tq), BF16)] * 2 + [pltpu.VMEM((Q_GROUP, HEAD_DIM, tq), BF16)]
        + [pltpu.VMEM((Q_GROUP, 1, tq), F32)] * 2,
        compiler_params=_cparams(("parallel", "parallel", "parallel")),
        name="attn",
    )(q, k, vt)


def _mix_kernel(yf_ref, at_ref, gf_ref, ga_ref, x_ref, g1_ref, sc2_ref, sh2_ref,
                wf_ref, wo_ref, wout_ref, l1g_ref, l1b_ref, wr_ref, br_ref, tri_ref,
                x1_ref, u2_ref, idx_ref, wgt_ref, rank_ref, cnt_ref, cnt_acc):
    first = jnp.logical_and(pl.program_id(0) == 0, pl.program_id(1) == 0)

    @pl.when(first)
    def _():
        cnt_acc[...] = jnp.zeros_like(cnt_acc)

    tm_all = x_ref.shape[0]
    rg = tm_all // MIX_ROW_GROUPS
    u_his, u_los = [], []
    groups = [slice(i * rg, (i + 1) * rg) for i in range(MIX_ROW_GROUPS)]
    ms = []
    for rows in groups:
        yf = jnp.dot(yf_ref[rows, :], wf_ref[...], preferred_element_type=F32)
        ya = jnp.dot(at_ref[rows, :], wo_ref[...], preferred_element_type=F32)
        ms.append((gf_ref[rows, :].astype(F32) * yf + ga_ref[rows, :].astype(F32) * ya).astype(BF16))
    hs = [jnp.dot(m, wout_ref[...], preferred_element_type=F32) for m in ms]
    for rows, h in zip(groups, hs):
        r = DEEPNORM_ALPHA * x_ref[rows, :] + g1_ref[...] * h
        x1 = _layer_norm(r, LN_EPS) * l1g_ref[...] + l1b_ref[...]
        x1_ref[rows, :] = x1
        u2 = _layer_norm(x1, ADA_EPS) * (1.0 + sc2_ref[...]) + sh2_ref[...]
        u2_ref[rows, :] = _pack_bf16_pairs(u2)
        u_hi = u2.astype(BF16)
        u_his.append(u_hi)
        u_los.append((u2 - u_hi.astype(F32)).astype(BF16))

    nt = (((1,), (1,)), ((), ()))
    u_hi = jnp.concatenate(u_his, axis=0)
    u_lo = jnp.concatenate(u_los, axis=0)
    w = wr_ref[...]
    w_hi = w.astype(BF16)
    w_lo = (w - w_hi.astype(F32)).astype(BF16)
    logits = (lax.dot_general(w_hi, u_hi, nt, preferred_element_type=F32)
              + lax.dot_general(w_lo, u_hi, nt, preferred_element_type=F32)
              + lax.dot_general(w_hi, u_lo, nt, preferred_element_type=F32)) + br_ref[...]

    ne, tm = logits.shape
    eid = lax.broadcasted_iota(jnp.int32, (ne, tm), 0)
    work = logits
    vals, idxs, hots = [], [], []
    for _ in range(TOP_K):
        mx = jnp.max(work, axis=0, keepdims=True)
        ix = jnp.min(jnp.where(work == mx, eid, ne), axis=0, keepdims=True)
        hot = eid == ix
        work = jnp.where(hot, -jnp.inf, work)
        vals.append(mx)
        idxs.append(ix)
        hots.append(hot)
    ex = [jnp.exp(v - vals[0]) for v in vals]
    den = ex[0] + ex[1] + ex[2] + ex[3]
    wgt_ref[...] = jnp.concatenate([e / den for e in ex], axis=0)
    idx_ref[...] = jnp.concatenate(idxs, axis=0)

    sel = hots[0] | hots[1] | hots[2] | hots[3]
    mask = jnp.where(sel, 1.0, 0.0)
    prefix = jnp.dot(mask.astype(BF16), tri_ref[...], preferred_element_type=F32)
    pos = prefix + cnt_acc[...]
    ranks = [jnp.sum(jnp.where(hot, pos, 0.0), axis=0, keepdims=True) for hot in hots]
    rank_ref[...] = jnp.concatenate(ranks, axis=0).astype(jnp.int32)
    cnt_acc[...] += jnp.sum(mask, axis=1, keepdims=True)
    cnt_ref[...] = cnt_acc[...].astype(jnp.int32)


def _mix(yfm, attn, gf, ga, x, g1, sc2, sh2, wf, wo, wout, l1g, l1b, wr_t, br, tm):
    bsz, s, d = x.shape
    n = bsz * s
    nt = s // tm
    tri = (jnp.arange(tm)[:, None] < jnp.arange(tm)[None, :]).astype(BF16)
    row = lambda w: pl.BlockSpec((None, tm, w), lambda b, i: (b, i, 0))
    mod = pl.BlockSpec((None, 1, d), lambda b, i: (b, 0, 0))
    full = lambda shp: pl.BlockSpec(shp, lambda b, i: (0,) * len(shp))
    tok = pl.BlockSpec((TOP_K, tm), lambda b, i: (0, b * nt + i))
    return pl.pallas_call(
        _mix_kernel,
        grid=(bsz, nt),
        in_specs=[row(FOURIER_DIM), row(Q_DIM), row(d), row(d), row(d), mod, mod, mod,
                  full(wf.shape), full(wo.shape), full(wout.shape), full((1, d)), full((1, d)),
                  full(wr_t.shape), full((N_EXPERTS, 1)), full((tm, tm))],
        out_specs=[row(d), row(d // 2), tok, tok, tok, full((N_EXPERTS, 1))],
        out_shape=[jax.ShapeDtypeStruct((bsz, s, d), F32),
                   jax.ShapeDtypeStruct((bsz, s, d // 2), jnp.int32),
                   jax.ShapeDtypeStruct((TOP_K, n), jnp.int32),
                   jax.ShapeDtypeStruct((TOP_K, n), F32),
                   jax.ShapeDtypeStruct((TOP_K, n), jnp.int32),
                   jax.ShapeDtypeStruct((N_EXPERTS, 1), jnp.int32)],
        scratch_shapes=[pltpu.VMEM((N_EXPERTS, 1), F32)],
        compiler_params=_cparams(("arbitrary", "arbitrary")),
        name="mix",
    )(yfm, attn, gf, ga, x, g1, sc2, sh2, wf, wo, wout, l1g, l1b, wr_t, br, tri)


def _expert_kernel(be_ref, nv_ref, new_ref, xs_ref, wgu_ref, bgu_ref, wd_ref, bd_ref, ys_ref,
                   wgu_bf, wd_bf):
    del be_ref
    i = pl.program_id(0)
    nvalid = nv_ref[i]

    @pl.when(new_ref[i] == 1)
    def _():
        wgu_bf[...] = wgu_ref[...].astype(BF16)
        wd_bf[...] = wd_ref[...].astype(BF16)

    @pl.when(nvalid > 0)
    def _():
        xw = xs_ref[...]
        row = lax.broadcasted_iota(jnp.int32, xw.shape, 0)
        lo, hi = _unpack_bf16_pairs(jnp.where(row < nvalid, xw, 0))
        xb = jnp.concatenate([lo.astype(BF16), hi.astype(BF16)], axis=1)

        cw = D_EXPERT // EXPERT_CHUNKS

        def gate_up(j):
            cg = slice(j * cw, (j + 1) * cw)
            cu = slice(D_EXPERT + j * cw, D_EXPERT + (j + 1) * cw)
            g = jnp.dot(xb, wgu_bf[:, cg], preferred_element_type=F32) + bgu_ref[:, cg]
            u = jnp.dot(xb, wgu_bf[:, cu], preferred_element_type=F32) + bgu_ref[:, cu]
            return g, u

        def down(j, g, u):
            gate = jnp.minimum(g, SWIGLU_LIMIT)
            up = jnp.clip(u, -SWIGLU_LIMIT, SWIGLU_LIMIT)
            act = (up + 1.0) * gate * jax.nn.sigmoid(SWIGLU_ALPHA * gate)
            return jnp.dot(act.astype(BF16), wd_bf[j * cw:(j + 1) * cw, :], preferred_element_type=F32)

        y = bd_ref[...]
        gu = gate_up(0)
        for j in range(EXPERT_CHUNKS):
            nxt = gate_up(j + 1) if j + 1 < EXPERT_CHUNKS else None
            y = y + down(j, *gu)
            gu = nxt
        ys_ref[...] = _pack_bf16_pairs(y)

    @pl.when(nvalid == 0)
    def _():
        ys_ref[...] = jnp.zeros_like(ys_ref)


def _experts(block_e, block_nvalid, block_new, xs, wgu, bgu, wd, bd, tmb):
    n_pad, dh = xs.shape
    d = 2 * dh
    grid_spec = pltpu.PrefetchScalarGridSpec(
        num_scalar_prefetch=3,
        grid=(n_pad // tmb,),
        in_specs=[pl.BlockSpec((tmb, dh), lambda i, be, nv, nw: (i, 0)),
                  pl.BlockSpec((None, d, 2 * D_EXPERT), lambda i, be, nv, nw: (be[i], 0, 0)),
                  pl.BlockSpec((None, 1, 2 * D_EXPERT), lambda i, be, nv, nw: (be[i], 0, 0)),
                  pl.BlockSpec((None, D_EXPERT, d), lambda i, be, nv, nw: (be[i], 0, 0)),
                  pl.BlockSpec((None, 1, d), lambda i, be, nv, nw: (be[i], 0, 0))],
        out_specs=pl.BlockSpec((tmb, dh), lambda i, be, nv, nw: (i, 0)),
        scratch_shapes=[pltpu.VMEM((d, 2 * D_EXPERT), BF16), pltpu.VMEM((D_EXPERT, d), BF16)],
    )
    return pl.pallas_call(
        _expert_kernel,
        grid_spec=grid_spec,
        out_shape=jax.ShapeDtypeStruct((n_pad, dh), jnp.int32),
        compiler_params=_cparams(("arbitrary",)),
        name="experts",
    )(block_e, block_nvalid, block_new, xs, wgu, bgu, wd, bd)


def _final_kernel(yg_ref, w_ref, x1_ref, g2_ref, lg_ref, lb_ref, o_ref):
    w = w_ref[...]
    acc_lo = None
    for j in range(TOP_K):
        lo, hi = _unpack_bf16_pairs(yg_ref[j])
        wj = w[:, j:j + 1]
        acc_lo = lo * wj if acc_lo is None else acc_lo + lo * wj
        acc_hi = hi * wj if j == 0 else acc_hi + hi * wj
    h = jnp.concatenate([acc_lo, acc_hi], axis=1)
    r = DEEPNORM_ALPHA * x1_ref[...] + g2_ref[...] * h
    o_ref[...] = _layer_norm(r, LN_EPS) * lg_ref[...] + lb_ref[...]


def _final(yg, w_tok, x1, g2, lg, lb, tm):
    bsz, s, d = x1.shape
    nt = s // tm
    row = pl.BlockSpec((None, tm, d), lambda b, i: (b, i, 0))
    full = pl.BlockSpec((1, d), lambda b, i: (0, 0))
    return pl.pallas_call(
        _final_kernel,
        grid=(bsz, nt),
        in_specs=[pl.BlockSpec((TOP_K, tm, d // 2), lambda b, i: (0, b * nt + i, 0)),
                  pl.BlockSpec((tm, TOP_K), lambda b, i: (b * nt + i, 0)),
                  row, pl.BlockSpec((None, 1, d), lambda b, i: (b, 0, 0)), full, full],
        out_specs=row,
        out_shape=jax.ShapeDtypeStruct((bsz, s, d), F32),
        compiler_params=_cparams(("parallel", "parallel")),
        name="final",
    )(yg, w_tok, x1, g2, lg, lb)


def _scatter_rows(rows, dest, m_out):
    n, w = rows.shape
    nj = dest.shape[0]
    ch = GATHER_CHUNK
    nw = SC_CORES * SC_SUBCORES
    nch = n // (nw * ch)
    assert n == nw * nch * ch and nch % 2 == 0
    per_w = nch * ch
    mesh = plsc.VectorSubcoreMesh(core_axis_name="c", subcore_axis_name="s",
                                  num_cores=SC_CORES, num_subcores=SC_SUBCORES)

    @functools.partial(
        pl.kernel, mesh=mesh,
        out_type=jax.ShapeDtypeStruct((m_out, w), rows.dtype),
        scratch_types=[pltpu.VMEM((nj, nch, ch), jnp.int32), pltpu.VMEM((2, ch, w), rows.dtype),
                       pltpu.SemaphoreType.DMA((2,)), pltpu.SemaphoreType.DMA((2,))],
    )
    def scatter_kernel(rows_hbm, dest_hbm, out_hbm, idx_v, rows_v, rsem, ssem):
        wid = lax.axis_index("s") * SC_CORES + lax.axis_index("c")
        base = wid * per_w
        for j in range(nj):
            pltpu.sync_copy(dest_hbm.at[j, wid], idx_v.at[j])

        def read(i, b):
            off = pl.multiple_of(base + i * ch, ch)
            return pltpu.make_async_copy(rows_hbm.at[pl.ds(off, ch)], rows_v.at[b], rsem.at[b])

        def scatter(i, b, j):
            return pltpu.make_async_copy(rows_v.at[b], out_hbm.at[idx_v.at[j, i]], ssem.at[b])

        read(0, 0).start()

        @pl.loop(0, nch, step=2)
        def _(i0):
            for b in range(2):
                i = i0 + b
                read(i, b).wait()

                @pl.when(i + 1 < nch)
                def _():
                    @pl.when(i >= 1)
                    def _():
                        for j in range(nj):
                            scatter(i - 1, 1 - b, j).wait()
                    read(i + 1, 1 - b).start()

                for j in range(nj):
                    scatter(i, b, j).start()

        for b in range(2):
            for j in range(nj):
                scatter(nch - 2 + b, b, j).wait()

    return scatter_kernel(rows, dest.reshape(nj, nw, nch, ch))


def _gather_rows(table, idx):
    m = idx.shape[0]
    w = table.shape[1]
    ch = GATHER_CHUNK
    nw = SC_CORES * SC_SUBCORES
    nch = m // (nw * ch)
    assert m == nw * nch * ch and nch % 2 == 0
    per_w = nch * ch
    mesh = plsc.VectorSubcoreMesh(core_axis_name="c", subcore_axis_name="s",
                                  num_cores=SC_CORES, num_subcores=SC_SUBCORES)

    @functools.partial(
        pl.kernel, mesh=mesh,
        out_type=jax.ShapeDtypeStruct((m, w), table.dtype),
        scratch_types=[pltpu.VMEM((nch, ch), jnp.int32), pltpu.VMEM((2, ch, w), table.dtype),
                       pltpu.SemaphoreType.DMA((2,)), pltpu.SemaphoreType.DMA((2,))],
    )
    def gather_kernel(table_hbm, idx_hbm, out_hbm, idx_v, rows_v, gsem, wsem):
        wid = lax.axis_index("s") * SC_CORES + lax.axis_index("c")
        base = wid * per_w
        pltpu.sync_copy(idx_hbm.at[wid], idx_v)

        def gather(j, b):
            return pltpu.make_async_copy(table_hbm.at[idx_v.at[j]], rows_v.at[b], gsem.at[b])

        def write(j, b):
            off = pl.multiple_of(base + j * ch, ch)
            return pltpu.make_async_copy(rows_v.at[b], out_hbm.at[pl.ds(off, ch)], wsem.at[b])

        gather(0, 0).start()

        @pl.loop(0, nch, step=2)
        def _(i):
            for b in range(2):
                j = i + b
                gather(j, b).wait()

                @pl.when(j + 1 < nch)
                def _():
                    @pl.when(j >= 1)
                    def _():
                        write(j - 1, 1 - b).wait()
                    gather(j + 1, 1 - b).start()

                write(j, b).start()

        write(nch - 2, 0).wait()
        write(nch - 1, 1).wait()

    return gather_kernel(table, idx.reshape(nw, nch, ch))


def kernel(x, c, w_ada, b_ada, w_in, q_norm_g, k_norm_g, w_fourier, w_attn_o, w_out, ln1_g, ln1_b,
           w_router, b_router, w_gate_up, b_gate_up, w_down, b_down, ln2_g, ln2_b):
    bsz, s, d = x.shape
    n = bsz * s
    tm = min(ROW_TILE, s)
    l = 0

    mod = _ada(c, w_ada[l], b_ada[l])
    sh1, sc1, g1, sh2, sc2, g2 = [m[:, None, :] for m in jnp.split(mod, N_MOD, axis=-1)]

    dft_c, m1, m2, cw, sw, p, q = _dft_tables(s)
    cos, sin_a, sin_b = _rope_tables(s)

    fa, fb, qh, kh, vt, gf, ga = _inproj(
        x, sc1, sh1, w_in[l].astype(BF16), dft_c,
        q_norm_g[l].reshape(1, HEAD_DIM), k_norm_g[l].reshape(1, HEAD_DIM), cos, sin_a, sin_b, tm)

    zr, zi = _fft1(fa, fb, m1, cw, sw, p, q)
    yfm = _fft2(zr, zi, m2)

    attn = _attention(qh, kh, vt, min(ATTN_TQ, s))

    x1, u2p, idx_t, wgt_t, rank_t, counts = _mix(
        yfm, attn, gf, ga, x, g1, sc2, sh2,
        w_fourier[l].astype(BF16), w_attn_o[l].astype(BF16), w_out[l].astype(BF16),
        ln1_g[l].reshape(1, d), ln1_b[l].reshape(1, d),
        w_router[l].T, b_router[l].reshape(N_EXPERTS, 1), tm)

    tmb = MOE_BLOCK
    counts = counts.reshape(N_EXPERTS)
    pcounts = ((counts + tmb - 1) // tmb) * tmb
    pends = jnp.cumsum(pcounts)
    pstarts = pends - pcounts
    n_pad = n * TOP_K + N_EXPERTS * tmb
    nblk = n_pad // tmb
    eids = jnp.arange(N_EXPERTS, dtype=jnp.int32)
    start_of = jnp.sum(jnp.where(idx_t[None] == eids[:, None, None], pstarts[:, None, None], 0), axis=0)
    dest_t = start_of + rank_t
    block_starts = jnp.arange(nblk, dtype=jnp.int32) * tmb
    block_e = jnp.sum((block_starts[:, None] >= pends[None, :]).astype(jnp.int32), axis=1)
    block_e = jnp.minimum(block_e, N_EXPERTS - 1)
    block_nvalid = jnp.clip(pstarts[block_e] + counts[block_e] - block_starts, 0, tmb)
    block_new = jnp.concatenate([jnp.ones((1,), jnp.int32),
                                 (block_e[1:] != block_e[:-1]).astype(jnp.int32)])

    xs = _scatter_rows(u2p.reshape(n, d // 2), dest_t, n_pad)
    ys = _experts(block_e, block_nvalid, block_new, xs, w_gate_up[l], b_gate_up[l][:, None, :],
                  w_down[l], b_down[l][:, None, :], tmb)
    yg = _gather_rows(ys, dest_t.reshape(-1)).reshape(TOP_K, n, d // 2)

    return _final(yg, wgt_t.T, x1, g2, ln2_g[l].reshape(1, d), ln2_b[l].reshape(1, d), tm)
```

```python
import functools
import math

import jax
import jax.numpy as jnp
import numpy as np
from jax import lax
from jax.experimental import pallas as pl
from jax.experimental.pallas import tpu as pltpu
from jax.experimental.pallas import tpu_sc as plsc

F32 = jnp.float32
BF16 = jnp.bfloat16

D_MODEL = 1024
GRID_W = 64
N_Q_HEADS = 8
N_KV_HEADS = 2
HEAD_DIM = 128
Q_GROUP = N_Q_HEADS // N_KV_HEADS
ROPE_THETA = 10000.0
ROPE_AXIS_DIM = HEAD_DIM // 2
N_FOURIER_GROUPS = 4
FOURIER_GROUP_DIM = 128
FOURIER_DIM = N_FOURIER_GROUPS * FOURIER_GROUP_DIM
Q_DIM = N_Q_HEADS * HEAD_DIM
KV_DIM = N_KV_HEADS * HEAD_DIM
N_EXPERTS = 32
TOP_K = 4
D_EXPERT = 1024
SWIGLU_LIMIT = 7.0
SWIGLU_ALPHA = 1.702
N_MOD = 6
DEPTH = 1
DEEPNORM_ALPHA = (2 * DEPTH) ** 0.25
LN_EPS = 1e-5
ADA_EPS = 1e-6
QK_EPS = 1e-6
LOG2E = 1.4426950408889634

V7X_VMEM_BYTES = 64 * 1024 * 1024
VMEM_LIMIT = V7X_VMEM_BYTES - 8 * 1024 * 1024
LANES = 128
BF16_SUBLANES = 16

FFT_Q = 128
ROW_TILE = 512
ATTN_TQ = 256
ONES_ROWS = 16
MIX_ROW_GROUPS = 2
MOE_BLOCK = 512
MOE_BATCH_GROUPS = 2

SC_CORES = 2
SC_SUBCORES = 16
GATHER_CHUNK = 64


def _cparams(sem):
    return pltpu.CompilerParams(dimension_semantics=sem, vmem_limit_bytes=VMEM_LIMIT)


def _layer_norm(x, eps):
    mu = jnp.mean(x, axis=-1, keepdims=True)
    xc = x - mu
    var = jnp.mean(xc * xc, axis=-1, keepdims=True)
    return xc * lax.rsqrt(var + eps)


def _pack_bf16_pairs(x):
    n = x.shape[1] // 2
    bits = pltpu.bitcast(x.astype(BF16).astype(F32), jnp.uint32)
    word = (bits[:, :n] >> 16) | (bits[:, n:] & jnp.uint32(0xFFFF0000))
    return pltpu.bitcast(word, jnp.int32)


def _unpack_bf16_pairs(w):
    bits = pltpu.bitcast(w, jnp.uint32)
    lo = pltpu.bitcast(bits << 16, F32)
    hi = pltpu.bitcast(bits & jnp.uint32(0xFFFF0000), F32)
    return lo, hi


def _ada_kernel(c_ref, w_ref, b_ref, o_ref):
    c = c_ref[...]
    cond = c * jax.nn.sigmoid(c)
    o_ref[...] = jnp.dot(cond, w_ref[...], preferred_element_type=F32,
                         precision=lax.Precision.HIGHEST) + b_ref[...]


def _ada(c, w_ada, b_ada):
    bsz, d = c.shape
    n = w_ada.shape[1]
    tn = 1536
    return pl.pallas_call(
        _ada_kernel,
        grid=(n // tn,),
        in_specs=[pl.BlockSpec((bsz, d), lambda j: (0, 0)),
                  pl.BlockSpec((d, tn), lambda j: (0, j)),
                  pl.BlockSpec((1, tn), lambda j: (0, j))],
        out_specs=pl.BlockSpec((bsz, tn), lambda j: (0, j)),
        out_shape=jax.ShapeDtypeStruct((bsz, n), F32),
        compiler_params=_cparams(("arbitrary",)),
        name="ada",
    )(c, w_ada, b_ada.reshape(1, n))


def _rope(xh, cos, sin_a, sin_b):
    a = pltpu.roll(xh, 32, 1)
    b = pltpu.roll(xh, 96, 1)
    return xh * cos + b * sin_b + a * sin_a


def _inproj_kernel(x_ref, sc_ref, sh_ref, w_ref, dft_ref, qg_ref, kg_ref,
                   cos_ref, sa_ref, sb_ref,
                   fa_ref, fb_ref, q_ref, k_ref, vt_ref, gf_ref, ga_ref):
    x = x_ref[...]
    u = _layer_norm(x, ADA_EPS) * (1.0 + sc_ref[...]) + sh_ref[...]
    ub = u.astype(BF16)
    cos, sa, sb = cos_ref[...], sa_ref[...], sb_ref[...]

    c0 = 0
    zf = jnp.dot(ub, w_ref[:, c0:c0 + FOURIER_DIM], preferred_element_type=F32).astype(BF16)
    for g in range(N_FOURIER_GROUPS):
        sl = slice(g * FOURIER_GROUP_DIM, (g + 1) * FOURIER_GROUP_DIM)
        ab = jnp.dot(zf[:, sl], dft_ref[...], preferred_element_type=F32)
        fa_ref[:, sl] = ab[:, :FOURIER_GROUP_DIM].astype(BF16)
        fb_ref[:, sl] = ab[:, FOURIER_GROUP_DIM:].astype(BF16)
    c0 += FOURIER_DIM

    def norm_rope(z, gain, scale):
        ms = jnp.mean(z * z, axis=-1, keepdims=True)
        zn = z * lax.rsqrt(ms + QK_EPS) * gain
        return (_rope(zn, cos, sa, sb) * scale).astype(BF16)

    q_scale = LOG2E / math.sqrt(HEAD_DIM)
    zq = jnp.dot(ub, w_ref[:, c0:c0 + Q_DIM], preferred_element_type=F32)
    for h in range(N_Q_HEADS):
        sl = slice(h * HEAD_DIM, (h + 1) * HEAD_DIM)
        q_ref[:, sl] = norm_rope(zq[:, sl], qg_ref[...], q_scale)
    c0 += Q_DIM

    zk = jnp.dot(ub, w_ref[:, c0:c0 + KV_DIM], preferred_element_type=F32)
    for h in range(N_KV_HEADS):
        sl = slice(h * HEAD_DIM, (h + 1) * HEAD_DIM)
        k_ref[:, sl] = norm_rope(zk[:, sl], kg_ref[...], 1.0)
    c0 += KV_DIM

    zv = jnp.dot(ub, w_ref[:, c0:c0 + KV_DIM], preferred_element_type=F32)
    for h in range(N_KV_HEADS):
        vt_ref[h, :HEAD_DIM, :] = zv[:, h * HEAD_DIM:(h + 1) * HEAD_DIM].T.astype(BF16)
        vt_ref[h, HEAD_DIM:, :] = jnp.ones((ONES_ROWS, zv.shape[0]), BF16)
    c0 += KV_DIM

    zg = jnp.dot(ub, w_ref[:, c0:c0 + D_MODEL], preferred_element_type=F32)
    gf_ref[...] = jax.nn.sigmoid(zg).astype(BF16)
    c0 += D_MODEL
    zg = jnp.dot(ub, w_ref[:, c0:c0 + D_MODEL], preferred_element_type=F32)
    ga_ref[...] = jax.nn.sigmoid(zg).astype(BF16)


def _inproj(x, sc1, sh1, w_in, dft_c, qg, kg, cos, sin_a, sin_b, tm):
    bsz, s, d = x.shape
    ncol = w_in.shape[1]
    row = lambda w: pl.BlockSpec((None, tm, w), lambda b, i: (b, i, 0))
    mod = pl.BlockSpec((None, 1, d), lambda b, i: (b, 0, 0))
    full = lambda shp: pl.BlockSpec(shp, lambda b, i: (0,) * len(shp))
    tab = pl.BlockSpec((tm, HEAD_DIM), lambda b, i: (i, 0))
    vt_rows = HEAD_DIM + ONES_ROWS
    vt_spec = pl.BlockSpec((None, N_KV_HEADS, None, vt_rows, tm), lambda b, i: (b, 0, i, 0, 0))
    vt_shape = jax.ShapeDtypeStruct((bsz, N_KV_HEADS, s // tm, vt_rows, tm), BF16)
    rows_out = lambda w: (row(w), jax.ShapeDtypeStruct((bsz, s, w), BF16))
    outs = [rows_out(FOURIER_DIM), rows_out(FOURIER_DIM), rows_out(Q_DIM), rows_out(KV_DIM),
            (vt_spec, vt_shape), rows_out(D_MODEL), rows_out(D_MODEL)]
    return pl.pallas_call(
        _inproj_kernel,
        grid=(bsz, s // tm),
        in_specs=[row(d), mod, mod, full((d, ncol)), full(dft_c.shape),
                  full((1, HEAD_DIM)), full((1, HEAD_DIM)), tab, tab, tab],
        out_specs=[o[0] for o in outs],
        out_shape=[o[1] for o in outs],
        compiler_params=_cparams(("parallel", "parallel")),
        name="inproj",
    )(x, sc1, sh1, w_in, dft_c, qg, kg, cos, sin_a, sin_b)


def _fft1_kernel(a_ref, b_ref, m_ref, cw_ref, sw_ref, zr_ref, zi_ref, a32, b32):
    p, nb, _ = a_ref.shape
    a32[...] = a_ref[...].astype(F32)
    b32[...] = b_ref[...].astype(F32)
    for j in range(nb):
        ab = jnp.concatenate([a32[:, j, :], b32[:, j, :]], axis=0).astype(BF16)
        t = jnp.dot(m_ref[...], ab, preferred_element_type=F32)
        tr, ti = t[:p], t[p:]
        cw, sw = cw_ref[j], sw_ref[j]
        zr_ref[j] = (tr * cw - ti * sw).astype(BF16)
        zi_ref[j] = (ti * cw + tr * sw).astype(BF16)


def _fft1(fa, fb, m1, cw, sw, p, q):
    bsz, s, c = fa.shape
    nb = BF16_SUBLANES
    inp = pl.BlockSpec((None, p, nb, c), lambda b, j: (b, 0, j, 0))
    tw = pl.BlockSpec((nb, p, 1), lambda b, j: (j, 0, 0))
    out = pl.BlockSpec((None, nb, p, c), lambda b, j: (b, j, 0, 0))
    return pl.pallas_call(
        _fft1_kernel,
        grid=(bsz, q // nb),
        in_specs=[inp, inp, pl.BlockSpec(m1.shape, lambda b, j: (0, 0)), tw, tw],
        out_specs=[out, out],
        out_shape=[jax.ShapeDtypeStruct((bsz, q, p, c), BF16)] * 2,
        scratch_shapes=[pltpu.VMEM((p, nb, c), F32)] * 2,
        compiler_params=_cparams(("parallel", "parallel")),
        name="fft1",
    )(fa.reshape(bsz, p, q, c), fb.reshape(bsz, p, q, c), m1, cw, sw)


def _fft2_kernel(zr_ref, zi_ref, m_ref, y_ref, zr32, zi32, y32):
    zr32[...] = zr_ref[...].astype(F32)
    zi32[...] = zi_ref[...].astype(F32)
    for j in range(zr_ref.shape[1]):
        z = jnp.concatenate([zr32[:, j, :], zi32[:, j, :]], axis=0).astype(BF16)
        y32[:, j, :] = jnp.dot(m_ref[...], z, preferred_element_type=F32)
    y_ref[...] = y32[...].astype(BF16)


def _fft2(zr, zi, m2):
    bsz, q, p, c = zr.shape
    pc = BF16_SUBLANES
    blk = pl.BlockSpec((None, q, pc, c), lambda b, j: (b, 0, j, 0))
    y = pl.pallas_call(
        _fft2_kernel,
        grid=(bsz, p // pc),
        in_specs=[blk, blk, pl.BlockSpec(m2.shape, lambda b, j: (0, 0))],
        out_specs=blk,
        out_shape=jax.ShapeDtypeStruct((bsz, q, p, c), BF16),
        scratch_shapes=[pltpu.VMEM((q, pc, c), F32)] * 3,
        compiler_params=_cparams(("parallel", "parallel")),
        name="fft2",
    )(zr, zi, m2)
    return y.reshape(bsz, q * p, c)


def _dft_tables(s):
    q = FFT_Q
    p = s // q

    def cs(n_rows, n_cols, n):
        r = np.arange(n_rows, dtype=np.int64)[:, None]
        c = np.arange(n_cols, dtype=np.int64)[None, :]
        ang = ((r * c) % n).astype(np.float64) * (2.0 * math.pi / n)
        return np.cos(ang).astype(np.float32), np.sin(ang).astype(np.float32)

    cc, sc = cs(FOURIER_GROUP_DIM, FOURIER_GROUP_DIM, FOURIER_GROUP_DIM)
    dft_c = jnp.asarray(np.concatenate([cc, sc], axis=1), dtype=BF16)
    cp, sp = cs(p, p, p)
    m1 = jnp.asarray(np.block([[cp, -sp], [sp, cp]]), dtype=BF16)
    cq, sq = cs(q, q, q)
    scale = np.float32(1.0 / math.sqrt(s * FOURIER_GROUP_DIM))
    m2 = jnp.asarray(np.concatenate([cq, -sq], axis=1) * scale, dtype=BF16)
    cw, sw = cs(q, p, s)
    return dft_c, m1, m2, jnp.asarray(cw[:, :, None]), jnp.asarray(sw[:, :, None]), p, q


def _rope_tables(s):
    f32 = np.float32
    rows = s // GRID_W
    row_ids = np.repeat(np.arange(rows, dtype=f32), GRID_W)
    col_ids = np.tile(np.arange(GRID_W, dtype=f32), rows)
    freqs = (f32(ROPE_THETA) ** (-np.arange(0, ROPE_AXIS_DIM, 2, dtype=f32) / f32(ROPE_AXIS_DIM))).astype(f32)
    ang_r = (row_ids[:, None] * freqs).astype(f32)
    ang_c = (col_ids[:, None] * freqs).astype(f32)
    cr, sr, cc, sc = np.cos(ang_r), np.sin(ang_r), np.cos(ang_c), np.sin(ang_c)
    z = np.zeros_like(sr)
    cos = np.concatenate([cr, cr, cc, cc], axis=1)
    sin_b = np.concatenate([-sr, z, -sc, z], axis=1)
    sin_a = np.concatenate([z, sr, z, sc], axis=1)
    return jnp.asarray(cos, F32), jnp.asarray(sin_a, F32), jnp.asarray(sin_b, F32)


def _attn_kernel(q_ref, k_ref, vt_ref, o_ref, m_ref, acc_ref, al0, al1, s0, s1, p0, p1, qt_ref, mx0, mx1):
    al_ref, s_ref, p_ref, mx_ref = (al0, al1), (s0, s1), (p0, p1), (mx0, mx1)
    nchunk = vt_ref.shape[0]
    tk = vt_ref.shape[2]
    heads = range(Q_GROUP)

    def load_q(src):
        for g in heads:
            qt_ref[g] = src[:, g * HEAD_DIM:(g + 1) * HEAD_DIM].astype(F32).T.astype(BF16)

    def scores_into(ksrc, c, slot):
        kc = ksrc[pl.ds(pl.multiple_of(c * tk, tk), tk), :]
        for g in heads:
            sc = jnp.dot(kc, qt_ref[g], preferred_element_type=F32)
            s_ref[slot][g] = sc
            parts = [jnp.max(sc[r * (tk // 4):(r + 1) * (tk // 4)], axis=0, keepdims=True) for r in range(4)]
            mx_ref[slot][g] = jnp.maximum(jnp.maximum(parts[0], parts[1]), jnp.maximum(parts[2], parts[3]))

    def softmax(slot):
        for g in heads:
            m_old = m_ref[g]
            m_new = jnp.maximum(m_old, mx_ref[slot][g])
            al_ref[slot][g] = jnp.exp2(m_old - m_new)
            m_ref[g] = m_new
            p_ref[slot][g] = jnp.exp2(s_ref[slot][g] - m_new).astype(BF16)

    def pv_update(c, slot):
        vc = vt_ref[c]
        for g in heads:
            acc_ref[g] = al_ref[slot][g] * acc_ref[g] + jnp.dot(vc, p_ref[slot][g],
                                                                preferred_element_type=F32)

    def start_tile(qsrc, ksrc):
        load_q(qsrc)
        scores_into(ksrc, 0, 0)
        m_ref[...] = jnp.full(m_ref.shape, -jnp.inf, F32)
        acc_ref[...] = jnp.zeros(acc_ref.shape, F32)
        softmax(0)
        scores_into(ksrc, 1, 1)

    start_tile(q_ref, k_ref)

    def step(c, slot):
        scores_into(k_ref, c + 1, 1 - slot)
        pv_update(c - 1, 1 - slot)
        softmax(slot)

    def pair(i, carry):
        step(2 * i + 1, 1)
        step(2 * i + 2, 0)
        return carry

    lax.fori_loop(0, (nchunk - 2) // 2, pair, 0)

    last = nchunk - 1
    pv_update(last - 1, 0)
    softmax(1)
    pv_update(last, 1)
    for g in heads:
        acc = acc_ref[g]
        out_t = acc[:HEAD_DIM] / acc[HEAD_DIM:HEAD_DIM + 1]
        o_ref[:, g * HEAD_DIM:(g + 1) * HEAD_DIM] = out_t.T.astype(BF16)


def _attention(q, k, vt, tq):
    bsz, s, _ = q.shape
    rows, tk = vt.shape[3], vt.shape[4]
    nchunk = s // tk
    assert nchunk % 2 == 0 and nchunk >= 4
    gw = Q_GROUP * HEAD_DIM
    qspec = pl.BlockSpec((None, tq, gw), lambda b, h, i: (b, i, h))
    kspec = pl.BlockSpec((None, s, HEAD_DIM), lambda b, h, i: (b, 0, h))
    vspec = pl.BlockSpec((None, None, nchunk, rows, tk), lambda b, h, i: (b, h, 0, 0, 0))
    head_vec = pltpu.VMEM((Q_GROUP, 1, tq), F32)
    return pl.pallas_call(
        _attn_kernel,
        grid=(bsz, N_KV_HEADS, s // tq),
        in_specs=[qspec, kspec, vspec],
        out_specs=qspec,
        out_shape=jax.ShapeDtypeStruct((bsz, s, Q_DIM), BF16),
        scratch_shapes=[head_vec, pltpu.VMEM((Q_GROUP, rows, tq), F32), head_vec, head_vec]
        + [pltpu.VMEM((Q_GROUP, tk, tq), F32)] * 2 + [pltpu.VMEM((Q_GROUP, tk, tq), BF16)] * 2
        + [pltpu.VMEM((Q_GROUP, HEAD_DIM, tq), BF16), head_vec, head_vec],
        compiler_params=_cparams(("parallel", "parallel", "parallel")),
        name="attn",
    )(q, k, vt)


def _mix_kernel(yf_ref, at_ref, gf_ref, ga_ref, x_ref, g1_ref, sc2_ref, sh2_ref,
                wf_ref, wo_ref, wout_ref, l1g_ref, l1b_ref, wr_ref, br_ref, tri_ref,
                x1_ref, u2_ref, idx_ref, wgt_ref, rank_ref, cnt_ref, cnt_acc):
    first = jnp.logical_and(pl.program_id(0) == 0, pl.program_id(1) == 0)

    @pl.when(first)
    def _():
        cnt_acc[...] = jnp.zeros_like(cnt_acc)

    tm_all = x_ref.shape[0]
    rg = tm_all // MIX_ROW_GROUPS
    u_his, u_los = [], []
    groups = [slice(i * rg, (i + 1) * rg) for i in range(MIX_ROW_GROUPS)]
    ms = []
    for rows in groups:
        yf = jnp.dot(yf_ref[rows, :], wf_ref[...], preferred_element_type=F32)
        ya = jnp.dot(at_ref[rows, :], wo_ref[...], preferred_element_type=F32)
        ms.append((gf_ref[rows, :].astype(F32) * yf + ga_ref[rows, :].astype(F32) * ya).astype(BF16))
    hs = [jnp.dot(m, wout_ref[...], preferred_element_type=F32) for m in ms]
    for rows, h in zip(groups, hs):
        r = DEEPNORM_ALPHA * x_ref[rows, :] + g1_ref[...] * h
        x1 = _layer_norm(r, LN_EPS) * l1g_ref[...] + l1b_ref[...]
        x1_ref[rows, :] = x1
        u2 = _layer_norm(x1, ADA_EPS) * (1.0 + sc2_ref[...]) + sh2_ref[...]
        u2_ref[rows, :] = _pack_bf16_pairs(u2)
        u_hi = u2.astype(BF16)
        u_his.append(u_hi)
        u_los.append((u2 - u_hi.astype(F32)).astype(BF16))

    nt = (((1,), (1,)), ((), ()))
    u_hi = jnp.concatenate(u_his, axis=0)
    u_lo = jnp.concatenate(u_los, axis=0)
    w = wr_ref[...]
    w_hi = w.astype(BF16)
    w_lo = (w - w_hi.astype(F32)).astype(BF16)
    logits = (lax.dot_general(w_hi, u_hi, nt, preferred_element_type=F32)
              + lax.dot_general(w_lo, u_hi, nt, preferred_element_type=F32)
              + lax.dot_general(w_hi, u_lo, nt, preferred_element_type=F32)) + br_ref[...]

    ne, tm = logits.shape
    eid = lax.broadcasted_iota(jnp.int32, (ne, tm), 0)
    work = logits
    vals, idxs, hots = [], [], []
    for _ in range(TOP_K):
        mx = jnp.max(work, axis=0, keepdims=True)
        ix = jnp.min(jnp.where(work == mx, eid, ne), axis=0, keepdims=True)
        hot = eid == ix
        work = jnp.where(hot, -jnp.inf, work)
        vals.append(mx)
        idxs.append(ix)
        hots.append(hot)
    ex = [jnp.exp(v - vals[0]) for v in vals]
    den = ex[0] + ex[1] + ex[2] + ex[3]
    wgt_ref[...] = jnp.concatenate([e / den for e in ex], axis=0)
    idx_ref[...] = jnp.concatenate(idxs, axis=0)

    sel = hots[0] | hots[1] | hots[2] | hots[3]
    mask = jnp.where(sel, 1.0, 0.0)
    prefix = jnp.dot(mask.astype(BF16), tri_ref[...], preferred_element_type=F32)
    pos = prefix + cnt_acc[...]
    ranks = [jnp.sum(jnp.where(hot, pos, 0.0), axis=0, keepdims=True) for hot in hots]
    rank_ref[...] = jnp.concatenate(ranks, axis=0).astype(jnp.int32)
    cnt_acc[...] += jnp.sum(mask, axis=1, keepdims=True)
    cnt_ref[...] = cnt_acc[...].astype(jnp.int32)


def _mix(yfm, attn, gf, ga, x, g1, sc2, sh2, wf, wo, wout, l1g, l1b, wr_t, br, tm, b0, bsz):
    _, s, d = x.shape
    n = bsz * s
    nt = s // tm
    tri = (jnp.arange(tm)[:, None] < jnp.arange(tm)[None, :]).astype(BF16)
    row_in = lambda w: pl.BlockSpec((None, tm, w), lambda b, i: (b + b0, i, 0))
    row = lambda w: pl.BlockSpec((None, tm, w), lambda b, i: (b, i, 0))
    mod = pl.BlockSpec((None, 1, d), lambda b, i: (b + b0, 0, 0))
    full = lambda shp: pl.BlockSpec(shp, lambda b, i: (0,) * len(shp))
    tok = pl.BlockSpec((TOP_K, tm), lambda b, i: (0, b * nt + i))
    return pl.pallas_call(
        _mix_kernel,
        grid=(bsz, nt),
        in_specs=[row_in(FOURIER_DIM), row_in(Q_DIM), row_in(d), row_in(d), row_in(d), mod, mod, mod,
                  full(wf.shape), full(wo.shape), full(wout.shape), full((1, d)), full((1, d)),
                  full(wr_t.shape), full((N_EXPERTS, 1)), full((tm, tm))],
        out_specs=[row(d), row(d // 2), tok, tok, tok, full((N_EXPERTS, 1))],
        out_shape=[jax.ShapeDtypeStruct((bsz, s, d), F32),
                   jax.ShapeDtypeStruct((bsz, s, d // 2), jnp.int32),
                   jax.ShapeDtypeStruct((TOP_K, n), jnp.int32),
                   jax.ShapeDtypeStruct((TOP_K, n), F32),
                   jax.ShapeDtypeStruct((TOP_K, n), jnp.int32),
                   jax.ShapeDtypeStruct((N_EXPERTS, 1), jnp.int32)],
        scratch_shapes=[pltpu.VMEM((N_EXPERTS, 1), F32)],
        compiler_params=_cparams(("arbitrary", "arbitrary")),
        name="mix",
    )(yfm, attn, gf, ga, x, g1, sc2, sh2, wf, wo, wout, l1g, l1b, wr_t, br, tri)


def _expert_kernel(be_ref, nv_ref, new_ref, xs_ref, wgu_ref, bgu_ref, wd_ref, bd_ref, ys_ref,
                   wgu_bf, wd_bf):
    del be_ref
    i = pl.program_id(0)
    nvalid = nv_ref[i]

    @pl.when(new_ref[i] == 1)
    def _():
        wgu_bf[...] = wgu_ref[...].astype(BF16)
        wd_bf[...] = wd_ref[...].astype(BF16)

    @pl.when(nvalid > 0)
    def _():
        xw = xs_ref[...]
        row = lax.broadcasted_iota(jnp.int32, xw.shape, 0)
        lo, hi = _unpack_bf16_pairs(jnp.where(row < nvalid, xw, 0))
        xb = jnp.concatenate([lo.astype(BF16), hi.astype(BF16)], axis=1)
        h = jnp.dot(xb, wgu_bf[...], preferred_element_type=F32) + bgu_ref[...]
        gate = jnp.minimum(h[:, :D_EXPERT], SWIGLU_LIMIT)
        up = jnp.clip(h[:, D_EXPERT:], -SWIGLU_LIMIT, SWIGLU_LIMIT)
        act = (up + 1.0) * gate * jax.nn.sigmoid(SWIGLU_ALPHA * gate)
        y = jnp.dot(act.astype(BF16), wd_bf[...], preferred_element_type=F32) + bd_ref[...]
        ys_ref[...] = _pack_bf16_pairs(y)

    @pl.when(nvalid == 0)
    def _():
        ys_ref[...] = jnp.zeros_like(ys_ref)


def _experts(block_e, block_nvalid, block_new, xs, wgu, bgu, wd, bd, tmb):
    n_pad, dh = xs.shape
    d = 2 * dh
    grid_spec = pltpu.PrefetchScalarGridSpec(
        num_scalar_prefetch=3,
        grid=(n_pad // tmb,),
        in_specs=[pl.BlockSpec((tmb, dh), lambda i, be, nv, nw: (i, 0)),
                  pl.BlockSpec((None, d, 2 * D_EXPERT), lambda i, be, nv, nw: (be[i], 0, 0)),
                  pl.BlockSpec((None, 1, 2 * D_EXPERT), lambda i, be, nv, nw: (be[i], 0, 0)),
                  pl.BlockSpec((None, D_EXPERT, d), lambda i, be, nv, nw: (be[i], 0, 0)),
                  pl.BlockSpec((None, 1, d), lambda i, be, nv, nw: (be[i], 0, 0))],
        out_specs=pl.BlockSpec((tmb, dh), lambda i, be, nv, nw: (i, 0)),
        scratch_shapes=[pltpu.VMEM((d, 2 * D_EXPERT), BF16), pltpu.VMEM((D_EXPERT, d), BF16)],
    )
    return pl.pallas_call(
        _expert_kernel,
        grid_spec=grid_spec,
        out_shape=jax.ShapeDtypeStruct((n_pad, dh), jnp.int32),
        compiler_params=_cparams(("arbitrary",)),
        name="experts",
    )(block_e, block_nvalid, block_new, xs, wgu, bgu, wd, bd)


def _final_kernel(yg_ref, w_ref, x1_ref, g2_ref, lg_ref, lb_ref, o_ref):
    w = w_ref[...]
    acc_lo = None
    for j in range(TOP_K):
        lo, hi = _unpack_bf16_pairs(yg_ref[j])
        wj = w[:, j:j + 1]
        acc_lo = lo * wj if acc_lo is None else acc_lo + lo * wj
        acc_hi = hi * wj if j == 0 else acc_hi + hi * wj
    h = jnp.concatenate([acc_lo, acc_hi], axis=1)
    r = DEEPNORM_ALPHA * x1_ref[...] + g2_ref[...] * h
    o_ref[...] = _layer_norm(r, LN_EPS) * lg_ref[...] + lb_ref[...]


def _final(yg, w_tok, x1, g2, lg, lb, tm, b0, total_bsz, out_so_far):
    bsz, s, d = x1.shape
    nt = s // tm
    row = pl.BlockSpec((None, tm, d), lambda b, i: (b, i, 0))
    full = pl.BlockSpec((1, d), lambda b, i: (0, 0))
    in_specs = [pl.BlockSpec((TOP_K, tm, d // 2), lambda b, i: (0, b * nt + i, 0)),
                pl.BlockSpec((tm, TOP_K), lambda b, i: (b * nt + i, 0)),
                row, pl.BlockSpec((None, 1, d), lambda b, i: (b + b0, 0, 0)), full, full]
    args = [yg, w_tok, x1, g2, lg, lb]
    aliases = {}
    if out_so_far is not None:
        in_specs.append(pl.BlockSpec(memory_space=pl.ANY))
        args.append(out_so_far)
        aliases = {len(args) - 1: 0}
    body = _final_kernel if out_so_far is None else (lambda *refs: _final_kernel(*refs[:6], refs[7]))
    return pl.pallas_call(
        body,
        grid=(bsz, nt),
        in_specs=in_specs,
        out_specs=pl.BlockSpec((None, tm, d), lambda b, i: (b + b0, i, 0)),
        out_shape=jax.ShapeDtypeStruct((total_bsz, s, d), F32),
        input_output_aliases=aliases,
        compiler_params=_cparams(("parallel", "parallel")),
        name="final",
    )(*args)


def _scatter_rows(rows, dest, m_out):
    n, w = rows.shape
    nj = dest.shape[0]
    ch = GATHER_CHUNK
    nw = SC_CORES * SC_SUBCORES
    nch = n // (nw * ch)
    assert n == nw * nch * ch and nch % 2 == 0
    per_w = nch * ch
    mesh = plsc.VectorSubcoreMesh(core_axis_name="c", subcore_axis_name="s",
                                  num_cores=SC_CORES, num_subcores=SC_SUBCORES)

    @functools.partial(
        pl.kernel, mesh=mesh,
        out_type=jax.ShapeDtypeStruct((m_out, w), rows.dtype),
        scratch_types=[pltpu.VMEM((nj, nch, ch), jnp.int32), pltpu.VMEM((2, ch, w), rows.dtype),
                       pltpu.SemaphoreType.DMA((2,)), pltpu.SemaphoreType.DMA((2,))],
    )
    def scatter_kernel(rows_hbm, dest_hbm, out_hbm, idx_v, rows_v, rsem, ssem):
        wid = lax.axis_index("s") * SC_CORES + lax.axis_index("c")
        base = wid * per_w
        for j in range(nj):
            pltpu.sync_copy(dest_hbm.at[j, wid], idx_v.at[j])

        def read(i, b):
            off = pl.multiple_of(base + i * ch, ch)
            return pltpu.make_async_copy(rows_hbm.at[pl.ds(off, ch)], rows_v.at[b], rsem.at[b])

        def scatter(i, b, j):
            return pltpu.make_async_copy(rows_v.at[b], out_hbm.at[idx_v.at[j, i]], ssem.at[b])

        read(0, 0).start()

        @pl.loop(0, nch, step=2)
        def _(i0):
            for b in range(2):
                i = i0 + b
                read(i, b).wait()

                @pl.when(i + 1 < nch)
                def _():
                    @pl.when(i >= 1)
                    def _():
                        for j in range(nj):
                            scatter(i - 1, 1 - b, j).wait()
                    read(i + 1, 1 - b).start()

                for j in range(nj):
                    scatter(i, b, j).start()

        for b in range(2):
            for j in range(nj):
                scatter(nch - 2 + b, b, j).wait()

    return scatter_kernel(rows, dest.reshape(nj, nw, nch, ch))


def _gather_rows(table, idx):
    m = idx.shape[0]
    w = table.shape[1]
    ch = GATHER_CHUNK
    nw = SC_CORES * SC_SUBCORES
    nch = m // (nw * ch)
    assert m == nw * nch * ch and nch % 2 == 0
    per_w = nch * ch
    mesh = plsc.VectorSubcoreMesh(core_axis_name="c", subcore_axis_name="s",
                                  num_cores=SC_CORES, num_subcores=SC_SUBCORES)

    @functools.partial(
        pl.kernel, mesh=mesh,
        out_type=jax.ShapeDtypeStruct((m, w), table.dtype),
        scratch_types=[pltpu.VMEM((nch, ch), jnp.int32), pltpu.VMEM((2, ch, w), table.dtype),
                       pltpu.SemaphoreType.DMA((2,)), pltpu.SemaphoreType.DMA((2,))],
    )
    def gather_kernel(table_hbm, idx_hbm, out_hbm, idx_v, rows_v, gsem, wsem):
        wid = lax.axis_index("s") * SC_CORES + lax.axis_index("c")
        base = wid * per_w
        pltpu.sync_copy(idx_hbm.at[wid], idx_v)

        def gather(j, b):
            return pltpu.make_async_copy(table_hbm.at[idx_v.at[j]], rows_v.at[b], gsem.at[b])

        def write(j, b):
            off = pl.multiple_of(base + j * ch, ch)
            return pltpu.make_async_copy(rows_v.at[b], out_hbm.at[pl.ds(off, ch)], wsem.at[b])

        gather(0, 0).start()

        @pl.loop(0, nch, step=2)
        def _(i):
            for b in range(2):
                j = i + b
                gather(j, b).wait()

                @pl.when(j + 1 < nch)
                def _():
                    @pl.when(j >= 1)
                    def _():
                        write(j - 1, 1 - b).wait()
                    gather(j + 1, 1 - b).start()

                write(j, b).start()

        write(nch - 2, 0).wait()
        write(nch - 1, 1).wait()

    return gather_kernel(table, idx.reshape(nw, nch, ch))


def kernel(x, c, w_ada, b_ada, w_in, q_norm_g, k_norm_g, w_fourier, w_attn_o, w_out, ln1_g, ln1_b,
           w_router, b_router, w_gate_up, b_gate_up, w_down, b_down, ln2_g, ln2_b):
    bsz, s, d = x.shape
    n = bsz * s
    tm = min(ROW_TILE, s)
    l = 0

    mod = _ada(c, w_ada[l], b_ada[l])
    sh1, sc1, g1, sh2, sc2, g2 = [m[:, None, :] for m in jnp.split(mod, N_MOD, axis=-1)]

    dft_c, m1, m2, cw, sw, p, q = _dft_tables(s)
    cos, sin_a, sin_b = _rope_tables(s)

    fa, fb, qh, kh, vt, gf, ga = _inproj(
        x, sc1, sh1, w_in[l].astype(BF16), dft_c,
        q_norm_g[l].reshape(1, HEAD_DIM), k_norm_g[l].reshape(1, HEAD_DIM), cos, sin_a, sin_b, tm)

    zr, zi = _fft1(fa, fb, m1, cw, sw, p, q)
    yfm = _fft2(zr, zi, m2)

    attn = _attention(qh, kh, vt, min(ATTN_TQ, s))

    wf_b, wo_b, wout_b = w_fourier[l].astype(BF16), w_attn_o[l].astype(BF16), w_out[l].astype(BF16)
    bgu, bd = b_gate_up[l][:, None, :], b_down[l][:, None, :]
    tmb = MOE_BLOCK
    eids = jnp.arange(N_EXPERTS, dtype=jnp.int32)

    ngroups = MOE_BATCH_GROUPS if bsz % MOE_BATCH_GROUPS == 0 else 1
    gb = bsz // ngroups
    ng = gb * s
    out = None
    for grp in range(ngroups):
        b0 = grp * gb
        x1, u2p, idx_t, wgt_t, rank_t, counts = _mix(
            yfm, attn, gf, ga, x, g1, sc2, sh2, wf_b, wo_b, wout_b,
            ln1_g[l].reshape(1, d), ln1_b[l].reshape(1, d),
            w_router[l].T, b_router[l].reshape(N_EXPERTS, 1), tm, b0, gb)

        counts = counts.reshape(N_EXPERTS)
        pcounts = ((counts + tmb - 1) // tmb) * tmb
        pends = jnp.cumsum(pcounts)
        pstarts = pends - pcounts
        n_pad = ng * TOP_K + N_EXPERTS * tmb
        nblk = n_pad // tmb
        start_of = jnp.sum(jnp.where(idx_t[None] == eids[:, None, None], pstarts[:, None, None], 0), axis=0)
        dest_t = start_of + rank_t
        block_starts = jnp.arange(nblk, dtype=jnp.int32) * tmb
        block_e = jnp.sum((block_starts[:, None] >= pends[None, :]).astype(jnp.int32), axis=1)
        block_e = jnp.minimum(block_e, N_EXPERTS - 1)
        block_nvalid = jnp.clip(pstarts[block_e] + counts[block_e] - block_starts, 0, tmb)
        block_new = jnp.concatenate([jnp.ones((1,), jnp.int32),
                                     (block_e[1:] != block_e[:-1]).astype(jnp.int32)])

        xs = _scatter_rows(u2p.reshape(ng, d // 2), dest_t, n_pad)
        ys = _experts(block_e, block_nvalid, block_new, xs, w_gate_up[l], bgu, w_down[l], bd, tmb)
        yg = _gather_rows(ys, dest_t.reshape(-1)).reshape(TOP_K, ng, d // 2)
        out = _final(yg, wgt_t.T, x1, g2, ln2_g[l].reshape(1, d), ln2_b[l].reshape(1, d), tm,
                     b0, bsz, out)
    return out
```

```python
import functools
import math

import jax
import jax.numpy as jnp
import numpy as np
from jax import lax
from jax.experimental import pallas as pl
from jax.experimental.pallas import tpu as pltpu
from jax.experimental.pallas import tpu_sc as plsc

F32 = jnp.float32
BF16 = jnp.bfloat16

D_MODEL = 1024
GRID_W = 64
N_Q_HEADS = 8
N_KV_HEADS = 2
HEAD_DIM = 128
Q_GROUP = N_Q_HEADS // N_KV_HEADS
ROPE_THETA = 10000.0
ROPE_AXIS_DIM = HEAD_DIM // 2
N_FOURIER_GROUPS = 4
FOURIER_GROUP_DIM = 128
FOURIER_DIM = N_FOURIER_GROUPS * FOURIER_GROUP_DIM
Q_DIM = N_Q_HEADS * HEAD_DIM
KV_DIM = N_KV_HEADS * HEAD_DIM
N_EXPERTS = 32
TOP_K = 4
D_EXPERT = 1024
SWIGLU_LIMIT = 7.0
SWIGLU_ALPHA = 1.702
N_MOD = 6
DEPTH = 1
DEEPNORM_ALPHA = (2 * DEPTH) ** 0.25
LN_EPS = 1e-5
ADA_EPS = 1e-6
QK_EPS = 1e-6
LOG2E = 1.4426950408889634

V7X_VMEM_BYTES = 64 * 1024 * 1024
VMEM_LIMIT = V7X_VMEM_BYTES - 8 * 1024 * 1024
LANES = 128
BF16_SUBLANES = 16

FFT_Q = 128
ROW_TILE = 512
ATTN_TQ = 256
ONES_ROWS = 16
DENOMINATOR_GUARD = 2.0 ** -80
MIX_ROW_GROUPS = 2
MOE_BLOCK = 512
MOE_BATCH_GROUPS = 2

SC_CORES = 2
SC_SUBCORES = 16
GATHER_CHUNK = 64


def _cparams(sem):
    return pltpu.CompilerParams(dimension_semantics=sem, vmem_limit_bytes=VMEM_LIMIT)


def _layer_norm(x, eps):
    mu = jnp.mean(x, axis=-1, keepdims=True)
    xc = x - mu
    var = jnp.mean(xc * xc, axis=-1, keepdims=True)
    return xc * lax.rsqrt(var + eps)


def _pack_bf16_pairs(x):
    n = x.shape[1] // 2
    bits = pltpu.bitcast(x.astype(BF16).astype(F32), jnp.uint32)
    word = (bits[:, :n] >> 16) | (bits[:, n:] & jnp.uint32(0xFFFF0000))
    return pltpu.bitcast(word, jnp.int32)


def _unpack_bf16_pairs(w):
    bits = pltpu.bitcast(w, jnp.uint32)
    lo = pltpu.bitcast(bits << 16, F32)
    hi = pltpu.bitcast(bits & jnp.uint32(0xFFFF0000), F32)
    return lo, hi


def _ada_kernel(c_ref, w_ref, b_ref, o_ref):
    c = c_ref[...]
    cond = c * jax.nn.sigmoid(c)
    o_ref[...] = jnp.dot(cond, w_ref[...], preferred_element_type=F32,
                         precision=lax.Precision.HIGHEST) + b_ref[...]


def _ada(c, w_ada, b_ada):
    bsz, d = c.shape
    n = w_ada.shape[1]
    tn = 1536
    return pl.pallas_call(
        _ada_kernel,
        grid=(n // tn,),
        in_specs=[pl.BlockSpec((bsz, d), lambda j: (0, 0)),
                  pl.BlockSpec((d, tn), lambda j: (0, j)),
                  pl.BlockSpec((1, tn), lambda j: (0, j))],
        out_specs=pl.BlockSpec((bsz, tn), lambda j: (0, j)),
        out_shape=jax.ShapeDtypeStruct((bsz, n), F32),
        compiler_params=_cparams(("arbitrary",)),
        name="ada",
    )(c, w_ada, b_ada.reshape(1, n))


def _rope(xh, cos, sin_a, sin_b):
    a = pltpu.roll(xh, 32, 1)
    b = pltpu.roll(xh, 96, 1)
    return xh * cos + b * sin_b + a * sin_a


def _inproj_kernel(x_ref, sc_ref, sh_ref, w_ref, dft_ref, qg_ref, kg_ref,
                   cos_ref, sa_ref, sb_ref,
                   fa_ref, fb_ref, q_ref, k_ref, vt_ref, gf_ref, ga_ref):
    x = x_ref[...]
    u = _layer_norm(x, ADA_EPS) * (1.0 + sc_ref[...]) + sh_ref[...]
    ub = u.astype(BF16)
    cos, sa, sb = cos_ref[...], sa_ref[...], sb_ref[...]

    c0 = 0
    zf = jnp.dot(ub, w_ref[:, c0:c0 + FOURIER_DIM], preferred_element_type=F32).astype(BF16)
    for g in range(N_FOURIER_GROUPS):
        sl = slice(g * FOURIER_GROUP_DIM, (g + 1) * FOURIER_GROUP_DIM)
        ab = jnp.dot(zf[:, sl], dft_ref[...], preferred_element_type=F32)
        fa_ref[:, sl] = ab[:, :FOURIER_GROUP_DIM].astype(BF16)
        fb_ref[:, sl] = ab[:, FOURIER_GROUP_DIM:].astype(BF16)
    c0 += FOURIER_DIM

    def norm_rope(z, gain, scale):
        ms = jnp.mean(z * z, axis=-1, keepdims=True)
        zn = z * lax.rsqrt(ms + QK_EPS) * gain
        return (_rope(zn, cos, sa, sb) * scale).astype(BF16)

    q_scale = LOG2E / math.sqrt(HEAD_DIM)
    zq = jnp.dot(ub, w_ref[:, c0:c0 + Q_DIM], preferred_element_type=F32)
    for h in range(N_Q_HEADS):
        sl = slice(h * HEAD_DIM, (h + 1) * HEAD_DIM)
        q_ref[:, sl] = norm_rope(zq[:, sl], qg_ref[...], q_scale)
    c0 += Q_DIM

    zk = jnp.dot(ub, w_ref[:, c0:c0 + KV_DIM], preferred_element_type=F32)
    for h in range(N_KV_HEADS):
        sl = slice(h * HEAD_DIM, (h + 1) * HEAD_DIM)
        k_ref[:, sl] = norm_rope(zk[:, sl], kg_ref[...], 1.0)
    c0 += KV_DIM

    zv = jnp.dot(ub, w_ref[:, c0:c0 + KV_DIM], preferred_element_type=F32)
    for h in range(N_KV_HEADS):
        vt_ref[h, :HEAD_DIM, :] = zv[:, h * HEAD_DIM:(h + 1) * HEAD_DIM].T.astype(BF16)
        vt_ref[h, HEAD_DIM:, :] = jnp.ones((ONES_ROWS, zv.shape[0]), BF16)
    c0 += KV_DIM

    zg = jnp.dot(ub, w_ref[:, c0:c0 + D_MODEL], preferred_element_type=F32)
    gf_ref[...] = jax.nn.sigmoid(zg).astype(BF16)
    c0 += D_MODEL
    zg = jnp.dot(ub, w_ref[:, c0:c0 + D_MODEL], preferred_element_type=F32)
    ga_ref[...] = jax.nn.sigmoid(zg).astype(BF16)


def _inproj(x, sc1, sh1, w_in, dft_c, qg, kg, cos, sin_a, sin_b, tm):
    bsz, s, d = x.shape
    ncol = w_in.shape[1]
    row = lambda w: pl.BlockSpec((None, tm, w), lambda b, i: (b, i, 0))
    mod = pl.BlockSpec((None, 1, d), lambda b, i: (b, 0, 0))
    full = lambda shp: pl.BlockSpec(shp, lambda b, i: (0,) * len(shp))
    tab = pl.BlockSpec((tm, HEAD_DIM), lambda b, i: (i, 0))
    vt_rows = HEAD_DIM + ONES_ROWS
    vt_spec = pl.BlockSpec((None, N_KV_HEADS, None, vt_rows, tm), lambda b, i: (b, 0, i, 0, 0))
    vt_shape = jax.ShapeDtypeStruct((bsz, N_KV_HEADS, s // tm, vt_rows, tm), BF16)
    rows_out = lambda w: (row(w), jax.ShapeDtypeStruct((bsz, s, w), BF16))
    outs = [rows_out(FOURIER_DIM), rows_out(FOURIER_DIM), rows_out(Q_DIM), rows_out(KV_DIM),
            (vt_spec, vt_shape), rows_out(D_MODEL), rows_out(D_MODEL)]
    return pl.pallas_call(
        _inproj_kernel,
        grid=(bsz, s // tm),
        in_specs=[row(d), mod, mod, full((d, ncol)), full(dft_c.shape),
                  full((1, HEAD_DIM)), full((1, HEAD_DIM)), tab, tab, tab],
        out_specs=[o[0] for o in outs],
        out_shape=[o[1] for o in outs],
        compiler_params=_cparams(("parallel", "parallel")),
        name="inproj",
    )(x, sc1, sh1, w_in, dft_c, qg, kg, cos, sin_a, sin_b)


def _fft1_kernel(a_ref, b_ref, m_ref, cw_ref, sw_ref, zr_ref, zi_ref, a32, b32):
    p, nb, _ = a_ref.shape
    a32[...] = a_ref[...].astype(F32)
    b32[...] = b_ref[...].astype(F32)
    for j in range(nb):
        ab = jnp.concatenate([a32[:, j, :], b32[:, j, :]], axis=0).astype(BF16)
        t = jnp.dot(m_ref[...], ab, preferred_element_type=F32)
        tr, ti = t[:p], t[p:]
        cw, sw = cw_ref[j], sw_ref[j]
        zr_ref[j] = (tr * cw - ti * sw).astype(BF16)
        zi_ref[j] = (ti * cw + tr * sw).astype(BF16)


def _fft1(fa, fb, m1, cw, sw, p, q):
    bsz, s, c = fa.shape
    nb = BF16_SUBLANES
    inp = pl.BlockSpec((None, p, nb, c), lambda b, j: (b, 0, j, 0))
    tw = pl.BlockSpec((nb, p, 1), lambda b, j: (j, 0, 0))
    out = pl.BlockSpec((None, nb, p, c), lambda b, j: (b, j, 0, 0))
    return pl.pallas_call(
        _fft1_kernel,
        grid=(bsz, q // nb),
        in_specs=[inp, inp, pl.BlockSpec(m1.shape, lambda b, j: (0, 0)), tw, tw],
        out_specs=[out, out],
        out_shape=[jax.ShapeDtypeStruct((bsz, q, p, c), BF16)] * 2,
        scratch_shapes=[pltpu.VMEM((p, nb, c), F32)] * 2,
        compiler_params=_cparams(("parallel", "parallel")),
        name="fft1",
    )(fa.reshape(bsz, p, q, c), fb.reshape(bsz, p, q, c), m1, cw, sw)


def _fft2_kernel(zr_ref, zi_ref, m_ref, y_ref, zr32, zi32, y32):
    zr32[...] = zr_ref[...].astype(F32)
    zi32[...] = zi_ref[...].astype(F32)
    for j in range(zr_ref.shape[1]):
        z = jnp.concatenate([zr32[:, j, :], zi32[:, j, :]], axis=0).astype(BF16)
        y32[:, j, :] = jnp.dot(m_ref[...], z, preferred_element_type=F32)
    y_ref[...] = y32[...].astype(BF16)


def _fft2(zr, zi, m2):
    bsz, q, p, c = zr.shape
    pc = BF16_SUBLANES
    blk = pl.BlockSpec((None, q, pc, c), lambda b, j: (b, 0, j, 0))
    y = pl.pallas_call(
        _fft2_kernel,
        grid=(bsz, p // pc),
        in_specs=[blk, blk, pl.BlockSpec(m2.shape, lambda b, j: (0, 0))],
        out_specs=blk,
        out_shape=jax.ShapeDtypeStruct((bsz, q, p, c), BF16),
        scratch_shapes=[pltpu.VMEM((q, pc, c), F32)] * 3,
        compiler_params=_cparams(("parallel", "parallel")),
        name="fft2",
    )(zr, zi, m2)
    return y.reshape(bsz, q * p, c)


def _dft_tables(s):
    q = FFT_Q
    p = s // q

    def cs(n_rows, n_cols, n):
        r = np.arange(n_rows, dtype=np.int64)[:, None]
        c = np.arange(n_cols, dtype=np.int64)[None, :]
        ang = ((r * c) % n).astype(np.float64) * (2.0 * math.pi / n)
        return np.cos(ang).astype(np.float32), np.sin(ang).astype(np.float32)

    cc, sc = cs(FOURIER_GROUP_DIM, FOURIER_GROUP_DIM, FOURIER_GROUP_DIM)
    dft_c = jnp.asarray(np.concatenate([cc, sc], axis=1), dtype=BF16)
    cp, sp = cs(p, p, p)
    m1 = jnp.asarray(np.block([[cp, -sp], [sp, cp]]), dtype=BF16)
    cq, sq = cs(q, q, q)
    scale = np.float32(1.0 / math.sqrt(s * FOURIER_GROUP_DIM))
    m2 = jnp.asarray(np.concatenate([cq, -sq], axis=1) * scale, dtype=BF16)
    cw, sw = cs(q, p, s)
    return dft_c, m1, m2, jnp.asarray(cw[:, :, None]), jnp.asarray(sw[:, :, None]), p, q


def _rope_tables(s):
    f32 = np.float32
    rows = s // GRID_W
    row_ids = np.repeat(np.arange(rows, dtype=f32), GRID_W)
    col_ids = np.tile(np.arange(GRID_W, dtype=f32), rows)
    freqs = (f32(ROPE_THETA) ** (-np.arange(0, ROPE_AXIS_DIM, 2, dtype=f32) / f32(ROPE_AXIS_DIM))).astype(f32)
    ang_r = (row_ids[:, None] * freqs).astype(f32)
    ang_c = (col_ids[:, None] * freqs).astype(f32)
    cr, sr, cc, sc = np.cos(ang_r), np.sin(ang_r), np.cos(ang_c), np.sin(ang_c)
    z = np.zeros_like(sr)
    cos = np.concatenate([cr, cr, cc, cc], axis=1)
    sin_b = np.concatenate([-sr, z, -sc, z], axis=1)
    sin_a = np.concatenate([z, sr, z, sc], axis=1)
    return jnp.asarray(cos, F32), jnp.asarray(sin_a, F32), jnp.asarray(sin_b, F32)


def _attn_kernel(kb_ref, q_ref, k_ref, vt_ref, o_ref, m_ref, acc_ref, al0, al1, s0, s1, p0, p1,
                 qt_ref, mx0, mx1, qa_ref):
    al_ref, s_ref, p_ref, mx_ref = (al0, al1), (s0, s1), (p0, p1), (mx0, mx1)
    nchunk = vt_ref.shape[0]
    tk = vt_ref.shape[2]
    tq = q_ref.shape[0]
    heads = range(Q_GROUP)

    def k_chunk(c):
        return k_ref[pl.ds(pl.multiple_of(c * tk, tk), tk), :]

    def write_out():
        for g in heads:
            acc = acc_ref[g]
            out_t = acc[:HEAD_DIM] / acc[HEAD_DIM:HEAD_DIM + 1]
            o_ref[:, g * HEAD_DIM:(g + 1) * HEAD_DIM] = out_t.T.astype(BF16)

    row0 = lax.broadcasted_iota(jnp.int32, (HEAD_DIM, tq), 0) == 0
    for g in heads:
        qt = q_ref[:, g * HEAD_DIM:(g + 1) * HEAD_DIM].astype(F32).T
        bound = jnp.sqrt(jnp.sum(qt * qt, axis=0, keepdims=True)) * kb_ref[...]
        qa_ref[g, :HEAD_DIM, :] = qt.astype(BF16)
        qa_ref[g, HEAD_DIM:, :] = jnp.where(row0, -bound, 0.0).astype(BF16)
    ones_col = (lax.broadcasted_iota(jnp.int32, (tk, HEAD_DIM), 1) == 0).astype(BF16)

    def probs_into(c, slot):
        ka = jnp.concatenate([k_chunk(c), ones_col], axis=1)
        for g in heads:
            p_ref[slot][g] = jnp.exp2(jnp.dot(ka, qa_ref[g], preferred_element_type=F32)).astype(BF16)

    def pv_add(c, slot):
        vc = vt_ref[c]
        for g in heads:
            acc_ref[g] += jnp.dot(vc, p_ref[slot][g], preferred_element_type=F32)

    acc_ref[...] = jnp.zeros(acc_ref.shape, F32)
    probs_into(0, 0)

    def fast_pair(i, carry):
        probs_into(2 * i + 1, 1)
        pv_add(2 * i, 0)
        probs_into(2 * i + 2, 0)
        pv_add(2 * i + 1, 1)
        return carry

    lax.fori_loop(0, (nchunk - 2) // 2, fast_pair, 0)
    probs_into(nchunk - 1, 1)
    pv_add(nchunk - 2, 0)
    pv_add(nchunk - 1, 1)
    write_out()
    den_min = acc_ref[0][HEAD_DIM:HEAD_DIM + 1]
    for g in heads[1:]:
        den_min = jnp.minimum(den_min, acc_ref[g][HEAD_DIM:HEAD_DIM + 1])
    safe = jnp.min(den_min) > DENOMINATOR_GUARD

    def scores_into(c, slot):
        kc = k_chunk(c)
        for g in heads:
            sc = jnp.dot(kc, qt_ref[g], preferred_element_type=F32)
            s_ref[slot][g] = sc
            parts = [jnp.max(sc[r * (tk // 4):(r + 1) * (tk // 4)], axis=0, keepdims=True) for r in range(4)]
            mx_ref[slot][g] = jnp.maximum(jnp.maximum(parts[0], parts[1]), jnp.maximum(parts[2], parts[3]))

    def softmax(slot):
        for g in heads:
            m_old = m_ref[g]
            m_new = jnp.maximum(m_old, mx_ref[slot][g])
            al_ref[slot][g] = jnp.exp2(m_old - m_new)
            m_ref[g] = m_new
            p_ref[slot][g] = jnp.exp2(s_ref[slot][g] - m_new).astype(BF16)

    def pv_update(c, slot):
        vc = vt_ref[c]
        for g in heads:
            acc_ref[g] = al_ref[slot][g] * acc_ref[g] + jnp.dot(vc, p_ref[slot][g],
                                                                preferred_element_type=F32)

    def step(c, slot):
        scores_into(c + 1, 1 - slot)
        pv_update(c - 1, 1 - slot)
        softmax(slot)

    @pl.when(jnp.logical_not(safe))
    def _():
        for g in heads:
            qt_ref[g] = qa_ref[g, :HEAD_DIM, :]
        scores_into(0, 0)
        m_ref[...] = jnp.full(m_ref.shape, -jnp.inf, F32)
        acc_ref[...] = jnp.zeros(acc_ref.shape, F32)
        softmax(0)
        scores_into(1, 1)

        def pair(i, carry):
            step(2 * i + 1, 1)
            step(2 * i + 2, 0)
            return carry

        lax.fori_loop(0, (nchunk - 2) // 2, pair, 0)
        pv_update(nchunk - 2, 0)
        softmax(1)
        pv_update(nchunk - 1, 1)
        write_out()


def _attention(q, k, vt, k_norm_bound, tq):
    bsz, s, _ = q.shape
    rows, tk = vt.shape[3], vt.shape[4]
    nchunk = s // tk
    assert nchunk % 2 == 0 and nchunk >= 4
    gw = Q_GROUP * HEAD_DIM
    qspec = pl.BlockSpec((None, tq, gw), lambda b, h, i: (b, i, h))
    kspec = pl.BlockSpec((None, s, HEAD_DIM), lambda b, h, i: (b, 0, h))
    vspec = pl.BlockSpec((None, None, nchunk, rows, tk), lambda b, h, i: (b, h, 0, 0, 0))
    head_vec = pltpu.VMEM((Q_GROUP, 1, tq), F32)
    return pl.pallas_call(
        _attn_kernel,
        grid=(bsz, N_KV_HEADS, s // tq),
        in_specs=[pl.BlockSpec((1, 1), lambda b, h, i: (0, 0)), qspec, kspec, vspec],
        out_specs=qspec,
        out_shape=jax.ShapeDtypeStruct((bsz, s, Q_DIM), BF16),
        scratch_shapes=[head_vec, pltpu.VMEM((Q_GROUP, rows, tq), F32), head_vec, head_vec]
        + [pltpu.VMEM((Q_GROUP, tk, tq), F32)] * 2 + [pltpu.VMEM((Q_GROUP, tk, tq), BF16)] * 2
        + [pltpu.VMEM((Q_GROUP, HEAD_DIM, tq), BF16), head_vec, head_vec,
           pltpu.VMEM((Q_GROUP, 2 * HEAD_DIM, tq), BF16)],
        compiler_params=_cparams(("parallel", "parallel", "parallel")),
        name="attn",
    )(k_norm_bound, q, k, vt)


def _mix_kernel(yf_ref, at_ref, gf_ref, ga_ref, x_ref, g1_ref, sc2_ref, sh2_ref,
                wf_ref, wo_ref, wout_ref, l1g_ref, l1b_ref, wr_ref, br_ref, tri_ref,
                x1_ref, u2_ref, idx_ref, wgt_ref, rank_ref, cnt_ref, cnt_acc):
    first = jnp.logical_and(pl.program_id(0) == 0, pl.program_id(1) == 0)

    @pl.when(first)
    def _():
        cnt_acc[...] = jnp.zeros_like(cnt_acc)

    tm_all = x_ref.shape[0]
    rg = tm_all // MIX_ROW_GROUPS
    u_his, u_los = [], []
    groups = [slice(i * rg, (i + 1) * rg) for i in range(MIX_ROW_GROUPS)]
    ms = []
    for rows in groups:
        yf = jnp.dot(yf_ref[rows, :], wf_ref[...], preferred_element_type=F32)
        ya = jnp.dot(at_ref[rows, :], wo_ref[...], preferred_element_type=F32)
        ms.append((gf_ref[rows, :].astype(F32) * yf + ga_ref[rows, :].astype(F32) * ya).astype(BF16))
    hs = [jnp.dot(m, wout_ref[...], preferred_element_type=F32) for m in ms]
    for rows, h in zip(groups, hs):
        r = DEEPNORM_ALPHA * x_ref[rows, :] + g1_ref[...] * h
        x1 = _layer_norm(r, LN_EPS) * l1g_ref[...] + l1b_ref[...]
        x1_ref[rows, :] = x1
        u2 = _layer_norm(x1, ADA_EPS) * (1.0 + sc2_ref[...]) + sh2_ref[...]
        u2_ref[rows, :] = _pack_bf16_pairs(u2)
        u_hi = u2.astype(BF16)
        u_his.append(u_hi)
        u_los.append((u2 - u_hi.astype(F32)).astype(BF16))

    nt = (((1,), (1,)), ((), ()))
    u_hi = jnp.concatenate(u_his, axis=0)
    u_lo = jnp.concatenate(u_los, axis=0)
    w = wr_ref[...]
    w_hi = w.astype(BF16)
    w_lo = (w - w_hi.astype(F32)).astype(BF16)
    logits = (lax.dot_general(w_hi, u_hi, nt, preferred_element_type=F32)
              + lax.dot_general(w_lo, u_hi, nt, preferred_element_type=F32)
              + lax.dot_general(w_hi, u_lo, nt, preferred_element_type=F32)) + br_ref[...]

    ne, tm = logits.shape
    eid = lax.broadcasted_iota(jnp.int32, (ne, tm), 0)
    work = logits
    vals, idxs, hots = [], [], []
    for _ in range(TOP_K):
        mx = jnp.max(work, axis=0, keepdims=True)
        ix = jnp.min(jnp.where(work == mx, eid, ne), axis=0, keepdims=True)
        hot = eid == ix
        work = jnp.where(hot, -jnp.inf, work)
        vals.append(mx)
        idxs.append(ix)
        hots.append(hot)
    ex = [jnp.exp(v - vals[0]) for v in vals]
    den = ex[0] + ex[1] + ex[2] + ex[3]
    wgt_ref[...] = jnp.concatenate([e / den for e in ex], axis=0)
    idx_ref[...] = jnp.concatenate(idxs, axis=0)

    sel = hots[0] | hots[1] | hots[2] | hots[3]
    mask = jnp.where(sel, 1.0, 0.0)
    prefix = jnp.dot(mask.astype(BF16), tri_ref[...], preferred_element_type=F32)
    pos = prefix + cnt_acc[...]
    ranks = [jnp.sum(jnp.where(hot, pos, 0.0), axis=0, keepdims=True) for hot in hots]
    rank_ref[...] = jnp.concatenate(ranks, axis=0).astype(jnp.int32)
    cnt_acc[...] += jnp.sum(mask, axis=1, keepdims=True)
    cnt_ref[...] = cnt_acc[...].astype(jnp.int32)


def _mix(yfm, attn, gf, ga, x, g1, sc2, sh2, wf, wo, wout, l1g, l1b, wr_t, br, tm, b0, bsz):
    _, s, d = x.shape
    n = bsz * s
    nt = s // tm
    tri = (jnp.arange(tm)[:, None] < jnp.arange(tm)[None, :]).astype(BF16)
    row_in = lambda w: pl.BlockSpec((None, tm, w), lambda b, i: (b + b0, i, 0))
    row = lambda w: pl.BlockSpec((None, tm, w), lambda b, i: (b, i, 0))
    mod = pl.BlockSpec((None, 1, d), lambda b, i: (b + b0, 0, 0))
    full = lambda shp: pl.BlockSpec(shp, lambda b, i: (0,) * len(shp))
    tok = pl.BlockSpec((TOP_K, tm), lambda b, i: (0, b * nt + i))
    return pl.pallas_call(
        _mix_kernel,
        grid=(bsz, nt),
        in_specs=[row_in(FOURIER_DIM), row_in(Q_DIM), row_in(d), row_in(d), row_in(d), mod, mod, mod,
                  full(wf.shape), full(wo.shape), full(wout.shape), full((1, d)), full((1, d)),
                  full(wr_t.shape), full((N_EXPERTS, 1)), full((tm, tm))],
        out_specs=[row(d), row(d // 2), tok, tok, tok, full((N_EXPERTS, 1))],
        out_shape=[jax.ShapeDtypeStruct((bsz, s, d), F32),
                   jax.ShapeDtypeStruct((bsz, s, d // 2), jnp.int32),
                   jax.ShapeDtypeStruct((TOP_K, n), jnp.int32),
                   jax.ShapeDtypeStruct((TOP_K, n), F32),
                   jax.ShapeDtypeStruct((TOP_K, n), jnp.int32),
                   jax.ShapeDtypeStruct((N_EXPERTS, 1), jnp.int32)],
        scratch_shapes=[pltpu.VMEM((N_EXPERTS, 1), F32)],
        compiler_params=_cparams(("arbitrary", "arbitrary")),
        name="mix",
    )(yfm, attn, gf, ga, x, g1, sc2, sh2, wf, wo, wout, l1g, l1b, wr_t, br, tri)


def _expert_kernel(be_ref, nv_ref, new_ref, xs_ref, wgu_ref, bgu_ref, wd_ref, bd_ref, ys_ref,
                   wgu_bf, wd_bf):
    del be_ref
    i = pl.program_id(0)
    nvalid = nv_ref[i]

    @pl.when(new_ref[i] == 1)
    def _():
        wgu_bf[...] = wgu_ref[...].astype(BF16)
        wd_bf[...] = wd_ref[...].astype(BF16)

    @pl.when(nvalid > 0)
    def _():
        xw = xs_ref[...]
        row = lax.broadcasted_iota(jnp.int32, xw.shape, 0)
        lo, hi = _unpack_bf16_pairs(jnp.where(row < nvalid, xw, 0))
        xb = jnp.concatenate([lo.astype(BF16), hi.astype(BF16)], axis=1)
        h = jnp.dot(xb, wgu_bf[...], preferred_element_type=F32) + bgu_ref[...]
        gate = jnp.minimum(h[:, :D_EXPERT], SWIGLU_LIMIT)
        up = jnp.clip(h[:, D_EXPERT:], -SWIGLU_LIMIT, SWIGLU_LIMIT)
        act = (up + 1.0) * gate * jax.nn.sigmoid(SWIGLU_ALPHA * gate)
        y = jnp.dot(act.astype(BF16), wd_bf[...], preferred_element_type=F32) + bd_ref[...]
        ys_ref[...] = _pack_bf16_pairs(y)

    @pl.when(nvalid == 0)
    def _():
        ys_ref[...] = jnp.zeros_like(ys_ref)


def _experts(block_e, block_nvalid, block_new, xs, wgu, bgu, wd, bd, tmb):
    n_pad, dh = xs.shape
    d = 2 * dh
    grid_spec = pltpu.PrefetchScalarGridSpec(
        num_scalar_prefetch=3,
        grid=(n_pad // tmb,),
        in_specs=[pl.BlockSpec((tmb, dh), lambda i, be, nv, nw: (i, 0)),
                  pl.BlockSpec((None, d, 2 * D_EXPERT), lambda i, be, nv, nw: (be[i], 0, 0)),
                  pl.BlockSpec((None, 1, 2 * D_EXPERT), lambda i, be, nv, nw: (be[i], 0, 0)),
                  pl.BlockSpec((None, D_EXPERT, d), lambda i, be, nv, nw: (be[i], 0, 0)),
                  pl.BlockSpec((None, 1, d), lambda i, be, nv, nw: (be[i], 0, 0))],
        out_specs=pl.BlockSpec((tmb, dh), lambda i, be, nv, nw: (i, 0)),
        scratch_shapes=[pltpu.VMEM((d, 2 * D_EXPERT), BF16), pltpu.VMEM((D_EXPERT, d), BF16)],
    )
    return pl.pallas_call(
        _expert_kernel,
        grid_spec=grid_spec,
        out_shape=jax.ShapeDtypeStruct((n_pad, dh), jnp.int32),
        compiler_params=_cparams(("arbitrary",)),
        name="experts",
    )(block_e, block_nvalid, block_new, xs, wgu, bgu, wd, bd)


def _final_kernel(yg_ref, w_ref, x1_ref, g2_ref, lg_ref, lb_ref, o_ref):
    w = w_ref[...]
    acc_lo = None
    for j in range(TOP_K):
        lo, hi = _unpack_bf16_pairs(yg_ref[j])
        wj = w[:, j:j + 1]
        acc_lo = lo * wj if acc_lo is None else acc_lo + lo * wj
        acc_hi = hi * wj if j == 0 else acc_hi + hi * wj
    h = jnp.concatenate([acc_lo, acc_hi], axis=1)
    r = DEEPNORM_ALPHA * x1_ref[...] + g2_ref[...] * h
    o_ref[...] = _layer_norm(r, LN_EPS) * lg_ref[...] + lb_ref[...]


def _final(yg, w_tok, x1, g2, lg, lb, tm, b0, total_bsz, out_so_far):
    bsz, s, d = x1.shape
    nt = s // tm
    row = pl.BlockSpec((None, tm, d), lambda b, i: (b, i, 0))
    full = pl.BlockSpec((1, d), lambda b, i: (0, 0))
    in_specs = [pl.BlockSpec((TOP_K, tm, d // 2), lambda b, i: (0, b * nt + i, 0)),
                pl.BlockSpec((tm, TOP_K), lambda b, i: (b * nt + i, 0)),
                row, pl.BlockSpec((None, 1, d), lambda b, i: (b + b0, 0, 0)), full, full]
    args = [yg, w_tok, x1, g2, lg, lb]
    aliases = {}
    if out_so_far is not None:
        in_specs.append(pl.BlockSpec(memory_space=pl.ANY))
        args.append(out_so_far)
        aliases = {len(args) - 1: 0}
    body = _final_kernel if out_so_far is None else (lambda *refs: _final_kernel(*refs[:6], refs[7]))
    return pl.pallas_call(
        body,
        grid=(bsz, nt),
        in_specs=in_specs,
        out_specs=pl.BlockSpec((None, tm, d), lambda b, i: (b + b0, i, 0)),
        out_shape=jax.ShapeDtypeStruct((total_bsz, s, d), F32),
        input_output_aliases=aliases,
        compiler_params=_cparams(("parallel", "parallel")),
        name="final",
    )(*args)


def _scatter_rows(rows, dest, m_out):
    n, w = rows.shape
    nj = dest.shape[0]
    ch = GATHER_CHUNK
    nw = SC_CORES * SC_SUBCORES
    nch = n // (nw * ch)
    assert n == nw * nch * ch and nch % 2 == 0
    per_w = nch * ch
    mesh = plsc.VectorSubcoreMesh(core_axis_name="c", subcore_axis_name="s",
                                  num_cores=SC_CORES, num_subcores=SC_SUBCORES)

    @functools.partial(
        pl.kernel, mesh=mesh,
        out_type=jax.ShapeDtypeStruct((m_out, w), rows.dtype),
        scratch_types=[pltpu.VMEM((nj, nch, ch), jnp.int32), pltpu.VMEM((2, ch, w), rows.dtype),
                       pltpu.SemaphoreType.DMA((2,)), pltpu.SemaphoreType.DMA((2,))],
    )
    def scatter_kernel(rows_hbm, dest_hbm, out_hbm, idx_v, rows_v, rsem, ssem):
        wid = lax.axis_index("s") * SC_CORES + lax.axis_index("c")
        base = wid * per_w
        for j in range(nj):
            pltpu.sync_copy(dest_hbm.at[j, wid], idx_v.at[j])

        def read(i, b):
            off = pl.multiple_of(base + i * ch, ch)
            return pltpu.make_async_copy(rows_hbm.at[pl.ds(off, ch)], rows_v.at[b], rsem.at[b])

        def scatter(i, b, j):
            return pltpu.make_async_copy(rows_v.at[b], out_hbm.at[idx_v.at[j, i]], ssem.at[b])

        read(0, 0).start()

        @pl.loop(0, nch, step=2)
        def _(i0):
            for b in range(2):
                i = i0 + b
                read(i, b).wait()

                @pl.when(i + 1 < nch)
                def _():
                    @pl.when(i >= 1)
                    def _():
                        for j in range(nj):
                            scatter(i - 1, 1 - b, j).wait()
                    read(i + 1, 1 - b).start()

                for j in range(nj):
                    scatter(i, b, j).start()

        for b in range(2):
            for j in range(nj):
                scatter(nch - 2 + b, b, j).wait()

    return scatter_kernel(rows, dest.reshape(nj, nw, nch, ch))


def _gather_rows(table, idx):
    m = idx.shape[0]
    w = table.shape[1]
    ch = GATHER_CHUNK
    nw = SC_CORES * SC_SUBCORES
    nch = m // (nw * ch)
    assert m == nw * nch * ch and nch % 2 == 0
    per_w = nch * ch
    mesh = plsc.VectorSubcoreMesh(core_axis_name="c", subcore_axis_name="s",
                                  num_cores=SC_CORES, num_subcores=SC_SUBCORES)

    @functools.partial(
        pl.kernel, mesh=mesh,
        out_type=jax.ShapeDtypeStruct((m, w), table.dtype),
        scratch_types=[pltpu.VMEM((nch, ch), jnp.int32), pltpu.VMEM((2, ch, w), table.dtype),
                       pltpu.SemaphoreType.DMA((2,)), pltpu.SemaphoreType.DMA((2,))],
    )
    def gather_kernel(table_hbm, idx_hbm, out_hbm, idx_v, rows_v, gsem, wsem):
        wid = lax.axis_index("s") * SC_CORES + lax.axis_index("c")
        base = wid * per_w
        pltpu.sync_copy(idx_hbm.at[wid], idx_v)

        def gather(j, b):
            return pltpu.make_async_copy(table_hbm.at[idx_v.at[j]], rows_v.at[b], gsem.at[b])

        def write(j, b):
            off = pl.multiple_of(base + j * ch, ch)
            return pltpu.make_async_copy(rows_v.at[b], out_hbm.at[pl.ds(off, ch)], wsem.at[b])

        gather(0, 0).start()

        @pl.loop(0, nch, step=2)
        def _(i):
            for b in range(2):
                j = i + b
                gather(j, b).wait()

                @pl.when(j + 1 < nch)
                def _():
                    @pl.when(j >= 1)
                    def _():
                        write(j - 1, 1 - b).wait()
                    gather(j + 1, 1 - b).start()

                write(j, b).start()

        write(nch - 2, 0).wait()
        write(nch - 1, 1).wait()

    return gather_kernel(table, idx.reshape(nw, nch, ch))


def kernel(x, c, w_ada, b_ada, w_in, q_norm_g, k_norm_g, w_fourier, w_attn_o, w_out, ln1_g, ln1_b,
           w_router, b_router, w_gate_up, b_gate_up, w_down, b_down, ln2_g, ln2_b):
    bsz, s, d = x.shape
    n = bsz * s
    tm = min(ROW_TILE, s)
    l = 0

    mod = _ada(c, w_ada[l], b_ada[l])
    sh1, sc1, g1, sh2, sc2, g2 = [m[:, None, :] for m in jnp.split(mod, N_MOD, axis=-1)]

    dft_c, m1, m2, cw, sw, p, q = _dft_tables(s)
    cos, sin_a, sin_b = _rope_tables(s)

    fa, fb, qh, kh, vt, gf, ga = _inproj(
        x, sc1, sh1, w_in[l].astype(BF16), dft_c,
        q_norm_g[l].reshape(1, HEAD_DIM), k_norm_g[l].reshape(1, HEAD_DIM), cos, sin_a, sin_b, tm)

    zr, zi = _fft1(fa, fb, m1, cw, sw, p, q)
    yfm = _fft2(zr, zi, m2)

    k_bound = (math.sqrt(HEAD_DIM) * (1.0 + 2.0 ** -7)) * jnp.max(jnp.abs(k_norm_g[l])).reshape(1, 1)
    attn = _attention(qh, kh, vt, k_bound, min(ATTN_TQ, s))

    wf_b, wo_b, wout_b = w_fourier[l].astype(BF16), w_attn_o[l].astype(BF16), w_out[l].astype(BF16)
    bgu, bd = b_gate_up[l][:, None, :], b_down[l][:, None, :]
    tmb = MOE_BLOCK
    eids = jnp.arange(N_EXPERTS, dtype=jnp.int32)

    ngroups = MOE_BATCH_GROUPS if bsz % MOE_BATCH_GROUPS == 0 else 1
    gb = bsz // ngroups
    ng = gb * s
    out = None
    for grp in range(ngroups):
        b0 = grp * gb
        x1, u2p, idx_t, wgt_t, rank_t, counts = _mix(
            yfm, attn, gf, ga, x, g1, sc2, sh2, wf_b, wo_b, wout_b,
            ln1_g[l].reshape(1, d), ln1_b[l].reshape(1, d),
            w_router[l].T, b_router[l].reshape(N_EXPERTS, 1), tm, b0, gb)

        counts = counts.reshape(N_EXPERTS)
        pcounts = ((counts + tmb - 1) // tmb) * tmb
        pends = jnp.cumsum(pcounts)
        pstarts = pends - pcounts
        n_pad = ng * TOP_K + N_EXPERTS * tmb
        nblk = n_pad // tmb
        start_of = jnp.sum(jnp.where(idx_t[None] == eids[:, None, None], pstarts[:, None, None], 0), axis=0)
        dest_t = start_of + rank_t
        block_starts = jnp.arange(nblk, dtype=jnp.int32) * tmb
        block_e = jnp.sum((block_starts[:, None] >= pends[None, :]).astype(jnp.int32), axis=1)
        block_e = jnp.minimum(block_e, N_EXPERTS - 1)
        block_nvalid = jnp.clip(pstarts[block_e] + counts[block_e] - block_starts, 0, tmb)
        block_new = jnp.concatenate([jnp.ones((1,), jnp.int32),
                                     (block_e[1:] != block_e[:-1]).astype(jnp.int32)])

        xs = _scatter_rows(u2p.reshape(ng, d // 2), dest_t, n_pad)
        ys = _experts(block_e, block_nvalid, block_new, xs, w_gate_up[l], bgu, w_down[l], bd, tmb)
        yg = _gather_rows(ys, dest_t.reshape(-1)).reshape(TOP_K, ng, d // 2)
        out = _final(yg, wgt_t.T, x1, g2, ln2_g[l].reshape(1, d), ln2_b[l].reshape(1, d), tm,
                     b0, bsz, out)
    return out
```

```python
import functools
import math

import jax
import jax.numpy as jnp
import numpy as np
from jax import lax
from jax.experimental import pallas as pl
from jax.experimental.pallas import tpu as pltpu
from jax.experimental.pallas import tpu_sc as plsc

F32 = jnp.float32
BF16 = jnp.bfloat16

D_MODEL = 1024
GRID_W = 64
N_Q_HEADS = 8
N_KV_HEADS = 2
HEAD_DIM = 128
Q_GROUP = N_Q_HEADS // N_KV_HEADS
ROPE_THETA = 10000.0
ROPE_AXIS_DIM = HEAD_DIM // 2
N_FOURIER_GROUPS = 4
FOURIER_GROUP_DIM = 128
FOURIER_DIM = N_FOURIER_GROUPS * FOURIER_GROUP_DIM
Q_DIM = N_Q_HEADS * HEAD_DIM
KV_DIM = N_KV_HEADS * HEAD_DIM
N_EXPERTS = 32
TOP_K = 4
D_EXPERT = 1024
SWIGLU_LIMIT = 7.0
SWIGLU_ALPHA = 1.702
N_MOD = 6
DEPTH = 1
DEEPNORM_ALPHA = (2 * DEPTH) ** 0.25
LN_EPS = 1e-5
ADA_EPS = 1e-6
QK_EPS = 1e-6
LOG2E = 1.4426950408889634

V7X_VMEM_BYTES = 64 * 1024 * 1024
VMEM_LIMIT = V7X_VMEM_BYTES - 8 * 1024 * 1024
LANES = 128
BF16_SUBLANES = 16

FFT_Q = 128
ROW_TILE = 512
ATTN_TQ = 512
ONES_ROWS = 16
ATTN_FAST_KCHUNKS = 2
DENOMINATOR_GUARD = 2.0 ** -80
MIX_ROW_GROUPS = 2
MOE_BLOCK = 512
MOE_BATCH_GROUPS = 2

SC_CORES = 2
SC_SUBCORES = 16
GATHER_CHUNK = 64


def _cparams(sem):
    return pltpu.CompilerParams(dimension_semantics=sem, vmem_limit_bytes=VMEM_LIMIT)


def _layer_norm(x, eps):
    mu = jnp.mean(x, axis=-1, keepdims=True)
    xc = x - mu
    var = jnp.mean(xc * xc, axis=-1, keepdims=True)
    return xc * lax.rsqrt(var + eps)


def _pack_bf16_pairs(x):
    n = x.shape[1] // 2
    bits = pltpu.bitcast(x.astype(BF16).astype(F32), jnp.uint32)
    word = (bits[:, :n] >> 16) | (bits[:, n:] & jnp.uint32(0xFFFF0000))
    return pltpu.bitcast(word, jnp.int32)


def _unpack_bf16_pairs(w):
    bits = pltpu.bitcast(w, jnp.uint32)
    lo = pltpu.bitcast(bits << 16, F32)
    hi = pltpu.bitcast(bits & jnp.uint32(0xFFFF0000), F32)
    return lo, hi


def _ada_kernel(c_ref, w_ref, b_ref, o_ref):
    c = c_ref[...]
    cond = c * jax.nn.sigmoid(c)
    o_ref[...] = jnp.dot(cond, w_ref[...], preferred_element_type=F32,
                         precision=lax.Precision.HIGHEST) + b_ref[...]


def _ada(c, w_ada, b_ada):
    bsz, d = c.shape
    n = w_ada.shape[1]
    tn = 1536
    return pl.pallas_call(
        _ada_kernel,
        grid=(n // tn,),
        in_specs=[pl.BlockSpec((bsz, d), lambda j: (0, 0)),
                  pl.BlockSpec((d, tn), lambda j: (0, j)),
                  pl.BlockSpec((1, tn), lambda j: (0, j))],
        out_specs=pl.BlockSpec((bsz, tn), lambda j: (0, j)),
        out_shape=jax.ShapeDtypeStruct((bsz, n), F32),
        compiler_params=_cparams(("arbitrary",)),
        name="ada",
    )(c, w_ada, b_ada.reshape(1, n))


def _rope(xh, cos, sin_a, sin_b):
    a = pltpu.roll(xh, 32, 1)
    b = pltpu.roll(xh, 96, 1)
    return xh * cos + b * sin_b + a * sin_a


def _inproj_kernel(x_ref, sc_ref, sh_ref, w_ref, dft_ref, qg_ref, kg_ref,
                   cos_ref, sa_ref, sb_ref,
                   fa_ref, fb_ref, q_ref, k_ref, vt_ref, gf_ref, ga_ref):
    x = x_ref[...]
    u = _layer_norm(x, ADA_EPS) * (1.0 + sc_ref[...]) + sh_ref[...]
    ub = u.astype(BF16)
    cos, sa, sb = cos_ref[...], sa_ref[...], sb_ref[...]

    c0 = 0
    zf = jnp.dot(ub, w_ref[:, c0:c0 + FOURIER_DIM], preferred_element_type=F32).astype(BF16)
    for g in range(N_FOURIER_GROUPS):
        sl = slice(g * FOURIER_GROUP_DIM, (g + 1) * FOURIER_GROUP_DIM)
        ab = jnp.dot(zf[:, sl], dft_ref[...], preferred_element_type=F32)
        fa_ref[:, sl] = ab[:, :FOURIER_GROUP_DIM].astype(BF16)
        fb_ref[:, sl] = ab[:, FOURIER_GROUP_DIM:].astype(BF16)
    c0 += FOURIER_DIM

    def norm_rope(z, gain, scale):
        ms = jnp.mean(z * z, axis=-1, keepdims=True)
        zn = z * lax.rsqrt(ms + QK_EPS) * gain
        return (_rope(zn, cos, sa, sb) * scale).astype(BF16)

    q_scale = LOG2E / math.sqrt(HEAD_DIM)
    zq = jnp.dot(ub, w_ref[:, c0:c0 + Q_DIM], preferred_element_type=F32)
    for h in range(N_Q_HEADS):
        sl = slice(h * HEAD_DIM, (h + 1) * HEAD_DIM)
        q_ref[:, sl] = norm_rope(zq[:, sl], qg_ref[...], q_scale)
    c0 += Q_DIM

    zk = jnp.dot(ub, w_ref[:, c0:c0 + KV_DIM], preferred_element_type=F32)
    for h in range(N_KV_HEADS):
        sl = slice(h * HEAD_DIM, (h + 1) * HEAD_DIM)
        k_ref[:, sl] = norm_rope(zk[:, sl], kg_ref[...], 1.0)
    c0 += KV_DIM

    zv = jnp.dot(ub, w_ref[:, c0:c0 + KV_DIM], preferred_element_type=F32)
    for h in range(N_KV_HEADS):
        vt_ref[h, :HEAD_DIM, :] = zv[:, h * HEAD_DIM:(h + 1) * HEAD_DIM].T.astype(BF16)
        vt_ref[h, HEAD_DIM:, :] = jnp.ones((ONES_ROWS, zv.shape[0]), BF16)
    c0 += KV_DIM

    zg = jnp.dot(ub, w_ref[:, c0:c0 + D_MODEL], preferred_element_type=F32)
    gf_ref[...] = jax.nn.sigmoid(zg).astype(BF16)
    c0 += D_MODEL
    zg = jnp.dot(ub, w_ref[:, c0:c0 + D_MODEL], preferred_element_type=F32)
    ga_ref[...] = jax.nn.sigmoid(zg).astype(BF16)


def _inproj(x, sc1, sh1, w_in, dft_c, qg, kg, cos, sin_a, sin_b, tm):
    bsz, s, d = x.shape
    ncol = w_in.shape[1]
    row = lambda w: pl.BlockSpec((None, tm, w), lambda b, i: (b, i, 0))
    mod = pl.BlockSpec((None, 1, d), lambda b, i: (b, 0, 0))
    full = lambda shp: pl.BlockSpec(shp, lambda b, i: (0,) * len(shp))
    tab = pl.BlockSpec((tm, HEAD_DIM), lambda b, i: (i, 0))
    vt_rows = HEAD_DIM + ONES_ROWS
    vt_spec = pl.BlockSpec((None, N_KV_HEADS, None, vt_rows, tm), lambda b, i: (b, 0, i, 0, 0))
    vt_shape = jax.ShapeDtypeStruct((bsz, N_KV_HEADS, s // tm, vt_rows, tm), BF16)
    rows_out = lambda w: (row(w), jax.ShapeDtypeStruct((bsz, s, w), BF16))
    outs = [rows_out(FOURIER_DIM), rows_out(FOURIER_DIM), rows_out(Q_DIM), rows_out(KV_DIM),
            (vt_spec, vt_shape), rows_out(D_MODEL), rows_out(D_MODEL)]
    return pl.pallas_call(
        _inproj_kernel,
        grid=(bsz, s // tm),
        in_specs=[row(d), mod, mod, full((d, ncol)), full(dft_c.shape),
                  full((1, HEAD_DIM)), full((1, HEAD_DIM)), tab, tab, tab],
        out_specs=[o[0] for o in outs],
        out_shape=[o[1] for o in outs],
        compiler_params=_cparams(("parallel", "parallel")),
        name="inproj",
    )(x, sc1, sh1, w_in, dft_c, qg, kg, cos, sin_a, sin_b)


def _fft1_kernel(a_ref, b_ref, m_ref, cw_ref, sw_ref, zr_ref, zi_ref, a32, b32):
    p, nb, _ = a_ref.shape
    a32[...] = a_ref[...].astype(F32)
    b32[...] = b_ref[...].astype(F32)
    for j in range(nb):
        ab = jnp.concatenate([a32[:, j, :], b32[:, j, :]], axis=0).astype(BF16)
        t = jnp.dot(m_ref[...], ab, preferred_element_type=F32)
        tr, ti = t[:p], t[p:]
        cw, sw = cw_ref[j], sw_ref[j]
        zr_ref[j] = (tr * cw - ti * sw).astype(BF16)
        zi_ref[j] = (ti * cw + tr * sw).astype(BF16)


def _fft1(fa, fb, m1, cw, sw, p, q):
    bsz, s, c = fa.shape
    nb = BF16_SUBLANES
    inp = pl.BlockSpec((None, p, nb, c), lambda b, j: (b, 0, j, 0))
    tw = pl.BlockSpec((nb, p, 1), lambda b, j: (j, 0, 0))
    out = pl.BlockSpec((None, nb, p, c), lambda b, j: (b, j, 0, 0))
    return pl.pallas_call(
        _fft1_kernel,
        grid=(bsz, q // nb),
        in_specs=[inp, inp, pl.BlockSpec(m1.shape, lambda b, j: (0, 0)), tw, tw],
        out_specs=[out, out],
        out_shape=[jax.ShapeDtypeStruct((bsz, q, p, c), BF16)] * 2,
        scratch_shapes=[pltpu.VMEM((p, nb, c), F32)] * 2,
        compiler_params=_cparams(("parallel", "parallel")),
        name="fft1",
    )(fa.reshape(bsz, p, q, c), fb.reshape(bsz, p, q, c), m1, cw, sw)


def _fft2_kernel(zr_ref, zi_ref, m_ref, y_ref, zr32, zi32, y32):
    zr32[...] = zr_ref[...].astype(F32)
    zi32[...] = zi_ref[...].astype(F32)
    for j in range(zr_ref.shape[1]):
        z = jnp.concatenate([zr32[:, j, :], zi32[:, j, :]], axis=0).astype(BF16)
        y32[:, j, :] = jnp.dot(m_ref[...], z, preferred_element_type=F32)
    y_ref[...] = y32[...].astype(BF16)


def _fft2(zr, zi, m2):
    bsz, q, p, c = zr.shape
    pc = BF16_SUBLANES
    blk = pl.BlockSpec((None, q, pc, c), lambda b, j: (b, 0, j, 0))
    y = pl.pallas_call(
        _fft2_kernel,
        grid=(bsz, p // pc),
        in_specs=[blk, blk, pl.BlockSpec(m2.shape, lambda b, j: (0, 0))],
        out_specs=blk,
        out_shape=jax.ShapeDtypeStruct((bsz, q, p, c), BF16),
        scratch_shapes=[pltpu.VMEM((q, pc, c), F32)] * 3,
        compiler_params=_cparams(("parallel", "parallel")),
        name="fft2",
    )(zr, zi, m2)
    return y.reshape(bsz, q * p, c)


def _dft_tables(s):
    q = FFT_Q
    p = s // q

    def cs(n_rows, n_cols, n):
        r = np.arange(n_rows, dtype=np.int64)[:, None]
        c = np.arange(n_cols, dtype=np.int64)[None, :]
        ang = ((r * c) % n).astype(np.float64) * (2.0 * math.pi / n)
        return np.cos(ang).astype(np.float32), np.sin(ang).astype(np.float32)

    cc, sc = cs(FOURIER_GROUP_DIM, FOURIER_GROUP_DIM, FOURIER_GROUP_DIM)
    dft_c = jnp.asarray(np.concatenate([cc, sc], axis=1), dtype=BF16)
    cp, sp = cs(p, p, p)
    m1 = jnp.asarray(np.block([[cp, -sp], [sp, cp]]), dtype=BF16)
    cq, sq = cs(q, q, q)
    scale = np.float32(1.0 / math.sqrt(s * FOURIER_GROUP_DIM))
    m2 = jnp.asarray(np.concatenate([cq, -sq], axis=1) * scale, dtype=BF16)
    cw, sw = cs(q, p, s)
    return dft_c, m1, m2, jnp.asarray(cw[:, :, None]), jnp.asarray(sw[:, :, None]), p, q


def _rope_tables(s):
    f32 = np.float32
    rows = s // GRID_W
    row_ids = np.repeat(np.arange(rows, dtype=f32), GRID_W)
    col_ids = np.tile(np.arange(GRID_W, dtype=f32), rows)
    freqs = (f32(ROPE_THETA) ** (-np.arange(0, ROPE_AXIS_DIM, 2, dtype=f32) / f32(ROPE_AXIS_DIM))).astype(f32)
    ang_r = (row_ids[:, None] * freqs).astype(f32)
    ang_c = (col_ids[:, None] * freqs).astype(f32)
    cr, sr, cc, sc = np.cos(ang_r), np.sin(ang_r), np.cos(ang_c), np.sin(ang_c)
    z = np.zeros_like(sr)
    cos = np.concatenate([cr, cr, cc, cc], axis=1)
    sin_b = np.concatenate([-sr, z, -sc, z], axis=1)
    sin_a = np.concatenate([z, sr, z, sc], axis=1)
    return jnp.asarray(cos, F32), jnp.asarray(sin_a, F32), jnp.asarray(sin_b, F32)


def _attn_kernel(kb_ref, q_ref, k_ref, vt_ref, o_ref, m_ref, acc_ref, al0, al1, s0, s1, p0, p1,
                 qt_ref, mx0, mx1, qa_ref):
    al_ref, s_ref, p_ref, mx_ref = (al0, al1), (s0, s1), (p0, p1), (mx0, mx1)
    nchunk = vt_ref.shape[0]
    tk = vt_ref.shape[2]
    tq = q_ref.shape[0]
    heads = range(Q_GROUP)

    def k_chunk(c):
        return k_ref[pl.ds(pl.multiple_of(c * tk, tk), tk), :]

    def write_out():
        for g in heads:
            acc = acc_ref[g]
            out_t = acc[:HEAD_DIM] / acc[HEAD_DIM:HEAD_DIM + 1]
            o_ref[:, g * HEAD_DIM:(g + 1) * HEAD_DIM] = out_t.T.astype(BF16)

    row0 = lax.broadcasted_iota(jnp.int32, (HEAD_DIM, tq), 0) == 0
    for g in heads:
        qt = q_ref[:, g * HEAD_DIM:(g + 1) * HEAD_DIM].astype(F32).T
        bound = jnp.sqrt(jnp.sum(qt * qt, axis=0, keepdims=True)) * kb_ref[...]
        qa_ref[g, :HEAD_DIM, :] = qt.astype(BF16)
        qa_ref[g, HEAD_DIM:, :] = jnp.where(row0, -bound, 0.0).astype(BF16)
    kf = ATTN_FAST_KCHUNKS
    tkf = kf * tk
    nfast = nchunk // kf
    ones_col = (lax.broadcasted_iota(jnp.int32, (tkf, HEAD_DIM), 1) == 0).astype(BF16)

    def probs_into(c, slot):
        kc = k_ref[pl.ds(pl.multiple_of(c * tkf, tkf), tkf), :]
        ka = jnp.concatenate([kc, ones_col], axis=1)
        for g in heads:
            p_ref[slot][g] = jnp.exp2(jnp.dot(ka, qa_ref[g], preferred_element_type=F32)).astype(BF16)

    def pv_add(c, slot):
        vc = jnp.concatenate([vt_ref[kf * c + j] for j in range(kf)], axis=1)
        for g in heads:
            acc_ref[g] += jnp.dot(vc, p_ref[slot][g], preferred_element_type=F32)

    acc_ref[...] = jnp.zeros(acc_ref.shape, F32)
    probs_into(0, 0)

    def fast_pair(i, carry):
        probs_into(2 * i + 1, 1)
        pv_add(2 * i, 0)
        probs_into(2 * i + 2, 0)
        pv_add(2 * i + 1, 1)
        return carry

    lax.fori_loop(0, (nfast - 2) // 2, fast_pair, 0)
    probs_into(nfast - 1, 1)
    pv_add(nfast - 2, 0)
    pv_add(nfast - 1, 1)
    write_out()
    den_min = acc_ref[0][HEAD_DIM:HEAD_DIM + 1]
    for g in heads[1:]:
        den_min = jnp.minimum(den_min, acc_ref[g][HEAD_DIM:HEAD_DIM + 1])
    safe = jnp.min(den_min) > DENOMINATOR_GUARD

    def scores_into(c, slot):
        kc = k_chunk(c)
        for g in heads:
            sc = jnp.dot(kc, qt_ref[g], preferred_element_type=F32)
            s_ref[slot][g] = sc
            parts = [jnp.max(sc[r * (tk // 4):(r + 1) * (tk // 4)], axis=0, keepdims=True) for r in range(4)]
            mx_ref[slot][g] = jnp.maximum(jnp.maximum(parts[0], parts[1]), jnp.maximum(parts[2], parts[3]))

    def softmax(slot):
        for g in heads:
            m_old = m_ref[g]
            m_new = jnp.maximum(m_old, mx_ref[slot][g])
            al_ref[slot][g] = jnp.exp2(m_old - m_new)
            m_ref[g] = m_new
            p_ref[slot][g, :tk, :] = jnp.exp2(s_ref[slot][g] - m_new).astype(BF16)

    def pv_update(c, slot):
        vc = vt_ref[c]
        for g in heads:
            acc_ref[g] = al_ref[slot][g] * acc_ref[g] + jnp.dot(vc, p_ref[slot][g, :tk, :],
                                                                preferred_element_type=F32)

    def step(c, slot):
        scores_into(c + 1, 1 - slot)
        pv_update(c - 1, 1 - slot)
        softmax(slot)

    @pl.when(jnp.logical_not(safe))
    def _():
        for g in heads:
            qt_ref[g] = qa_ref[g, :HEAD_DIM, :]
        scores_into(0, 0)
        m_ref[...] = jnp.full(m_ref.shape, -jnp.inf, F32)
        acc_ref[...] = jnp.zeros(acc_ref.shape, F32)
        softmax(0)
        scores_into(1, 1)

        def pair(i, carry):
            step(2 * i + 1, 1)
            step(2 * i + 2, 0)
            return carry

        lax.fori_loop(0, (nchunk - 2) // 2, pair, 0)
        pv_update(nchunk - 2, 0)
        softmax(1)
        pv_update(nchunk - 1, 1)
        write_out()


def _attention(q, k, vt, k_norm_bound, tq):
    bsz, s, _ = q.shape
    rows, tk = vt.shape[3], vt.shape[4]
    nchunk = s // tk
    assert nchunk % (2 * ATTN_FAST_KCHUNKS) == 0 and nchunk >= 4 * ATTN_FAST_KCHUNKS
    gw = Q_GROUP * HEAD_DIM
    qspec = pl.BlockSpec((None, tq, gw), lambda b, h, i: (b, i, h))
    kspec = pl.BlockSpec((None, s, HEAD_DIM), lambda b, h, i: (b, 0, h))
    vspec = pl.BlockSpec((None, None, nchunk, rows, tk), lambda b, h, i: (b, h, 0, 0, 0))
    head_vec = pltpu.VMEM((Q_GROUP, 1, tq), F32)
    return pl.pallas_call(
        _attn_kernel,
        grid=(bsz, N_KV_HEADS, s // tq),
        in_specs=[pl.BlockSpec((1, 1), lambda b, h, i: (0, 0)), qspec, kspec, vspec],
        out_specs=qspec,
        out_shape=jax.ShapeDtypeStruct((bsz, s, Q_DIM), BF16),
        scratch_shapes=[head_vec, pltpu.VMEM((Q_GROUP, rows, tq), F32), head_vec, head_vec]
        + [pltpu.VMEM((Q_GROUP, tk, tq), F32)] * 2
        + [pltpu.VMEM((Q_GROUP, ATTN_FAST_KCHUNKS * tk, tq), BF16)] * 2
        + [pltpu.VMEM((Q_GROUP, HEAD_DIM, tq), BF16), head_vec, head_vec,
           pltpu.VMEM((Q_GROUP, 2 * HEAD_DIM, tq), BF16)],
        compiler_params=_cparams(("parallel", "parallel", "parallel")),
        name="attn",
    )(k_norm_bound, q, k, vt)


def _mix_kernel(yf_ref, at_ref, gf_ref, ga_ref, x_ref, g1_ref, sc2_ref, sh2_ref,
                wf_ref, wo_ref, wout_ref, l1g_ref, l1b_ref, wr_ref, br_ref, tri_ref,
                x1_ref, u2_ref, idx_ref, wgt_ref, rank_ref, cnt_ref, cnt_acc):
    first = jnp.logical_and(pl.program_id(0) == 0, pl.program_id(1) == 0)

    @pl.when(first)
    def _():
        cnt_acc[...] = jnp.zeros_like(cnt_acc)

    tm_all = x_ref.shape[0]
    rg = tm_all // MIX_ROW_GROUPS
    u_his, u_los = [], []
    groups = [slice(i * rg, (i + 1) * rg) for i in range(MIX_ROW_GROUPS)]
    ms = []
    for rows in groups:
        yf = jnp.dot(yf_ref[rows, :], wf_ref[...], preferred_element_type=F32)
        ya = jnp.dot(at_ref[rows, :], wo_ref[...], preferred_element_type=F32)
        ms.append((gf_ref[rows, :].astype(F32) * yf + ga_ref[rows, :].astype(F32) * ya).astype(BF16))
    hs = [jnp.dot(m, wout_ref[...], preferred_element_type=F32) for m in ms]
    for rows, h in zip(groups, hs):
        r = DEEPNORM_ALPHA * x_ref[rows, :] + g1_ref[...] * h
        x1 = _layer_norm(r, LN_EPS) * l1g_ref[...] + l1b_ref[...]
        x1_ref[rows, :] = x1
        u2 = _layer_norm(x1, ADA_EPS) * (1.0 + sc2_ref[...]) + sh2_ref[...]
        u2_ref[rows, :] = _pack_bf16_pairs(u2)
        u_hi = u2.astype(BF16)
        u_his.append(u_hi)
        u_los.append((u2 - u_hi.astype(F32)).astype(BF16))

    nt = (((1,), (1,)), ((), ()))
    u_hi = jnp.concatenate(u_his, axis=0)
    u_lo = jnp.concatenate(u_los, axis=0)
    w = wr_ref[...]
    w_hi = w.astype(BF16)
    w_lo = (w - w_hi.astype(F32)).astype(BF16)
    logits = (lax.dot_general(w_hi, u_hi, nt, preferred_element_type=F32)
              + lax.dot_general(w_lo, u_hi, nt, preferred_element_type=F32)
              + lax.dot_general(w_hi, u_lo, nt, preferred_element_type=F32)) + br_ref[...]

    ne, tm = logits.shape
    eid = lax.broadcasted_iota(jnp.int32, (ne, tm), 0)
    work = logits
    vals, idxs, hots = [], [], []
    for _ in range(TOP_K):
        mx = jnp.max(work, axis=0, keepdims=True)
        ix = jnp.min(jnp.where(work == mx, eid, ne), axis=0, keepdims=True)
        hot = eid == ix
        work = jnp.where(hot, -jnp.inf, work)
        vals.append(mx)
        idxs.append(ix)
        hots.append(hot)
    ex = [jnp.exp(v - vals[0]) for v in vals]
    den = ex[0] + ex[1] + ex[2] + ex[3]
    wgt_ref[...] = jnp.concatenate([e / den for e in ex], axis=0)
    idx_ref[...] = jnp.concatenate(idxs, axis=0)

    sel = hots[0] | hots[1] | hots[2] | hots[3]
    mask = jnp.where(sel, 1.0, 0.0)
    prefix = jnp.dot(mask.astype(BF16), tri_ref[...], preferred_element_type=F32)
    pos = prefix + cnt_acc[...]
    ranks = [jnp.sum(jnp.where(hot, pos, 0.0), axis=0, keepdims=True) for hot in hots]
    rank_ref[...] = jnp.concatenate(ranks, axis=0).astype(jnp.int32)
    cnt_acc[...] += jnp.sum(mask, axis=1, keepdims=True)
    cnt_ref[...] = cnt_acc[...].astype(jnp.int32)


def _mix(yfm, attn, gf, ga, x, g1, sc2, sh2, wf, wo, wout, l1g, l1b, wr_t, br, tm, b0, bsz):
    _, s, d = x.shape
    n = bsz * s
    nt = s // tm
    tri = (jnp.arange(tm)[:, None] < jnp.arange(tm)[None, :]).astype(BF16)
    row_in = lambda w: pl.BlockSpec((None, tm, w), lambda b, i: (b + b0, i, 0))
    row = lambda w: pl.BlockSpec((None, tm, w), lambda b, i: (b, i, 0))
    mod = pl.BlockSpec((None, 1, d), lambda b, i: (b + b0, 0, 0))
    full = lambda shp: pl.BlockSpec(shp, lambda b, i: (0,) * len(shp))
    tok = pl.BlockSpec((TOP_K, tm), lambda b, i: (0, b * nt + i))
    return pl.pallas_call(
        _mix_kernel,
        grid=(bsz, nt),
        in_specs=[row_in(FOURIER_DIM), row_in(Q_DIM), row_in(d), row_in(d), row_in(d), mod, mod, mod,
                  full(wf.shape), full(wo.shape), full(wout.shape), full((1, d)), full((1, d)),
                  full(wr_t.shape), full((N_EXPERTS, 1)), full((tm, tm))],
        out_specs=[row(d), row(d // 2), tok, tok, tok, full((N_EXPERTS, 1))],
        out_shape=[jax.ShapeDtypeStruct((bsz, s, d), F32),
                   jax.ShapeDtypeStruct((bsz, s, d // 2), jnp.int32),
                   jax.ShapeDtypeStruct((TOP_K, n), jnp.int32),
                   jax.ShapeDtypeStruct((TOP_K, n), F32),
                   jax.ShapeDtypeStruct((TOP_K, n), jnp.int32),
                   jax.ShapeDtypeStruct((N_EXPERTS, 1), jnp.int32)],
        scratch_shapes=[pltpu.VMEM((N_EXPERTS, 1), F32)],
        compiler_params=_cparams(("arbitrary", "arbitrary")),
        name="mix",
    )(yfm, attn, gf, ga, x, g1, sc2, sh2, wf, wo, wout, l1g, l1b, wr_t, br, tri)


def _expert_kernel(be_ref, nv_ref, new_ref, xs_ref, wgu_ref, bgu_ref, wd_ref, bd_ref, ys_ref,
                   wgu_bf, wd_bf):
    del be_ref
    i = pl.program_id(0)
    nvalid = nv_ref[i]

    @pl.when(new_ref[i] == 1)
    def _():
        wgu_bf[...] = wgu_ref[...].astype(BF16)
        wd_bf[...] = wd_ref[...].astype(BF16)

    @pl.when(nvalid > 0)
    def _():
        xw = xs_ref[...]
        row = lax.broadcasted_iota(jnp.int32, xw.shape, 0)
        lo, hi = _unpack_bf16_pairs(jnp.where(row < nvalid, xw, 0))
        xb = jnp.concatenate([lo.astype(BF16), hi.astype(BF16)], axis=1)
        h = jnp.dot(xb, wgu_bf[...], preferred_element_type=F32) + bgu_ref[...]
        gate = jnp.minimum(h[:, :D_EXPERT], SWIGLU_LIMIT)
        up = jnp.clip(h[:, D_EXPERT:], -SWIGLU_LIMIT, SWIGLU_LIMIT)
        act = (up + 1.0) * gate * jax.nn.sigmoid(SWIGLU_ALPHA * gate)
        y = jnp.dot(act.astype(BF16), wd_bf[...], preferred_element_type=F32) + bd_ref[...]
        ys_ref[...] = _pack_bf16_pairs(y)

    @pl.when(nvalid == 0)
    def _():
        ys_ref[...] = jnp.zeros_like(ys_ref)


def _experts(block_e, block_nvalid, block_new, xs, wgu, bgu, wd, bd, tmb):
    n_pad, dh = xs.shape
    d = 2 * dh
    grid_spec = pltpu.PrefetchScalarGridSpec(
        num_scalar_prefetch=3,
        grid=(n_pad // tmb,),
        in_specs=[pl.BlockSpec((tmb, dh), lambda i, be, nv, nw: (i, 0)),
                  pl.BlockSpec((None, d, 2 * D_EXPERT), lambda i, be, nv, nw: (be[i], 0, 0)),
                  pl.BlockSpec((None, 1, 2 * D_EXPERT), lambda i, be, nv, nw: (be[i], 0, 0)),
                  pl.BlockSpec((None, D_EXPERT, d), lambda i, be, nv, nw: (be[i], 0, 0)),
                  pl.BlockSpec((None, 1, d), lambda i, be, nv, nw: (be[i], 0, 0))],
        out_specs=pl.BlockSpec((tmb, dh), lambda i, be, nv, nw: (i, 0)),
        scratch_shapes=[pltpu.VMEM((d, 2 * D_EXPERT), BF16), pltpu.VMEM((D_EXPERT, d), BF16)],
    )
    return pl.pallas_call(
        _expert_kernel,
        grid_spec=grid_spec,
        out_shape=jax.ShapeDtypeStruct((n_pad, dh), jnp.int32),
        compiler_params=_cparams(("arbitrary",)),
        name="experts",
    )(block_e, block_nvalid, block_new, xs, wgu, bgu, wd, bd)


def _final_kernel(yg_ref, w_ref, x1_ref, g2_ref, lg_ref, lb_ref, o_ref):
    w = w_ref[...]
    acc_lo = None
    for j in range(TOP_K):
        lo, hi = _unpack_bf16_pairs(yg_ref[j])
        wj = w[:, j:j + 1]
        acc_lo = lo * wj if acc_lo is None else acc_lo + lo * wj
        acc_hi = hi * wj if j == 0 else acc_hi + hi * wj
    h = jnp.concatenate([acc_lo, acc_hi], axis=1)
    r = DEEPNORM_ALPHA * x1_ref[...] + g2_ref[...] * h
    o_ref[...] = _layer_norm(r, LN_EPS) * lg_ref[...] + lb_ref[...]


def _final(yg, w_tok, x1, g2, lg, lb, tm, b0, total_bsz, out_so_far):
    bsz, s, d = x1.shape
    nt = s // tm
    row = pl.BlockSpec((None, tm, d), lambda b, i: (b, i, 0))
    full = pl.BlockSpec((1, d), lambda b, i: (0, 0))
    in_specs = [pl.BlockSpec((TOP_K, tm, d // 2), lambda b, i: (0, b * nt + i, 0)),
                pl.BlockSpec((tm, TOP_K), lambda b, i: (b * nt + i, 0)),
                row, pl.BlockSpec((None, 1, d), lambda b, i: (b + b0, 0, 0)), full, full]
    args = [yg, w_tok, x1, g2, lg, lb]
    aliases = {}
    if out_so_far is not None:
        in_specs.append(pl.BlockSpec(memory_space=pl.ANY))
        args.append(out_so_far)
        aliases = {len(args) - 1: 0}
    body = _final_kernel if out_so_far is None else (lambda *refs: _final_kernel(*refs[:6], refs[7]))
    return pl.pallas_call(
        body,
        grid=(bsz, nt),
        in_specs=in_specs,
        out_specs=pl.BlockSpec((None, tm, d), lambda b, i: (b + b0, i, 0)),
        out_shape=jax.ShapeDtypeStruct((total_bsz, s, d), F32),
        input_output_aliases=aliases,
        compiler_params=_cparams(("parallel", "parallel")),
        name="final",
    )(*args)


def _scatter_rows(rows, dest, m_out):
    n, w = rows.shape
    nj = dest.shape[0]
    ch = GATHER_CHUNK
    nw = SC_CORES * SC_SUBCORES
    nch = n // (nw * ch)
    assert n == nw * nch * ch and nch % 2 == 0
    per_w = nch * ch
    mesh = plsc.VectorSubcoreMesh(core_axis_name="c", subcore_axis_name="s",
                                  num_cores=SC_CORES, num_subcores=SC_SUBCORES)

    @functools.partial(
        pl.kernel, mesh=mesh,
        out_type=jax.ShapeDtypeStruct((m_out, w), rows.dtype),
        scratch_types=[pltpu.VMEM((nj, nch, ch), jnp.int32), pltpu.VMEM((2, ch, w), rows.dtype),
                       pltpu.SemaphoreType.DMA((2,)), pltpu.SemaphoreType.DMA((2,))],
    )
    def scatter_kernel(rows_hbm, dest_hbm, out_hbm, idx_v, rows_v, rsem, ssem):
        wid = lax.axis_index("s") * SC_CORES + lax.axis_index("c")
        base = wid * per_w
        for j in range(nj):
            pltpu.sync_copy(dest_hbm.at[j, wid], idx_v.at[j])

        def read(i, b):
            off = pl.multiple_of(base + i * ch, ch)
            return pltpu.make_async_copy(rows_hbm.at[pl.ds(off, ch)], rows_v.at[b], rsem.at[b])

        def scatter(i, b, j):
            return pltpu.make_async_copy(rows_v.at[b], out_hbm.at[idx_v.at[j, i]], ssem.at[b])

        read(0, 0).start()

        @pl.loop(0, nch, step=2)
        def _(i0):
            for b in range(2):
                i = i0 + b
                read(i, b).wait()

                @pl.when(i + 1 < nch)
                def _():
                    @pl.when(i >= 1)
                    def _():
                        for j in range(nj):
                            scatter(i - 1, 1 - b, j).wait()
                    read(i + 1, 1 - b).start()

                for j in range(nj):
                    scatter(i, b, j).start()

        for b in range(2):
            for j in range(nj):
                scatter(nch - 2 + b, b, j).wait()

    return scatter_kernel(rows, dest.reshape(nj, nw, nch, ch))


def _gather_rows(table, idx):
    m = idx.shape[0]
    w = table.shape[1]
    ch = GATHER_CHUNK
    nw = SC_CORES * SC_SUBCORES
    nch = m // (nw * ch)
    assert m == nw * nch * ch and nch % 2 == 0
    per_w = nch * ch
    mesh = plsc.VectorSubcoreMesh(core_axis_name="c", subcore_axis_name="s",
                                  num_cores=SC_CORES, num_subcores=SC_SUBCORES)

    @functools.partial(
        pl.kernel, mesh=mesh,
        out_type=jax.ShapeDtypeStruct((m, w), table.dtype),
        scratch_types=[pltpu.VMEM((nch, ch), jnp.int32), pltpu.VMEM((2, ch, w), table.dtype),
                       pltpu.SemaphoreType.DMA((2,)), pltpu.SemaphoreType.DMA((2,))],
    )
    def gather_kernel(table_hbm, idx_hbm, out_hbm, idx_v, rows_v, gsem, wsem):
        wid = lax.axis_index("s") * SC_CORES + lax.axis_index("c")
        base = wid * per_w
        pltpu.sync_copy(idx_hbm.at[wid], idx_v)

        def gather(j, b):
            return pltpu.make_async_copy(table_hbm.at[idx_v.at[j]], rows_v.at[b], gsem.at[b])

        def write(j, b):
            off = pl.multiple_of(base + j * ch, ch)
            return pltpu.make_async_copy(rows_v.at[b], out_hbm.at[pl.ds(off, ch)], wsem.at[b])

        gather(0, 0).start()

        @pl.loop(0, nch, step=2)
        def _(i):
            for b in range(2):
                j = i + b
                gather(j, b).wait()

                @pl.when(j + 1 < nch)
                def _():
                    @pl.when(j >= 1)
                    def _():
                        write(j - 1, 1 - b).wait()
                    gather(j + 1, 1 - b).start()

                write(j, b).start()

        write(nch - 2, 0).wait()
        write(nch - 1, 1).wait()

    return gather_kernel(table, idx.reshape(nw, nch, ch))


def kernel(x, c, w_ada, b_ada, w_in, q_norm_g, k_norm_g, w_fourier, w_attn_o, w_out, ln1_g, ln1_b,
           w_router, b_router, w_gate_up, b_gate_up, w_down, b_down, ln2_g, ln2_b):
    bsz, s, d = x.shape
    n = bsz * s
    tm = min(ROW_TILE, s)
    l = 0

    mod = _ada(c, w_ada[l], b_ada[l])
    sh1, sc1, g1, sh2, sc2, g2 = [m[:, None, :] for m in jnp.split(mod, N_MOD, axis=-1)]

    dft_c, m1, m2, cw, sw, p, q = _dft_tables(s)
    cos, sin_a, sin_b = _rope_tables(s)

    fa, fb, qh, kh, vt, gf, ga = _inproj(
        x, sc1, sh1, w_in[l].astype(BF16), dft_c,
        q_norm_g[l].reshape(1, HEAD_DIM), k_norm_g[l].reshape(1, HEAD_DIM), cos, sin_a, sin_b, tm)

    zr, zi = _fft1(fa, fb, m1, cw, sw, p, q)
    yfm = _fft2(zr, zi, m2)

    k_bound = (math.sqrt(HEAD_DIM) * (1.0 + 2.0 ** -7)) * jnp.max(jnp.abs(k_norm_g[l])).reshape(1, 1)
    attn = _attention(qh, kh, vt, k_bound, min(ATTN_TQ, s))

    wf_b, wo_b, wout_b = w_fourier[l].astype(BF16), w_attn_o[l].astype(BF16), w_out[l].astype(BF16)
    bgu, bd = b_gate_up[l][:, None, :], b_down[l][:, None, :]
    tmb = MOE_BLOCK
    eids = jnp.arange(N_EXPERTS, dtype=jnp.int32)

    ngroups = MOE_BATCH_GROUPS if bsz % MOE_BATCH_GROUPS == 0 else 1
    gb = bsz // ngroups
    ng = gb * s
    out = None
    for grp in range(ngroups):
        b0 = grp * gb
        x1, u2p, idx_t, wgt_t, rank_t, counts = _mix(
            yfm, attn, gf, ga, x, g1, sc2, sh2, wf_b, wo_b, wout_b,
            ln1_g[l].reshape(1, d), ln1_b[l].reshape(1, d),
            w_router[l].T, b_router[l].reshape(N_EXPERTS, 1), tm, b0, gb)

        counts = counts.reshape(N_EXPERTS)
        pcounts = ((counts + tmb - 1) // tmb) * tmb
        pends = jnp.cumsum(pcounts)
        pstarts = pends - pcounts
        n_pad = ng * TOP_K + N_EXPERTS * tmb
        nblk = n_pad // tmb
        start_of = jnp.sum(jnp.where(idx_t[None] == eids[:, None, None], pstarts[:, None, None], 0), axis=0)
        dest_t = start_of + rank_t
        block_starts = jnp.arange(nblk, dtype=jnp.int32) * tmb
        block_e = jnp.sum((block_starts[:, None] >= pends[None, :]).astype(jnp.int32), axis=1)
        block_e = jnp.minimum(block_e, N_EXPERTS - 1)
        block_nvalid = jnp.clip(pstarts[block_e] + counts[block_e] - block_starts, 0, tmb)
        block_new = jnp.concatenate([jnp.ones((1,), jnp.int32),
                                     (block_e[1:] != block_e[:-1]).astype(jnp.int32)])

        xs = _scatter_rows(u2p.reshape(ng, d // 2), dest_t, n_pad)
        ys = _experts(block_e, block_nvalid, block_new, xs, w_gate_up[l], bgu, w_down[l], bd, tmb)
        yg = _gather_rows(ys, dest_t.reshape(-1)).reshape(TOP_K, ng, d // 2)
        out = _final(yg, wgt_t.T, x1, g2, ln2_g[l].reshape(1, d), ln2_b[l].reshape(1, d), tm,
                     b0, bsz, out)
    return out
```

```python
import functools
import math

import jax
import jax.numpy as jnp
import numpy as np
from jax import lax
from jax.experimental import pallas as pl
from jax.experimental.pallas import tpu as pltpu
from jax.experimental.pallas import tpu_sc as plsc

F32 = jnp.float32
BF16 = jnp.bfloat16

D_MODEL = 1024
GRID_W = 64
N_Q_HEADS = 8
N_KV_HEADS = 2
HEAD_DIM = 128
Q_GROUP = N_Q_HEADS // N_KV_HEADS
ROPE_THETA = 10000.0
ROPE_AXIS_DIM = HEAD_DIM // 2
N_FOURIER_GROUPS = 4
FOURIER_GROUP_DIM = 128
FOURIER_DIM = N_FOURIER_GROUPS * FOURIER_GROUP_DIM
Q_DIM = N_Q_HEADS * HEAD_DIM
KV_DIM = N_KV_HEADS * HEAD_DIM
N_EXPERTS = 32
TOP_K = 4
D_EXPERT = 1024
SWIGLU_LIMIT = 7.0
SWIGLU_ALPHA = 1.702
N_MOD = 6
DEPTH = 1
DEEPNORM_ALPHA = (2 * DEPTH) ** 0.25
LN_EPS = 1e-5
ADA_EPS = 1e-6
QK_EPS = 1e-6
LOG2E = 1.4426950408889634

V7X_VMEM_BYTES = 64 * 1024 * 1024
VMEM_LIMIT = V7X_VMEM_BYTES - 8 * 1024 * 1024
LANES = 128
BF16_SUBLANES = 16

FFT_Q = 128
ROW_TILE = 512
ATTN_TQ = 512
ONES_ROWS = 16
ATTN_FAST_KCHUNKS = 2
DENOMINATOR_GUARD = 2.0 ** -80
INPROJ_ROW_GROUPS = 2
MIX_ROW_GROUPS = 2
MOE_BLOCK = 512
MOE_BATCH_GROUPS = 2

SC_CORES = 2
SC_SUBCORES = 16
GATHER_CHUNK = 64


def _cparams(sem):
    return pltpu.CompilerParams(dimension_semantics=sem, vmem_limit_bytes=VMEM_LIMIT)


def _layer_norm(x, eps):
    mu = jnp.mean(x, axis=-1, keepdims=True)
    xc = x - mu
    var = jnp.mean(xc * xc, axis=-1, keepdims=True)
    return xc * lax.rsqrt(var + eps)


def _pack_bf16_pairs(x):
    n = x.shape[1] // 2
    bits = pltpu.bitcast(x.astype(BF16).astype(F32), jnp.uint32)
    word = (bits[:, :n] >> 16) | (bits[:, n:] & jnp.uint32(0xFFFF0000))
    return pltpu.bitcast(word, jnp.int32)


def _unpack_bf16_pairs(w):
    bits = pltpu.bitcast(w, jnp.uint32)
    lo = pltpu.bitcast(bits << 16, F32)
    hi = pltpu.bitcast(bits & jnp.uint32(0xFFFF0000), F32)
    return lo, hi


def _ada_kernel(c_ref, w_ref, b_ref, o_ref):
    c = c_ref[...]
    cond = c * jax.nn.sigmoid(c)
    o_ref[...] = jnp.dot(cond, w_ref[...], preferred_element_type=F32,
                         precision=lax.Precision.HIGHEST) + b_ref[...]


def _ada(c, w_ada, b_ada):
    bsz, d = c.shape
    n = w_ada.shape[1]
    tn = 1536
    return pl.pallas_call(
        _ada_kernel,
        grid=(n // tn,),
        in_specs=[pl.BlockSpec((bsz, d), lambda j: (0, 0)),
                  pl.BlockSpec((d, tn), lambda j: (0, j)),
                  pl.BlockSpec((1, tn), lambda j: (0, j))],
        out_specs=pl.BlockSpec((bsz, tn), lambda j: (0, j)),
        out_shape=jax.ShapeDtypeStruct((bsz, n), F32),
        compiler_params=_cparams(("arbitrary",)),
        name="ada",
    )(c, w_ada, b_ada.reshape(1, n))


def _rope(xh, cos, sin_a, sin_b):
    a = pltpu.roll(xh, 32, 1)
    b = pltpu.roll(xh, 96, 1)
    return xh * cos + b * sin_b + a * sin_a


def _inproj_kernel(x_ref, sc_ref, sh_ref, w_ref, dft_ref, qg_ref, kg_ref,
                   cos_ref, sa_ref, sb_ref,
                   fa_ref, fb_ref, q_ref, k_ref, vt_ref, gf_ref, ga_ref):
    tm = x_ref.shape[0]
    rg = tm // INPROJ_ROW_GROUPS
    q_scale = LOG2E / math.sqrt(HEAD_DIM)
    for i in range(INPROJ_ROW_GROUPS):
        rows = slice(i * rg, (i + 1) * rg)
        u = _layer_norm(x_ref[rows, :], ADA_EPS) * (1.0 + sc_ref[...]) + sh_ref[...]
        ub = u.astype(BF16)
        cos, sa, sb = cos_ref[rows, :], sa_ref[rows, :], sb_ref[rows, :]

        def norm_rope(z, gain, scale):
            ms = jnp.mean(z * z, axis=-1, keepdims=True)
            zn = z * lax.rsqrt(ms + QK_EPS) * gain
            return (_rope(zn, cos, sa, sb) * scale).astype(BF16)

        c0 = 0
        zf = jnp.dot(ub, w_ref[:, c0:c0 + FOURIER_DIM], preferred_element_type=F32).astype(BF16)
        for g in range(N_FOURIER_GROUPS):
            sl = slice(g * FOURIER_GROUP_DIM, (g + 1) * FOURIER_GROUP_DIM)
            ab = jnp.dot(zf[:, sl], dft_ref[...], preferred_element_type=F32)
            fa_ref[rows, sl] = ab[:, :FOURIER_GROUP_DIM].astype(BF16)
            fb_ref[rows, sl] = ab[:, FOURIER_GROUP_DIM:].astype(BF16)
        c0 += FOURIER_DIM

        zq = jnp.dot(ub, w_ref[:, c0:c0 + Q_DIM], preferred_element_type=F32)
        for h in range(N_Q_HEADS):
            sl = slice(h * HEAD_DIM, (h + 1) * HEAD_DIM)
            q_ref[rows, sl] = norm_rope(zq[:, sl], qg_ref[...], q_scale)
        c0 += Q_DIM

        zk = jnp.dot(ub, w_ref[:, c0:c0 + KV_DIM], preferred_element_type=F32)
        for h in range(N_KV_HEADS):
            sl = slice(h * HEAD_DIM, (h + 1) * HEAD_DIM)
            k_ref[rows, sl] = norm_rope(zk[:, sl], kg_ref[...], 1.0)
        c0 += KV_DIM

        zv = jnp.dot(ub, w_ref[:, c0:c0 + KV_DIM], preferred_element_type=F32)
        for h in range(N_KV_HEADS):
            vt_ref[h, :HEAD_DIM, rows] = zv[:, h * HEAD_DIM:(h + 1) * HEAD_DIM].T.astype(BF16)
            vt_ref[h, HEAD_DIM:, rows] = jnp.ones((ONES_ROWS, rg), BF16)
        c0 += KV_DIM

        zg = jnp.dot(ub, w_ref[:, c0:c0 + D_MODEL], preferred_element_type=F32)
        gf_ref[rows, :] = jax.nn.sigmoid(zg).astype(BF16)
        c0 += D_MODEL
        zg = jnp.dot(ub, w_ref[:, c0:c0 + D_MODEL], preferred_element_type=F32)
        ga_ref[rows, :] = jax.nn.sigmoid(zg).astype(BF16)


def _inproj(x, sc1, sh1, w_in, dft_c, qg, kg, cos, sin_a, sin_b, tm):
    bsz, s, d = x.shape
    ncol = w_in.shape[1]
    row = lambda w: pl.BlockSpec((None, tm, w), lambda b, i: (b, i, 0))
    mod = pl.BlockSpec((None, 1, d), lambda b, i: (b, 0, 0))
    full = lambda shp: pl.BlockSpec(shp, lambda b, i: (0,) * len(shp))
    tab = pl.BlockSpec((tm, HEAD_DIM), lambda b, i: (i, 0))
    vt_rows = HEAD_DIM + ONES_ROWS
    vt_spec = pl.BlockSpec((None, N_KV_HEADS, None, vt_rows, tm), lambda b, i: (b, 0, i, 0, 0))
    vt_shape = jax.ShapeDtypeStruct((bsz, N_KV_HEADS, s // tm, vt_rows, tm), BF16)
    rows_out = lambda w: (row(w), jax.ShapeDtypeStruct((bsz, s, w), BF16))
    outs = [rows_out(FOURIER_DIM), rows_out(FOURIER_DIM), rows_out(Q_DIM), rows_out(KV_DIM),
            (vt_spec, vt_shape), rows_out(D_MODEL), rows_out(D_MODEL)]
    return pl.pallas_call(
        _inproj_kernel,
        grid=(bsz, s // tm),
        in_specs=[row(d), mod, mod, full((d, ncol)), full(dft_c.shape),
                  full((1, HEAD_DIM)), full((1, HEAD_DIM)), tab, tab, tab],
        out_specs=[o[0] for o in outs],
        out_shape=[o[1] for o in outs],
        compiler_params=_cparams(("parallel", "parallel")),
        name="inproj",
    )(x, sc1, sh1, w_in, dft_c, qg, kg, cos, sin_a, sin_b)


def _fft1_kernel(a_ref, b_ref, m_ref, cw_ref, sw_ref, zr_ref, zi_ref, a32, b32):
    p, nb, _ = a_ref.shape
    a32[...] = a_ref[...].astype(F32)
    b32[...] = b_ref[...].astype(F32)
    for j in range(nb):
        ab = jnp.concatenate([a32[:, j, :], b32[:, j, :]], axis=0).astype(BF16)
        t = jnp.dot(m_ref[...], ab, preferred_element_type=F32)
        tr, ti = t[:p], t[p:]
        cw, sw = cw_ref[j], sw_ref[j]
        zr_ref[j] = (tr * cw - ti * sw).astype(BF16)
        zi_ref[j] = (ti * cw + tr * sw).astype(BF16)


def _fft1(fa, fb, m1, cw, sw, p, q):
    bsz, s, c = fa.shape
    nb = BF16_SUBLANES
    inp = pl.BlockSpec((None, p, nb, c), lambda b, j: (b, 0, j, 0))
    tw = pl.BlockSpec((nb, p, 1), lambda b, j: (j, 0, 0))
    out = pl.BlockSpec((None, nb, p, c), lambda b, j: (b, j, 0, 0))
    return pl.pallas_call(
        _fft1_kernel,
        grid=(bsz, q // nb),
        in_specs=[inp, inp, pl.BlockSpec(m1.shape, lambda b, j: (0, 0)), tw, tw],
        out_specs=[out, out],
        out_shape=[jax.ShapeDtypeStruct((bsz, q, p, c), BF16)] * 2,
        scratch_shapes=[pltpu.VMEM((p, nb, c), F32)] * 2,
        compiler_params=_cparams(("parallel", "parallel")),
        name="fft1",
    )(fa.reshape(bsz, p, q, c), fb.reshape(bsz, p, q, c), m1, cw, sw)


def _fft2_kernel(zr_ref, zi_ref, m_ref, y_ref, zr32, zi32, y32):
    zr32[...] = zr_ref[...].astype(F32)
    zi32[...] = zi_ref[...].astype(F32)
    for j in range(zr_ref.shape[1]):
        z = jnp.concatenate([zr32[:, j, :], zi32[:, j, :]], axis=0).astype(BF16)
        y32[:, j, :] = jnp.dot(m_ref[...], z, preferred_element_type=F32)
    y_ref[...] = y32[...].astype(BF16)


def _fft2(zr, zi, m2):
    bsz, q, p, c = zr.shape
    pc = BF16_SUBLANES
    blk = pl.BlockSpec((None, q, pc, c), lambda b, j: (b, 0, j, 0))
    y = pl.pallas_call(
        _fft2_kernel,
        grid=(bsz, p // pc),
        in_specs=[blk, blk, pl.BlockSpec(m2.shape, lambda b, j: (0, 0))],
        out_specs=blk,
        out_shape=jax.ShapeDtypeStruct((bsz, q, p, c), BF16),
        scratch_shapes=[pltpu.VMEM((q, pc, c), F32)] * 3,
        compiler_params=_cparams(("parallel", "parallel")),
        name="fft2",
    )(zr, zi, m2)
    return y.reshape(bsz, q * p, c)


def _dft_tables(s):
    q = FFT_Q
    p = s // q

    def cs(n_rows, n_cols, n):
        r = np.arange(n_rows, dtype=np.int64)[:, None]
        c = np.arange(n_cols, dtype=np.int64)[None, :]
        ang = ((r * c) % n).astype(np.float64) * (2.0 * math.pi / n)
        return np.cos(ang).astype(np.float32), np.sin(ang).astype(np.float32)

    cc, sc = cs(FOURIER_GROUP_DIM, FOURIER_GROUP_DIM, FOURIER_GROUP_DIM)
    dft_c = jnp.asarray(np.concatenate([cc, sc], axis=1), dtype=BF16)
    cp, sp = cs(p, p, p)
    m1 = jnp.asarray(np.block([[cp, -sp], [sp, cp]]), dtype=BF16)
    cq, sq = cs(q, q, q)
    scale = np.float32(1.0 / math.sqrt(s * FOURIER_GROUP_DIM))
    m2 = jnp.asarray(np.concatenate([cq, -sq], axis=1) * scale, dtype=BF16)
    cw, sw = cs(q, p, s)
    return dft_c, m1, m2, jnp.asarray(cw[:, :, None]), jnp.asarray(sw[:, :, None]), p, q


def _rope_tables(s):
    f32 = np.float32
    rows = s // GRID_W
    row_ids = np.repeat(np.arange(rows, dtype=f32), GRID_W)
    col_ids = np.tile(np.arange(GRID_W, dtype=f32), rows)
    freqs = (f32(ROPE_THETA) ** (-np.arange(0, ROPE_AXIS_DIM, 2, dtype=f32) / f32(ROPE_AXIS_DIM))).astype(f32)
    ang_r = (row_ids[:, None] * freqs).astype(f32)
    ang_c = (col_ids[:, None] * freqs).astype(f32)
    cr, sr, cc, sc = np.cos(ang_r), np.sin(ang_r), np.cos(ang_c), np.sin(ang_c)
    z = np.zeros_like(sr)
    cos = np.concatenate([cr, cr, cc, cc], axis=1)
    sin_b = np.concatenate([-sr, z, -sc, z], axis=1)
    sin_a = np.concatenate([z, sr, z, sc], axis=1)
    return jnp.asarray(cos, F32), jnp.asarray(sin_a, F32), jnp.asarray(sin_b, F32)


def _attn_kernel(kb_ref, q_ref, k_ref, vt_ref, o_ref, m_ref, acc_ref, al0, al1, s0, s1, p0, p1,
                 qt_ref, mx0, mx1, qa_ref):
    al_ref, s_ref, p_ref, mx_ref = (al0, al1), (s0, s1), (p0, p1), (mx0, mx1)
    nchunk = vt_ref.shape[0]
    tk = vt_ref.shape[2]
    tq = q_ref.shape[0]
    heads = range(Q_GROUP)

    def k_chunk(c):
        return k_ref[pl.ds(pl.multiple_of(c * tk, tk), tk), :]

    def write_out():
        for g in heads:
            acc = acc_ref[g]
            out_t = acc[:HEAD_DIM] / acc[HEAD_DIM:HEAD_DIM + 1]
            o_ref[:, g * HEAD_DIM:(g + 1) * HEAD_DIM] = out_t.T.astype(BF16)

    row0 = lax.broadcasted_iota(jnp.int32, (HEAD_DIM, tq), 0) == 0
    for g in heads:
        qt = q_ref[:, g * HEAD_DIM:(g + 1) * HEAD_DIM].astype(F32).T
        bound = jnp.sqrt(jnp.sum(qt * qt, axis=0, keepdims=True)) * kb_ref[...]
        qa_ref[g, :HEAD_DIM, :] = qt.astype(BF16)
        qa_ref[g, HEAD_DIM:, :] = jnp.where(row0, -bound, 0.0).astype(BF16)
    kf = ATTN_FAST_KCHUNKS
    tkf = kf * tk
    nfast = nchunk // kf
    ones_col = (lax.broadcasted_iota(jnp.int32, (tkf, HEAD_DIM), 1) == 0).astype(BF16)

    def probs_into(c, slot):
        kc = k_ref[pl.ds(pl.multiple_of(c * tkf, tkf), tkf), :]
        ka = jnp.concatenate([kc, ones_col], axis=1)
        for g in heads:
            p_ref[slot][g] = jnp.exp2(jnp.dot(ka, qa_ref[g], preferred_element_type=F32)).astype(BF16)

    def pv_add(c, slot):
        vc = jnp.concatenate([vt_ref[kf * c + j] for j in range(kf)], axis=1)
        for g in heads:
            acc_ref[g] += jnp.dot(vc, p_ref[slot][g], preferred_element_type=F32)

    acc_ref[...] = jnp.zeros(acc_ref.shape, F32)
    probs_into(0, 0)

    def fast_pair(i, carry):
        probs_into(2 * i + 1, 1)
        pv_add(2 * i, 0)
        probs_into(2 * i + 2, 0)
        pv_add(2 * i + 1, 1)
        return carry

    lax.fori_loop(0, (nfast - 2) // 2, fast_pair, 0)
    probs_into(nfast - 1, 1)
    pv_add(nfast - 2, 0)
    pv_add(nfast - 1, 1)
    write_out()
    den_min = acc_ref[0][HEAD_DIM:HEAD_DIM + 1]
    for g in heads[1:]:
        den_min = jnp.minimum(den_min, acc_ref[g][HEAD_DIM:HEAD_DIM + 1])
    safe = jnp.min(den_min) > DENOMINATOR_GUARD

    def scores_into(c, slot):
        kc = k_chunk(c)
        for g in heads:
            sc = jnp.dot(kc, qt_ref[g], preferred_element_type=F32)
            s_ref[slot][g] = sc
            parts = [jnp.max(sc[r * (tk // 4):(r + 1) * (tk // 4)], axis=0, keepdims=True) for r in range(4)]
            mx_ref[slot][g] = jnp.maximum(jnp.maximum(parts[0], parts[1]), jnp.maximum(parts[2], parts[3]))

    def softmax(slot):
        for g in heads:
            m_old = m_ref[g]
            m_new = jnp.maximum(m_old, mx_ref[slot][g])
            al_ref[slot][g] = jnp.exp2(m_old - m_new)
            m_ref[g] = m_new
            p_ref[slot][g, :tk, :] = jnp.exp2(s_ref[slot][g] - m_new).astype(BF16)

    def pv_update(c, slot):
        vc = vt_ref[c]
        for g in heads:
            acc_ref[g] = al_ref[slot][g] * acc_ref[g] + jnp.dot(vc, p_ref[slot][g, :tk, :],
                                                                preferred_element_type=F32)

    def step(c, slot):
        scores_into(c + 1, 1 - slot)
        pv_update(c - 1, 1 - slot)
        softmax(slot)

    @pl.when(jnp.logical_not(safe))
    def _():
        for g in heads:
            qt_ref[g] = qa_ref[g, :HEAD_DIM, :]
        scores_into(0, 0)
        m_ref[...] = jnp.full(m_ref.shape, -jnp.inf, F32)
        acc_ref[...] = jnp.zeros(acc_ref.shape, F32)
        softmax(0)
        scores_into(1, 1)

        def pair(i, carry):
            step(2 * i + 1, 1)
            step(2 * i + 2, 0)
            return carry

        lax.fori_loop(0, (nchunk - 2) // 2, pair, 0)
        pv_update(nchunk - 2, 0)
        softmax(1)
        pv_update(nchunk - 1, 1)
        write_out()


def _attention(q, k, vt, k_norm_bound, tq):
    bsz, s, _ = q.shape
    rows, tk = vt.shape[3], vt.shape[4]
    nchunk = s // tk
    assert nchunk % (2 * ATTN_FAST_KCHUNKS) == 0 and nchunk >= 4 * ATTN_FAST_KCHUNKS
    gw = Q_GROUP * HEAD_DIM
    qspec = pl.BlockSpec((None, tq, gw), lambda b, h, i: (b, i, h))
    kspec = pl.BlockSpec((None, s, HEAD_DIM), lambda b, h, i: (b, 0, h))
    vspec = pl.BlockSpec((None, None, nchunk, rows, tk), lambda b, h, i: (b, h, 0, 0, 0))
    head_vec = pltpu.VMEM((Q_GROUP, 1, tq), F32)
    return pl.pallas_call(
        _attn_kernel,
        grid=(bsz, N_KV_HEADS, s // tq),
        in_specs=[pl.BlockSpec((1, 1), lambda b, h, i: (0, 0)), qspec, kspec, vspec],
        out_specs=qspec,
        out_shape=jax.ShapeDtypeStruct((bsz, s, Q_DIM), BF16),
        scratch_shapes=[head_vec, pltpu.VMEM((Q_GROUP, rows, tq), F32), head_vec, head_vec]
        + [pltpu.VMEM((Q_GROUP, tk, tq), F32)] * 2
        + [pltpu.VMEM((Q_GROUP, ATTN_FAST_KCHUNKS * tk, tq), BF16)] * 2
        + [pltpu.VMEM((Q_GROUP, HEAD_DIM, tq), BF16), head_vec, head_vec,
           pltpu.VMEM((Q_GROUP, 2 * HEAD_DIM, tq), BF16)],
        compiler_params=_cparams(("parallel", "parallel", "parallel")),
        name="attn",
    )(k_norm_bound, q, k, vt)


def _mix_kernel(yf_ref, at_ref, gf_ref, ga_ref, x_ref, g1_ref, sc2_ref, sh2_ref,
                wf_ref, wo_ref, wout_ref, l1g_ref, l1b_ref, wr_ref, br_ref, tri_ref,
                x1_ref, u2_ref, idx_ref, wgt_ref, rank_ref, cnt_ref, cnt_acc):
    first = jnp.logical_and(pl.program_id(0) == 0, pl.program_id(1) == 0)

    @pl.when(first)
    def _():
        cnt_acc[...] = jnp.zeros_like(cnt_acc)

    tm_all = x_ref.shape[0]
    rg = tm_all // MIX_ROW_GROUPS
    u_his, u_los = [], []
    groups = [slice(i * rg, (i + 1) * rg) for i in range(MIX_ROW_GROUPS)]
    ms = []
    for rows in groups:
        yf = jnp.dot(yf_ref[rows, :], wf_ref[...], preferred_element_type=F32)
        ya = jnp.dot(at_ref[rows, :], wo_ref[...], preferred_element_type=F32)
        ms.append((gf_ref[rows, :].astype(F32) * yf + ga_ref[rows, :].astype(F32) * ya).astype(BF16))
    hs = [jnp.dot(m, wout_ref[...], preferred_element_type=F32) for m in ms]
    for rows, h in zip(groups, hs):
        r = DEEPNORM_ALPHA * x_ref[rows, :] + g1_ref[...] * h
        x1 = _layer_norm(r, LN_EPS) * l1g_ref[...] + l1b_ref[...]
        x1_ref[rows, :] = x1
        u2 = _layer_norm(x1, ADA_EPS) * (1.0 + sc2_ref[...]) + sh2_ref[...]
        u2_ref[rows, :] = _pack_bf16_pairs(u2)
        u_hi = u2.astype(BF16)
        u_his.append(u_hi)
        u_los.append((u2 - u_hi.astype(F32)).astype(BF16))

    nt = (((1,), (1,)), ((), ()))
    u_hi = jnp.concatenate(u_his, axis=0)
    u_lo = jnp.concatenate(u_los, axis=0)
    w = wr_ref[...]
    w_hi = w.astype(BF16)
    w_lo = (w - w_hi.astype(F32)).astype(BF16)
    logits = (lax.dot_general(w_hi, u_hi, nt, preferred_element_type=F32)
              + lax.dot_general(w_lo, u_hi, nt, preferred_element_type=F32)
              + lax.dot_general(w_hi, u_lo, nt, preferred_element_type=F32)) + br_ref[...]

    ne, tm = logits.shape
    eid = lax.broadcasted_iota(jnp.int32, (ne, tm), 0)
    work = logits
    vals, idxs, hots = [], [], []
    for _ in range(TOP_K):
        mx = jnp.max(work, axis=0, keepdims=True)
        ix = jnp.min(jnp.where(work == mx, eid, ne), axis=0, keepdims=True)
        hot = eid == ix
        work = jnp.where(hot, -jnp.inf, work)
        vals.append(mx)
        idxs.append(ix)
        hots.append(hot)
    ex = [jnp.exp(v - vals[0]) for v in vals]
    den = ex[0] + ex[1] + ex[2] + ex[3]
    wgt_ref[...] = jnp.concatenate([e / den for e in ex], axis=0)
    idx_ref[...] = jnp.concatenate(idxs, axis=0)

    sel = hots[0] | hots[1] | hots[2] | hots[3]
    mask = jnp.where(sel, 1.0, 0.0)
    prefix = jnp.dot(mask.astype(BF16), tri_ref[...], preferred_element_type=F32)
    pos = prefix + cnt_acc[...]
    ranks = [jnp.sum(jnp.where(hot, pos, 0.0), axis=0, keepdims=True) for hot in hots]
    rank_ref[...] = jnp.concatenate(ranks, axis=0).astype(jnp.int32)
    cnt_acc[...] += jnp.sum(mask, axis=1, keepdims=True)
    cnt_ref[...] = cnt_acc[...].astype(jnp.int32)


def _mix(yfm, attn, gf, ga, x, g1, sc2, sh2, wf, wo, wout, l1g, l1b, wr_t, br, tm, b0, bsz):
    _, s, d = x.shape
    n = bsz * s
    nt = s // tm
    tri = (jnp.arange(tm)[:, None] < jnp.arange(tm)[None, :]).astype(BF16)
    row_in = lambda w: pl.BlockSpec((None, tm, w), lambda b, i: (b + b0, i, 0))
    row = lambda w: pl.BlockSpec((None, tm, w), lambda b, i: (b, i, 0))
    mod = pl.BlockSpec((None, 1, d), lambda b, i: (b + b0, 0, 0))
    full = lambda shp: pl.BlockSpec(shp, lambda b, i: (0,) * len(shp))
    tok = pl.BlockSpec((TOP_K, tm), lambda b, i: (0, b * nt + i))
    return pl.pallas_call(
        _mix_kernel,
        grid=(bsz, nt),
        in_specs=[row_in(FOURIER_DIM), row_in(Q_DIM), row_in(d), row_in(d), row_in(d), mod, mod, mod,
                  full(wf.shape), full(wo.shape), full(wout.shape), full((1, d)), full((1, d)),
                  full(wr_t.shape), full((N_EXPERTS, 1)), full((tm, tm))],
        out_specs=[row(d), row(d // 2), tok, tok, tok, full((N_EXPERTS, 1))],
        out_shape=[jax.ShapeDtypeStruct((bsz, s, d), F32),
                   jax.ShapeDtypeStruct((bsz, s, d // 2), jnp.int32),
                   jax.ShapeDtypeStruct((TOP_K, n), jnp.int32),
                   jax.ShapeDtypeStruct((TOP_K, n), F32),
                   jax.ShapeDtypeStruct((TOP_K, n), jnp.int32),
                   jax.ShapeDtypeStruct((N_EXPERTS, 1), jnp.int32)],
        scratch_shapes=[pltpu.VMEM((N_EXPERTS, 1), F32)],
        compiler_params=_cparams(("arbitrary", "arbitrary")),
        name="mix",
    )(yfm, attn, gf, ga, x, g1, sc2, sh2, wf, wo, wout, l1g, l1b, wr_t, br, tri)


def _expert_kernel(be_ref, nv_ref, new_ref, xs_ref, wgu_ref, bgu_ref, wd_ref, bd_ref, ys_ref,
                   wgu_bf, wd_bf):
    del be_ref
    i = pl.program_id(0)
    nvalid = nv_ref[i]

    @pl.when(new_ref[i] == 1)
    def _():
        wgu_bf[...] = wgu_ref[...].astype(BF16)
        wd_bf[...] = wd_ref[...].astype(BF16)

    @pl.when(nvalid > 0)
    def _():
        xw = xs_ref[...]
        row = lax.broadcasted_iota(jnp.int32, xw.shape, 0)
        lo, hi = _unpack_bf16_pairs(jnp.where(row < nvalid, xw, 0))
        xb = jnp.concatenate([lo.astype(BF16), hi.astype(BF16)], axis=1)
        h = jnp.dot(xb, wgu_bf[...], preferred_element_type=F32) + bgu_ref[...]
        gate = jnp.minimum(h[:, :D_EXPERT], SWIGLU_LIMIT)
        up = jnp.clip(h[:, D_EXPERT:], -SWIGLU_LIMIT, SWIGLU_LIMIT)
        act = (up + 1.0) * gate * jax.nn.sigmoid(SWIGLU_ALPHA * gate)
        y = jnp.dot(act.astype(BF16), wd_bf[...], preferred_element_type=F32) + bd_ref[...]
        ys_ref[...] = _pack_bf16_pairs(y)

    @pl.when(nvalid == 0)
    def _():
        ys_ref[...] = jnp.zeros_like(ys_ref)


def _experts(block_e, block_nvalid, block_new, xs, wgu, bgu, wd, bd, tmb):
    n_pad, dh = xs.shape
    d = 2 * dh
    grid_spec = pltpu.PrefetchScalarGridSpec(
        num_scalar_prefetch=3,
        grid=(n_pad // tmb,),
        in_specs=[pl.BlockSpec((tmb, dh), lambda i, be, nv, nw: (i, 0)),
                  pl.BlockSpec((None, d, 2 * D_EXPERT), lambda i, be, nv, nw: (be[i], 0, 0)),
                  pl.BlockSpec((None, 1, 2 * D_EXPERT), lambda i, be, nv, nw: (be[i], 0, 0)),
                  pl.BlockSpec((None, D_EXPERT, d), lambda i, be, nv, nw: (be[i], 0, 0)),
                  pl.BlockSpec((None, 1, d), lambda i, be, nv, nw: (be[i], 0, 0))],
        out_specs=pl.BlockSpec((tmb, dh), lambda i, be, nv, nw: (i, 0)),
        scratch_shapes=[pltpu.VMEM((d, 2 * D_EXPERT), BF16), pltpu.VMEM((D_EXPERT, d), BF16)],
    )
    return pl.pallas_call(
        _expert_kernel,
        grid_spec=grid_spec,
        out_shape=jax.ShapeDtypeStruct((n_pad, dh), jnp.int32),
        compiler_params=_cparams(("arbitrary",)),
        name="experts",
    )(block_e, block_nvalid, block_new, xs, wgu, bgu, wd, bd)


def _final_kernel(yg_ref, w_ref, x1_ref, g2_ref, lg_ref, lb_ref, o_ref):
    w = w_ref[...]
    acc_lo = None
    for j in range(TOP_K):
        lo, hi = _unpack_bf16_pairs(yg_ref[j])
        wj = w[:, j:j + 1]
        acc_lo = lo * wj if acc_lo is None else acc_lo + lo * wj
        acc_hi = hi * wj if j == 0 else acc_hi + hi * wj
    h = jnp.concatenate([acc_lo, acc_hi], axis=1)
    r = DEEPNORM_ALPHA * x1_ref[...] + g2_ref[...] * h
    o_ref[...] = _layer_norm(r, LN_EPS) * lg_ref[...] + lb_ref[...]


def _final(yg, w_tok, x1, g2, lg, lb, tm, b0, total_bsz, out_so_far):
    bsz, s, d = x1.shape
    nt = s // tm
    row = pl.BlockSpec((None, tm, d), lambda b, i: (b, i, 0))
    full = pl.BlockSpec((1, d), lambda b, i: (0, 0))
    in_specs = [pl.BlockSpec((TOP_K, tm, d // 2), lambda b, i: (0, b * nt + i, 0)),
                pl.BlockSpec((tm, TOP_K), lambda b, i: (b * nt + i, 0)),
                row, pl.BlockSpec((None, 1, d), lambda b, i: (b + b0, 0, 0)), full, full]
    args = [yg, w_tok, x1, g2, lg, lb]
    aliases = {}
    if out_so_far is not None:
        in_specs.append(pl.BlockSpec(memory_space=pl.ANY))
        args.append(out_so_far)
        aliases = {len(args) - 1: 0}
    body = _final_kernel if out_so_far is None else (lambda *refs: _final_kernel(*refs[:6], refs[7]))
    return pl.pallas_call(
        body,
        grid=(bsz, nt),
        in_specs=in_specs,
        out_specs=pl.BlockSpec((None, tm, d), lambda b, i: (b + b0, i, 0)),
        out_shape=jax.ShapeDtypeStruct((total_bsz, s, d), F32),
        input_output_aliases=aliases,
        compiler_params=_cparams(("parallel", "parallel")),
        name="final",
    )(*args)


def _scatter_rows(rows, dest, m_out):
    n, w = rows.shape
    nj = dest.shape[0]
    ch = GATHER_CHUNK
    nw = SC_CORES * SC_SUBCORES
    nch = n // (nw * ch)
    assert n == nw * nch * ch and nch % 2 == 0
    per_w = nch * ch
    mesh = plsc.VectorSubcoreMesh(core_axis_name="c", subcore_axis_name="s",
                                  num_cores=SC_CORES, num_subcores=SC_SUBCORES)

    @functools.partial(
        pl.kernel, mesh=mesh,
        out_type=jax.ShapeDtypeStruct((m_out, w), rows.dtype),
        scratch_types=[pltpu.VMEM((nj, nch, ch), jnp.int32), pltpu.VMEM((2, ch, w), rows.dtype),
                       pltpu.SemaphoreType.DMA((2,)), pltpu.SemaphoreType.DMA((2,))],
    )
    def scatter_kernel(rows_hbm, dest_hbm, out_hbm, idx_v, rows_v, rsem, ssem):
        wid = lax.axis_index("s") * SC_CORES + lax.axis_index("c")
        base = wid * per_w
        for j in range(nj):
            pltpu.sync_copy(dest_hbm.at[j, wid], idx_v.at[j])

        def read(i, b):
            off = pl.multiple_of(base + i * ch, ch)
            return pltpu.make_async_copy(rows_hbm.at[pl.ds(off, ch)], rows_v.at[b], rsem.at[b])

        def scatter(i, b, j):
            return pltpu.make_async_copy(rows_v.at[b], out_hbm.at[idx_v.at[j, i]], ssem.at[b])

        read(0, 0).start()

        @pl.loop(0, nch, step=2)
        def _(i0):
            for b in range(2):
                i = i0 + b
                read(i, b).wait()

                @pl.when(i + 1 < nch)
                def _():
                    @pl.when(i >= 1)
                    def _():
                        for j in range(nj):
                            scatter(i - 1, 1 - b, j).wait()
                    read(i + 1, 1 - b).start()

                for j in range(nj):
                    scatter(i, b, j).start()

        for b in range(2):
            for j in range(nj):
                scatter(nch - 2 + b, b, j).wait()

    return scatter_kernel(rows, dest.reshape(nj, nw, nch, ch))


def _gather_rows(table, idx):
    m = idx.shape[0]
    w = table.shape[1]
    ch = GATHER_CHUNK
    nw = SC_CORES * SC_SUBCORES
    nch = m // (nw * ch)
    assert m == nw * nch * ch and nch % 2 == 0
    per_w = nch * ch
    mesh = plsc.VectorSubcoreMesh(core_axis_name="c", subcore_axis_name="s",
                                  num_cores=SC_CORES, num_subcores=SC_SUBCORES)

    @functools.partial(
        pl.kernel, mesh=mesh,
        out_type=jax.ShapeDtypeStruct((m, w), table.dtype),
        scratch_types=[pltpu.VMEM((nch, ch), jnp.int32), pltpu.VMEM((2, ch, w), table.dtype),
                       pltpu.SemaphoreType.DMA((2,)), pltpu.SemaphoreType.DMA((2,))],
    )
    def gather_kernel(table_hbm, idx_hbm, out_hbm, idx_v, rows_v, gsem, wsem):
        wid = lax.axis_index("s") * SC_CORES + lax.axis_index("c")
        base = wid * per_w
        pltpu.sync_copy(idx_hbm.at[wid], idx_v)

        def gather(j, b):
            return pltpu.make_async_copy(table_hbm.at[idx_v.at[j]], rows_v.at[b], gsem.at[b])

        def write(j, b):
            off = pl.multiple_of(base + j * ch, ch)
            return pltpu.make_async_copy(rows_v.at[b], out_hbm.at[pl.ds(off, ch)], wsem.at[b])

        gather(0, 0).start()

        @pl.loop(0, nch, step=2)
        def _(i):
            for b in range(2):
                j = i + b
                gather(j, b).wait()

                @pl.when(j + 1 < nch)
                def _():
                    @pl.when(j >= 1)
                    def _():
                        write(j - 1, 1 - b).wait()
                    gather(j + 1, 1 - b).start()

                write(j, b).start()

        write(nch - 2, 0).wait()
        write(nch - 1, 1).wait()

    return gather_kernel(table, idx.reshape(nw, nch, ch))


def kernel(x, c, w_ada, b_ada, w_in, q_norm_g, k_norm_g, w_fourier, w_attn_o, w_out, ln1_g, ln1_b,
           w_router, b_router, w_gate_up, b_gate_up, w_down, b_down, ln2_g, ln2_b):
    bsz, s, d = x.shape
    n = bsz * s
    tm = min(ROW_TILE, s)
    l = 0

    mod = _ada(c, w_ada[l], b_ada[l])
    sh1, sc1, g1, sh2, sc2, g2 = [m[:, None, :] for m in jnp.split(mod, N_MOD, axis=-1)]

    dft_c, m1, m2, cw, sw, p, q = _dft_tables(s)
    cos, sin_a, sin_b = _rope_tables(s)

    fa, fb, qh, kh, vt, gf, ga = _inproj(
        x, sc1, sh1, w_in[l].astype(BF16), dft_c,
        q_norm_g[l].reshape(1, HEAD_DIM), k_norm_g[l].reshape(1, HEAD_DIM), cos, sin_a, sin_b, tm)

    zr, zi = _fft1(fa, fb, m1, cw, sw, p, q)
    yfm = _fft2(zr, zi, m2)

    k_bound = (math.sqrt(HEAD_DIM) * (1.0 + 2.0 ** -7)) * jnp.max(jnp.abs(k_norm_g[l])).reshape(1, 1)
    attn = _attention(qh, kh, vt, k_bound, min(ATTN_TQ, s))

    wf_b, wo_b, wout_b = w_fourier[l].astype(BF16), w_attn_o[l].astype(BF16), w_out[l].astype(BF16)
    bgu, bd = b_gate_up[l][:, None, :], b_down[l][:, None, :]
    tmb = MOE_BLOCK
    eids = jnp.arange(N_EXPERTS, dtype=jnp.int32)

    ngroups = MOE_BATCH_GROUPS if bsz % MOE_BATCH_GROUPS == 0 else 1
    gb = bsz // ngroups
    ng = gb * s
    out = None
    for grp in range(ngroups):
        b0 = grp * gb
        x1, u2p, idx_t, wgt_t, rank_t, counts = _mix(
            yfm, attn, gf, ga, x, g1, sc2, sh2, wf_b, wo_b, wout_b,
            ln1_g[l].reshape(1, d), ln1_b[l].reshape(1, d),
            w_router[l].T, b_router[l].reshape(N_EXPERTS, 1), tm, b0, gb)

        counts = counts.reshape(N_EXPERTS)
        pcounts = ((counts + tmb - 1) // tmb) * tmb
        pends = jnp.cumsum(pcounts)
        pstarts = pends - pcounts
        n_pad = ng * TOP_K + N_EXPERTS * tmb
        nblk = n_pad // tmb
        start_of = jnp.sum(jnp.where(idx_t[None] == eids[:, None, None], pstarts[:, None, None], 0), axis=0)
        dest_t = start_of + rank_t
        block_starts = jnp.arange(nblk, dtype=jnp.int32) * tmb
        block_e = jnp.sum((block_starts[:, None] >= pends[None, :]).astype(jnp.int32), axis=1)
        block_e = jnp.minimum(block_e, N_EXPERTS - 1)
        valid_end = jnp.sum(jnp.where(block_e[:, None] == eids[None, :], (pstarts + counts)[None, :], 0), axis=1)
        block_nvalid = jnp.clip(valid_end - block_starts, 0, tmb)
        block_new = jnp.concatenate([jnp.ones((1,), jnp.int32),
                                     (block_e[1:] != block_e[:-1]).astype(jnp.int32)])

        xs = _scatter_rows(u2p.reshape(ng, d // 2), dest_t, n_pad)
        ys = _experts(block_e, block_nvalid, block_new, xs, w_gate_up[l], bgu, w_down[l], bd, tmb)
        yg = _gather_rows(ys, dest_t.reshape(-1)).reshape(TOP_K, ng, d // 2)
        out = _final(yg, wgt_t.T, x1, g2, ln2_g[l].reshape(1, d), ln2_b[l].reshape(1, d), tm,
                     b0, bsz, out)
    return out
```

```python
import functools
import math

import jax
import jax.numpy as jnp
import numpy as np
from jax import lax
from jax.experimental import pallas as pl
from jax.experimental.pallas import tpu as pltpu
from jax.experimental.pallas import tpu_sc as plsc

F32 = jnp.float32
BF16 = jnp.bfloat16

D_MODEL = 1024
GRID_W = 64
N_Q_HEADS = 8
N_KV_HEADS = 2
HEAD_DIM = 128
Q_GROUP = N_Q_HEADS // N_KV_HEADS
ROPE_THETA = 10000.0
ROPE_AXIS_DIM = HEAD_DIM // 2
N_FOURIER_GROUPS = 4
FOURIER_GROUP_DIM = 128
FOURIER_DIM = N_FOURIER_GROUPS * FOURIER_GROUP_DIM
Q_DIM = N_Q_HEADS * HEAD_DIM
KV_DIM = N_KV_HEADS * HEAD_DIM
N_EXPERTS = 32
TOP_K = 4
D_EXPERT = 1024
SWIGLU_LIMIT = 7.0
SWIGLU_ALPHA = 1.702
N_MOD = 6
DEPTH = 1
DEEPNORM_ALPHA = (2 * DEPTH) ** 0.25
LN_EPS = 1e-5
ADA_EPS = 1e-6
QK_EPS = 1e-6
LOG2E = 1.4426950408889634

V7X_VMEM_BYTES = 64 * 1024 * 1024
VMEM_LIMIT = V7X_VMEM_BYTES - 8 * 1024 * 1024
LANES = 128
BF16_SUBLANES = 16

FFT_Q = 128
ROW_TILE = 512
ATTN_TQ = 512
ATTN_FAST_KCHUNKS = 2
DENOMINATOR_GUARD = 2.0 ** -80
INPROJ_ROW_GROUPS = 2
MIX_ROW_GROUPS = 2
MOE_BLOCK = 512
MOE_BATCH_GROUPS = 2

SC_CORES = 2
SC_SUBCORES = 16
GATHER_CHUNK = 64


def _cparams(sem):
    return pltpu.CompilerParams(dimension_semantics=sem, vmem_limit_bytes=VMEM_LIMIT)


def _layer_norm(x, eps):
    mu = jnp.mean(x, axis=-1, keepdims=True)
    xc = x - mu
    var = jnp.mean(xc * xc, axis=-1, keepdims=True)
    return xc * lax.rsqrt(var + eps)


def _pack_bf16_pairs(x):
    n = x.shape[1] // 2
    bits = pltpu.bitcast(x.astype(BF16).astype(F32), jnp.uint32)
    word = (bits[:, :n] >> 16) | (bits[:, n:] & jnp.uint32(0xFFFF0000))
    return pltpu.bitcast(word, jnp.int32)


def _unpack_bf16_pairs(w):
    bits = pltpu.bitcast(w, jnp.uint32)
    lo = pltpu.bitcast(bits << 16, F32)
    hi = pltpu.bitcast(bits & jnp.uint32(0xFFFF0000), F32)
    return lo, hi


def _ada_kernel(c_ref, w_ref, b_ref, o_ref):
    c = c_ref[...]
    cond = c * jax.nn.sigmoid(c)
    o_ref[...] = jnp.dot(cond, w_ref[...], preferred_element_type=F32,
                         precision=lax.Precision.HIGHEST) + b_ref[...]


def _ada(c, w_ada, b_ada):
    bsz, d = c.shape
    n = w_ada.shape[1]
    tn = 1536
    return pl.pallas_call(
        _ada_kernel,
        grid=(n // tn,),
        in_specs=[pl.BlockSpec((bsz, d), lambda j: (0, 0)),
                  pl.BlockSpec((d, tn), lambda j: (0, j)),
                  pl.BlockSpec((1, tn), lambda j: (0, j))],
        out_specs=pl.BlockSpec((bsz, tn), lambda j: (0, j)),
        out_shape=jax.ShapeDtypeStruct((bsz, n), F32),
        compiler_params=_cparams(("arbitrary",)),
        name="ada",
    )(c, w_ada, b_ada.reshape(1, n))


def _rope(xh, cos, sin_a, sin_b):
    a = pltpu.roll(xh, 32, 1)
    b = pltpu.roll(xh, 96, 1)
    return xh * cos + b * sin_b + a * sin_a


def _inproj_kernel(x_ref, sc_ref, sh_ref, w_ref, dft_ref, qg_ref, kg_ref,
                   cos_ref, sa_ref, sb_ref,
                   fa_ref, fb_ref, q_ref, k_ref, vt_ref, gf_ref, ga_ref):
    tm = x_ref.shape[0]
    rg = tm // INPROJ_ROW_GROUPS
    q_scale = LOG2E / math.sqrt(HEAD_DIM)
    for i in range(INPROJ_ROW_GROUPS):
        rows = slice(i * rg, (i + 1) * rg)
        u = _layer_norm(x_ref[rows, :], ADA_EPS) * (1.0 + sc_ref[...]) + sh_ref[...]
        ub = u.astype(BF16)
        cos, sa, sb = cos_ref[rows, :], sa_ref[rows, :], sb_ref[rows, :]

        def norm_rope(z, gain, scale):
            ms = jnp.mean(z * z, axis=-1, keepdims=True)
            zn = z * lax.rsqrt(ms + QK_EPS) * gain
            return (_rope(zn, cos, sa, sb) * scale).astype(BF16)

        c0 = 0
        zf = jnp.dot(ub, w_ref[:, c0:c0 + FOURIER_DIM], preferred_element_type=F32).astype(BF16)
        for g in range(N_FOURIER_GROUPS):
            sl = slice(g * FOURIER_GROUP_DIM, (g + 1) * FOURIER_GROUP_DIM)
            ab = jnp.dot(zf[:, sl], dft_ref[...], preferred_element_type=F32)
            fa_ref[rows, sl] = ab[:, :FOURIER_GROUP_DIM].astype(BF16)
            fb_ref[rows, sl] = ab[:, FOURIER_GROUP_DIM:].astype(BF16)
        c0 += FOURIER_DIM

        zq = jnp.dot(ub, w_ref[:, c0:c0 + Q_DIM], preferred_element_type=F32)
        for h in range(N_Q_HEADS):
            sl = slice(h * HEAD_DIM, (h + 1) * HEAD_DIM)
            q_ref[rows, sl] = norm_rope(zq[:, sl], qg_ref[...], q_scale)
        c0 += Q_DIM

        zk = jnp.dot(ub, w_ref[:, c0:c0 + KV_DIM], preferred_element_type=F32)
        for h in range(N_KV_HEADS):
            sl = slice(h * HEAD_DIM, (h + 1) * HEAD_DIM)
            k_ref[rows, sl] = norm_rope(zk[:, sl], kg_ref[...], 1.0)
        c0 += KV_DIM

        zv = jnp.dot(ub, w_ref[:, c0:c0 + KV_DIM], preferred_element_type=F32)
        for h in range(N_KV_HEADS):
            vt_ref[h, :, rows] = zv[:, h * HEAD_DIM:(h + 1) * HEAD_DIM].T.astype(BF16)
        c0 += KV_DIM

        zg = jnp.dot(ub, w_ref[:, c0:c0 + D_MODEL], preferred_element_type=F32)
        gf_ref[rows, :] = jax.nn.sigmoid(zg).astype(BF16)
        c0 += D_MODEL
        zg = jnp.dot(ub, w_ref[:, c0:c0 + D_MODEL], preferred_element_type=F32)
        ga_ref[rows, :] = jax.nn.sigmoid(zg).astype(BF16)


def _inproj(x, sc1, sh1, w_in, dft_c, qg, kg, cos, sin_a, sin_b, tm):
    bsz, s, d = x.shape
    ncol = w_in.shape[1]
    row = lambda w: pl.BlockSpec((None, tm, w), lambda b, i: (b, i, 0))
    mod = pl.BlockSpec((None, 1, d), lambda b, i: (b, 0, 0))
    full = lambda shp: pl.BlockSpec(shp, lambda b, i: (0,) * len(shp))
    tab = pl.BlockSpec((tm, HEAD_DIM), lambda b, i: (i, 0))
    vt_rows = HEAD_DIM
    vt_spec = pl.BlockSpec((None, N_KV_HEADS, None, vt_rows, tm), lambda b, i: (b, 0, i, 0, 0))
    vt_shape = jax.ShapeDtypeStruct((bsz, N_KV_HEADS, s // tm, vt_rows, tm), BF16)
    rows_out = lambda w: (row(w), jax.ShapeDtypeStruct((bsz, s, w), BF16))
    outs = [rows_out(FOURIER_DIM), rows_out(FOURIER_DIM), rows_out(Q_DIM), rows_out(KV_DIM),
            (vt_spec, vt_shape), rows_out(D_MODEL), rows_out(D_MODEL)]
    return pl.pallas_call(
        _inproj_kernel,
        grid=(bsz, s // tm),
        in_specs=[row(d), mod, mod, full((d, ncol)), full(dft_c.shape),
                  full((1, HEAD_DIM)), full((1, HEAD_DIM)), tab, tab, tab],
        out_specs=[o[0] for o in outs],
        out_shape=[o[1] for o in outs],
        compiler_params=_cparams(("parallel", "parallel")),
        name="inproj",
    )(x, sc1, sh1, w_in, dft_c, qg, kg, cos, sin_a, sin_b)


def _fft1_kernel(a_ref, b_ref, m_ref, cw_ref, sw_ref, zr_ref, zi_ref, a32, b32):
    p, nb, _ = a_ref.shape
    a32[...] = a_ref[...].astype(F32)
    b32[...] = b_ref[...].astype(F32)
    for j in range(nb):
        ab = jnp.concatenate([a32[:, j, :], b32[:, j, :]], axis=0).astype(BF16)
        t = jnp.dot(m_ref[...], ab, preferred_element_type=F32)
        tr, ti = t[:p], t[p:]
        cw, sw = cw_ref[j], sw_ref[j]
        zr_ref[j] = (tr * cw - ti * sw).astype(BF16)
        zi_ref[j] = (ti * cw + tr * sw).astype(BF16)


def _fft1(fa, fb, m1, cw, sw, p, q):
    bsz, s, c = fa.shape
    nb = BF16_SUBLANES
    inp = pl.BlockSpec((None, p, nb, c), lambda b, j: (b, 0, j, 0))
    tw = pl.BlockSpec((nb, p, 1), lambda b, j: (j, 0, 0))
    out = pl.BlockSpec((None, nb, p, c), lambda b, j: (b, j, 0, 0))
    return pl.pallas_call(
        _fft1_kernel,
        grid=(bsz, q // nb),
        in_specs=[inp, inp, pl.BlockSpec(m1.shape, lambda b, j: (0, 0)), tw, tw],
        out_specs=[out, out],
        out_shape=[jax.ShapeDtypeStruct((bsz, q, p, c), BF16)] * 2,
        scratch_shapes=[pltpu.VMEM((p, nb, c), F32)] * 2,
        compiler_params=_cparams(("parallel", "parallel")),
        name="fft1",
    )(fa.reshape(bsz, p, q, c), fb.reshape(bsz, p, q, c), m1, cw, sw)


def _fft2_kernel(zr_ref, zi_ref, m_ref, y_ref, zr32, zi32, y32):
    zr32[...] = zr_ref[...].astype(F32)
    zi32[...] = zi_ref[...].astype(F32)
    for j in range(zr_ref.shape[1]):
        z = jnp.concatenate([zr32[:, j, :], zi32[:, j, :]], axis=0).astype(BF16)
        y32[:, j, :] = jnp.dot(m_ref[...], z, preferred_element_type=F32)
    y_ref[...] = y32[...].astype(BF16)


def _fft2(zr, zi, m2):
    bsz, q, p, c = zr.shape
    pc = BF16_SUBLANES
    blk = pl.BlockSpec((None, q, pc, c), lambda b, j: (b, 0, j, 0))
    y = pl.pallas_call(
        _fft2_kernel,
        grid=(bsz, p // pc),
        in_specs=[blk, blk, pl.BlockSpec(m2.shape, lambda b, j: (0, 0))],
        out_specs=blk,
        out_shape=jax.ShapeDtypeStruct((bsz, q, p, c), BF16),
        scratch_shapes=[pltpu.VMEM((q, pc, c), F32)] * 3,
        compiler_params=_cparams(("parallel", "parallel")),
        name="fft2",
    )(zr, zi, m2)
    return y.reshape(bsz, q * p, c)


def _dft_tables(s):
    q = FFT_Q
    p = s // q

    def cs(n_rows, n_cols, n):
        r = np.arange(n_rows, dtype=np.int64)[:, None]
        c = np.arange(n_cols, dtype=np.int64)[None, :]
        ang = ((r * c) % n).astype(np.float64) * (2.0 * math.pi / n)
        return np.cos(ang).astype(np.float32), np.sin(ang).astype(np.float32)

    cc, sc = cs(FOURIER_GROUP_DIM, FOURIER_GROUP_DIM, FOURIER_GROUP_DIM)
    dft_c = jnp.asarray(np.concatenate([cc, sc], axis=1), dtype=BF16)
    cp, sp = cs(p, p, p)
    m1 = jnp.asarray(np.block([[cp, -sp], [sp, cp]]), dtype=BF16)
    cq, sq = cs(q, q, q)
    scale = np.float32(1.0 / math.sqrt(s * FOURIER_GROUP_DIM))
    m2 = jnp.asarray(np.concatenate([cq, -sq], axis=1) * scale, dtype=BF16)
    cw, sw = cs(q, p, s)
    return dft_c, m1, m2, jnp.asarray(cw[:, :, None]), jnp.asarray(sw[:, :, None]), p, q


def _rope_tables(s):
    f32 = np.float32
    rows = s // GRID_W
    row_ids = np.repeat(np.arange(rows, dtype=f32), GRID_W)
    col_ids = np.tile(np.arange(GRID_W, dtype=f32), rows)
    freqs = (f32(ROPE_THETA) ** (-np.arange(0, ROPE_AXIS_DIM, 2, dtype=f32) / f32(ROPE_AXIS_DIM))).astype(f32)
    ang_r = (row_ids[:, None] * freqs).astype(f32)
    ang_c = (col_ids[:, None] * freqs).astype(f32)
    cr, sr, cc, sc = np.cos(ang_r), np.sin(ang_r), np.cos(ang_c), np.sin(ang_c)
    z = np.zeros_like(sr)
    cos = np.concatenate([cr, cr, cc, cc], axis=1)
    sin_b = np.concatenate([-sr, z, -sc, z], axis=1)
    sin_a = np.concatenate([z, sr, z, sc], axis=1)
    return jnp.asarray(cos, F32), jnp.asarray(sin_a, F32), jnp.asarray(sin_b, F32)


def _attn_kernel(kb_ref, q_ref, k_ref, vt_ref, o_ref, m_ref, acc_ref, al0, al1, s0, s1, p0, p1,
                 qt_ref, mx0, mx1, qa_ref, l_ref, ps0, ps1):
    al_ref, s_ref, p_ref, mx_ref, ps_ref = (al0, al1), (s0, s1), (p0, p1), (mx0, mx1), (ps0, ps1)
    nchunk = vt_ref.shape[0]
    tk = vt_ref.shape[2]
    tq = q_ref.shape[0]
    heads = range(Q_GROUP)

    def k_chunk(c):
        return k_ref[pl.ds(pl.multiple_of(c * tk, tk), tk), :]

    def write_out():
        for g in heads:
            out_t = acc_ref[g] / l_ref[g]
            o_ref[:, g * HEAD_DIM:(g + 1) * HEAD_DIM] = out_t.T.astype(BF16)

    row0 = lax.broadcasted_iota(jnp.int32, (HEAD_DIM, tq), 0) == 0
    for g in heads:
        qt = q_ref[:, g * HEAD_DIM:(g + 1) * HEAD_DIM].astype(F32).T
        bound = jnp.sqrt(jnp.sum(qt * qt, axis=0, keepdims=True)) * kb_ref[...]
        qa_ref[g, :HEAD_DIM, :] = qt.astype(BF16)
        qa_ref[g, HEAD_DIM:, :] = jnp.where(row0, -bound, 0.0).astype(BF16)
    kf = ATTN_FAST_KCHUNKS
    tkf = kf * tk
    nfast = nchunk // kf
    ones_col = (lax.broadcasted_iota(jnp.int32, (tkf, HEAD_DIM), 1) == 0).astype(BF16)

    def probs_into(c, slot):
        kc = k_ref[pl.ds(pl.multiple_of(c * tkf, tkf), tkf), :]
        ka = jnp.concatenate([kc, ones_col], axis=1)
        for g in heads:
            p32 = jnp.exp2(jnp.dot(ka, qa_ref[g], preferred_element_type=F32))
            l_ref[g] += jnp.sum(p32, axis=0, keepdims=True)
            p_ref[slot][g] = p32.astype(BF16)

    def pv_add(c, slot):
        vc = jnp.concatenate([vt_ref[kf * c + j] for j in range(kf)], axis=1)
        for g in heads:
            acc_ref[g] += jnp.dot(vc, p_ref[slot][g], preferred_element_type=F32)

    acc_ref[...] = jnp.zeros(acc_ref.shape, F32)
    l_ref[...] = jnp.zeros(l_ref.shape, F32)
    probs_into(0, 0)

    def fast_pair(i, carry):
        probs_into(2 * i + 1, 1)
        pv_add(2 * i, 0)
        probs_into(2 * i + 2, 0)
        pv_add(2 * i + 1, 1)
        return carry

    lax.fori_loop(0, (nfast - 2) // 2, fast_pair, 0)
    probs_into(nfast - 1, 1)
    pv_add(nfast - 2, 0)
    pv_add(nfast - 1, 1)
    write_out()
    den_min = l_ref[0]
    for g in heads[1:]:
        den_min = jnp.minimum(den_min, l_ref[g])
    safe = jnp.min(den_min) > DENOMINATOR_GUARD

    def scores_into(c, slot):
        kc = k_chunk(c)
        for g in heads:
            sc = jnp.dot(kc, qt_ref[g], preferred_element_type=F32)
            s_ref[slot][g] = sc
            parts = [jnp.max(sc[r * (tk // 4):(r + 1) * (tk // 4)], axis=0, keepdims=True) for r in range(4)]
            mx_ref[slot][g] = jnp.maximum(jnp.maximum(parts[0], parts[1]), jnp.maximum(parts[2], parts[3]))

    def softmax(slot):
        for g in heads:
            m_old = m_ref[g]
            m_new = jnp.maximum(m_old, mx_ref[slot][g])
            al_ref[slot][g] = jnp.exp2(m_old - m_new)
            m_ref[g] = m_new
            p32 = jnp.exp2(s_ref[slot][g] - m_new)
            ps_ref[slot][g] = jnp.sum(p32, axis=0, keepdims=True)
            p_ref[slot][g, :tk, :] = p32.astype(BF16)

    def pv_update(c, slot):
        vc = vt_ref[c]
        for g in heads:
            acc_ref[g] = al_ref[slot][g] * acc_ref[g] + jnp.dot(vc, p_ref[slot][g, :tk, :],
                                                                preferred_element_type=F32)
            l_ref[g] = al_ref[slot][g] * l_ref[g] + ps_ref[slot][g]

    def step(c, slot):
        scores_into(c + 1, 1 - slot)
        pv_update(c - 1, 1 - slot)
        softmax(slot)

    @pl.when(jnp.logical_not(safe))
    def _():
        for g in heads:
            qt_ref[g] = qa_ref[g, :HEAD_DIM, :]
        scores_into(0, 0)
        m_ref[...] = jnp.full(m_ref.shape, -jnp.inf, F32)
        acc_ref[...] = jnp.zeros(acc_ref.shape, F32)
        l_ref[...] = jnp.zeros(l_ref.shape, F32)
        softmax(0)
        scores_into(1, 1)

        def pair(i, carry):
            step(2 * i + 1, 1)
            step(2 * i + 2, 0)
            return carry

        lax.fori_loop(0, (nchunk - 2) // 2, pair, 0)
        pv_update(nchunk - 2, 0)
        softmax(1)
        pv_update(nchunk - 1, 1)
        write_out()


def _attention(q, k, vt, k_norm_bound, tq):
    bsz, s, _ = q.shape
    rows, tk = vt.shape[3], vt.shape[4]
    nchunk = s // tk
    assert nchunk % (2 * ATTN_FAST_KCHUNKS) == 0 and nchunk >= 4 * ATTN_FAST_KCHUNKS
    gw = Q_GROUP * HEAD_DIM
    qspec = pl.BlockSpec((None, tq, gw), lambda b, h, i: (b, i, h))
    kspec = pl.BlockSpec((None, s, HEAD_DIM), lambda b, h, i: (b, 0, h))
    vspec = pl.BlockSpec((None, None, nchunk, rows, tk), lambda b, h, i: (b, h, 0, 0, 0))
    head_vec = pltpu.VMEM((Q_GROUP, 1, tq), F32)
    return pl.pallas_call(
        _attn_kernel,
        grid=(bsz, N_KV_HEADS, s // tq),
        in_specs=[pl.BlockSpec((1, 1), lambda b, h, i: (0, 0)), qspec, kspec, vspec],
        out_specs=qspec,
        out_shape=jax.ShapeDtypeStruct((bsz, s, Q_DIM), BF16),
        scratch_shapes=[head_vec, pltpu.VMEM((Q_GROUP, rows, tq), F32), head_vec, head_vec]
        + [pltpu.VMEM((Q_GROUP, tk, tq), F32)] * 2
        + [pltpu.VMEM((Q_GROUP, ATTN_FAST_KCHUNKS * tk, tq), BF16)] * 2
        + [pltpu.VMEM((Q_GROUP, HEAD_DIM, tq), BF16), head_vec, head_vec,
           pltpu.VMEM((Q_GROUP, 2 * HEAD_DIM, tq), BF16), head_vec, head_vec, head_vec],
        compiler_params=_cparams(("parallel", "parallel", "parallel")),
        name="attn",
    )(k_norm_bound, q, k, vt)


def _mix_kernel(yf_ref, at_ref, gf_ref, ga_ref, x_ref, g1_ref, sc2_ref, sh2_ref,
                wf_ref, wo_ref, wout_ref, l1g_ref, l1b_ref, wr_ref, br_ref, tri_ref,
                x1_ref, u2_ref, idx_ref, wgt_ref, rank_ref, cnt_ref, cnt_acc):
    first = jnp.logical_and(pl.program_id(0) == 0, pl.program_id(1) == 0)

    @pl.when(first)
    def _():
        cnt_acc[...] = jnp.zeros_like(cnt_acc)

    tm_all = x_ref.shape[0]
    rg = tm_all // MIX_ROW_GROUPS
    u_his, u_los = [], []
    groups = [slice(i * rg, (i + 1) * rg) for i in range(MIX_ROW_GROUPS)]
    ms = []
    for rows in groups:
        yf = jnp.dot(yf_ref[rows, :], wf_ref[...], preferred_element_type=F32)
        ya = jnp.dot(at_ref[rows, :], wo_ref[...], preferred_element_type=F32)
        ms.append((gf_ref[rows, :].astype(F32) * yf + ga_ref[rows, :].astype(F32) * ya).astype(BF16))
    hs = [jnp.dot(m, wout_ref[...], preferred_element_type=F32) for m in ms]
    for rows, h in zip(groups, hs):
        r = DEEPNORM_ALPHA * x_ref[rows, :] + g1_ref[...] * h
        x1 = _layer_norm(r, LN_EPS) * l1g_ref[...] + l1b_ref[...]
        x1_ref[rows, :] = x1
        u2 = _layer_norm(x1, ADA_EPS) * (1.0 + sc2_ref[...]) + sh2_ref[...]
        u2_ref[rows, :] = _pack_bf16_pairs(u2)
        u_hi = u2.astype(BF16)
        u_his.append(u_hi)
        u_los.append((u2 - u_hi.astype(F32)).astype(BF16))

    nt = (((1,), (1,)), ((), ()))
    u_hi = jnp.concatenate(u_his, axis=0)
    u_lo = jnp.concatenate(u_los, axis=0)
    w = wr_ref[...]
    w_hi = w.astype(BF16)
    w_lo = (w - w_hi.astype(F32)).astype(BF16)
    logits = (lax.dot_general(w_hi, u_hi, nt, preferred_element_type=F32)
              + lax.dot_general(w_lo, u_hi, nt, preferred_element_type=F32)
              + lax.dot_general(w_hi, u_lo, nt, preferred_element_type=F32)) + br_ref[...]

    ne, tm = logits.shape
    eid = lax.broadcasted_iota(jnp.int32, (ne, tm), 0)
    work = logits
    vals, idxs, hots = [], [], []
    for _ in range(TOP_K):
        mx = jnp.max(work, axis=0, keepdims=True)
        ix = jnp.min(jnp.where(work == mx, eid, ne), axis=0, keepdims=True)
        hot = eid == ix
        work = jnp.where(hot, -jnp.inf, work)
        vals.append(mx)
        idxs.append(ix)
        hots.append(hot)
    ex = [jnp.exp(v - vals[0]) for v in vals]
    den = ex[0] + ex[1] + ex[2] + ex[3]
    wgt_ref[...] = jnp.concatenate([e / den for e in ex], axis=0)
    idx_ref[...] = jnp.concatenate(idxs, axis=0)

    sel = hots[0] | hots[1] | hots[2] | hots[3]
    mask = jnp.where(sel, 1.0, 0.0)
    prefix = jnp.dot(mask.astype(BF16), tri_ref[...], preferred_element_type=F32)
    pos = prefix + cnt_acc[...]
    ranks = [jnp.sum(jnp.where(hot, pos, 0.0), axis=0, keepdims=True) for hot in hots]
    rank_ref[...] = jnp.concatenate(ranks, axis=0).astype(jnp.int32)
    cnt_acc[...] += jnp.sum(mask, axis=1, keepdims=True)
    cnt_ref[...] = cnt_acc[...].astype(jnp.int32)


def _mix(yfm, attn, gf, ga, x, g1, sc2, sh2, wf, wo, wout, l1g, l1b, wr_t, br, tm, b0, bsz):
    _, s, d = x.shape
    n = bsz * s
    nt = s // tm
    tri = (jnp.arange(tm)[:, None] < jnp.arange(tm)[None, :]).astype(BF16)
    row_in = lambda w: pl.BlockSpec((None, tm, w), lambda b, i: (b + b0, i, 0))
    row = lambda w: pl.BlockSpec((None, tm, w), lambda b, i: (b, i, 0))
    mod = pl.BlockSpec((None, 1, d), lambda b, i: (b + b0, 0, 0))
    full = lambda shp: pl.BlockSpec(shp, lambda b, i: (0,) * len(shp))
    tok = pl.BlockSpec((TOP_K, tm), lambda b, i: (0, b * nt + i))
    return pl.pallas_call(
        _mix_kernel,
        grid=(bsz, nt),
        in_specs=[row_in(FOURIER_DIM), row_in(Q_DIM), row_in(d), row_in(d), row_in(d), mod, mod, mod,
                  full(wf.shape), full(wo.shape), full(wout.shape), full((1, d)), full((1, d)),
                  full(wr_t.shape), full((N_EXPERTS, 1)), full((tm, tm))],
        out_specs=[row(d), row(d // 2), tok, tok, tok, full((N_EXPERTS, 1))],
        out_shape=[jax.ShapeDtypeStruct((bsz, s, d), F32),
                   jax.ShapeDtypeStruct((bsz, s, d // 2), jnp.int32),
                   jax.ShapeDtypeStruct((TOP_K, n), jnp.int32),
                   jax.ShapeDtypeStruct((TOP_K, n), F32),
                   jax.ShapeDtypeStruct((TOP_K, n), jnp.int32),
                   jax.ShapeDtypeStruct((N_EXPERTS, 1), jnp.int32)],
        scratch_shapes=[pltpu.VMEM((N_EXPERTS, 1), F32)],
        compiler_params=_cparams(("arbitrary", "arbitrary")),
        name="mix",
    )(yfm, attn, gf, ga, x, g1, sc2, sh2, wf, wo, wout, l1g, l1b, wr_t, br, tri)


def _expert_kernel(be_ref, nv_ref, new_ref, xs_ref, wgu_ref, bgu_ref, wd_ref, bd_ref, ys_ref,
                   wgu_bf, wd_bf):
    del be_ref
    i = pl.program_id(0)
    nvalid = nv_ref[i]

    @pl.when(new_ref[i] == 1)
    def _():
        wgu_bf[...] = wgu_ref[...].astype(BF16)
        wd_bf[...] = wd_ref[...].astype(BF16)

    @pl.when(nvalid > 0)
    def _():
        xw = xs_ref[...]
        row = lax.broadcasted_iota(jnp.int32, xw.shape, 0)
        lo, hi = _unpack_bf16_pairs(jnp.where(row < nvalid, xw, 0))
        xb = jnp.concatenate([lo.astype(BF16), hi.astype(BF16)], axis=1)
        h = jnp.dot(xb, wgu_bf[...], preferred_element_type=F32) + bgu_ref[...]
        gate = jnp.minimum(h[:, :D_EXPERT], SWIGLU_LIMIT)
        up = jnp.clip(h[:, D_EXPERT:], -SWIGLU_LIMIT, SWIGLU_LIMIT)
        act = (up + 1.0) * gate * jax.nn.sigmoid(SWIGLU_ALPHA * gate)
        y = jnp.dot(act.astype(BF16), wd_bf[...], preferred_element_type=F32) + bd_ref[...]
        ys_ref[...] = _pack_bf16_pairs(y)

    @pl.when(nvalid == 0)
    def _():
        ys_ref[...] = jnp.zeros_like(ys_ref)


def _experts(block_e, block_nvalid, block_new, xs, wgu, bgu, wd, bd, tmb):
    n_pad, dh = xs.shape
    d = 2 * dh
    grid_spec = pltpu.PrefetchScalarGridSpec(
        num_scalar_prefetch=3,
        grid=(n_pad // tmb,),
        in_specs=[pl.BlockSpec((tmb, dh), lambda i, be, nv, nw: (i, 0)),
                  pl.BlockSpec((None, d, 2 * D_EXPERT), lambda i, be, nv, nw: (be[i], 0, 0)),
                  pl.BlockSpec((None, 1, 2 * D_EXPERT), lambda i, be, nv, nw: (be[i], 0, 0)),
                  pl.BlockSpec((None, D_EXPERT, d), lambda i, be, nv, nw: (be[i], 0, 0)),
                  pl.BlockSpec((None, 1, d), lambda i, be, nv, nw: (be[i], 0, 0))],
        out_specs=pl.BlockSpec((tmb, dh), lambda i, be, nv, nw: (i, 0)),
        scratch_shapes=[pltpu.VMEM((d, 2 * D_EXPERT), BF16), pltpu.VMEM((D_EXPERT, d), BF16)],
    )
    return pl.pallas_call(
        _expert_kernel,
        grid_spec=grid_spec,
        out_shape=jax.ShapeDtypeStruct((n_pad, dh), jnp.int32),
        compiler_params=_cparams(("arbitrary",)),
        name="experts",
    )(block_e, block_nvalid, block_new, xs, wgu, bgu, wd, bd)


def _final_kernel(yg_ref, w_ref, x1_ref, g2_ref, lg_ref, lb_ref, o_ref):
    w = w_ref[...]
    acc_lo = None
    for j in range(TOP_K):
        lo, hi = _unpack_bf16_pairs(yg_ref[j])
        wj = w[:, j:j + 1]
        acc_lo = lo * wj if acc_lo is None else acc_lo + lo * wj
        acc_hi = hi * wj if j == 0 else acc_hi + hi * wj
    h = jnp.concatenate([acc_lo, acc_hi], axis=1)
    r = DEEPNORM_ALPHA * x1_ref[...] + g2_ref[...] * h
    o_ref[...] = _layer_norm(r, LN_EPS) * lg_ref[...] + lb_ref[...]


def _final(yg, w_tok, x1, g2, lg, lb, tm, b0, total_bsz, out_so_far):
    bsz, s, d = x1.shape
    nt = s // tm
    row = pl.BlockSpec((None, tm, d), lambda b, i: (b, i, 0))
    full = pl.BlockSpec((1, d), lambda b, i: (0, 0))
    in_specs = [pl.BlockSpec((TOP_K, tm, d // 2), lambda b, i: (0, b * nt + i, 0)),
                pl.BlockSpec((tm, TOP_K), lambda b, i: (b * nt + i, 0)),
                row, pl.BlockSpec((None, 1, d), lambda b, i: (b + b0, 0, 0)), full, full]
    args = [yg, w_tok, x1, g2, lg, lb]
    aliases = {}
    if out_so_far is not None:
        in_specs.append(pl.BlockSpec(memory_space=pl.ANY))
        args.append(out_so_far)
        aliases = {len(args) - 1: 0}
    body = _final_kernel if out_so_far is None else (lambda *refs: _final_kernel(*refs[:6], refs[7]))
    return pl.pallas_call(
        body,
        grid=(bsz, nt),
        in_specs=in_specs,
        out_specs=pl.BlockSpec((None, tm, d), lambda b, i: (b + b0, i, 0)),
        out_shape=jax.ShapeDtypeStruct((total_bsz, s, d), F32),
        input_output_aliases=aliases,
        compiler_params=_cparams(("parallel", "parallel")),
        name="final",
    )(*args)


def _scatter_rows(rows, dest, m_out):
    n, w = rows.shape
    nj = dest.shape[0]
    ch = GATHER_CHUNK
    nw = SC_CORES * SC_SUBCORES
    nch = n // (nw * ch)
    assert n == nw * nch * ch and nch % 2 == 0
    per_w = nch * ch
    mesh = plsc.VectorSubcoreMesh(core_axis_name="c", subcore_axis_name="s",
                                  num_cores=SC_CORES, num_subcores=SC_SUBCORES)

    @functools.partial(
        pl.kernel, mesh=mesh,
        out_type=jax.ShapeDtypeStruct((m_out, w), rows.dtype),
        scratch_types=[pltpu.VMEM((nj, nch, ch), jnp.int32), pltpu.VMEM((2, ch, w), rows.dtype),
                       pltpu.SemaphoreType.DMA((2,)), pltpu.SemaphoreType.DMA((2,))],
    )
    def scatter_kernel(rows_hbm, dest_hbm, out_hbm, idx_v, rows_v, rsem, ssem):
        wid = lax.axis_index("s") * SC_CORES + lax.axis_index("c")
        base = wid * per_w
        for j in range(nj):
            pltpu.sync_copy(dest_hbm.at[j, wid], idx_v.at[j])

        def read(i, b):
            off = pl.multiple_of(base + i * ch, ch)
            return pltpu.make_async_copy(rows_hbm.at[pl.ds(off, ch)], rows_v.at[b], rsem.at[b])

        def scatter(i, b, j):
            return pltpu.make_async_copy(rows_v.at[b], out_hbm.at[idx_v.at[j, i]], ssem.at[b])

        read(0, 0).start()

        @pl.loop(0, nch, step=2)
        def _(i0):
            for b in range(2):
                i = i0 + b
                read(i, b).wait()

                @pl.when(i + 1 < nch)
                def _():
                    @pl.when(i >= 1)
                    def _():
                        for j in range(nj):
                            scatter(i - 1, 1 - b, j).wait()
                    read(i + 1, 1 - b).start()

                for j in range(nj):
                    scatter(i, b, j).start()

        for b in range(2):
            for j in range(nj):
                scatter(nch - 2 + b, b, j).wait()

    return scatter_kernel(rows, dest.reshape(nj, nw, nch, ch))


def _gather_rows(table, idx):
    m = idx.shape[0]
    w = table.shape[1]
    ch = GATHER_CHUNK
    nw = SC_CORES * SC_SUBCORES
    nch = m // (nw * ch)
    assert m == nw * nch * ch and nch % 2 == 0
    per_w = nch * ch
    mesh = plsc.VectorSubcoreMesh(core_axis_name="c", subcore_axis_name="s",
                                  num_cores=SC_CORES, num_subcores=SC_SUBCORES)

    @functools.partial(
        pl.kernel, mesh=mesh,
        out_type=jax.ShapeDtypeStruct((m, w), table.dtype),
        scratch_types=[pltpu.VMEM((nch, ch), jnp.int32), pltpu.VMEM((2, ch, w), table.dtype),
                       pltpu.SemaphoreType.DMA((2,)), pltpu.SemaphoreType.DMA((2,))],
    )
    def gather_kernel(table_hbm, idx_hbm, out_hbm, idx_v, rows_v, gsem, wsem):
        wid = lax.axis_index("s") * SC_CORES + lax.axis_index("c")
        base = wid * per_w
        pltpu.sync_copy(idx_hbm.at[wid], idx_v)

        def gather(j, b):
            return pltpu.make_async_copy(table_hbm.at[idx_v.at[j]], rows_v.at[b], gsem.at[b])

        def write(j, b):
            off = pl.multiple_of(base + j * ch, ch)
            return pltpu.make_async_copy(rows_v.at[b], out_hbm.at[pl.ds(off, ch)], wsem.at[b])

        gather(0, 0).start()

        @pl.loop(0, nch, step=2)
        def _(i):
            for b in range(2):
                j = i + b
                gather(j, b).wait()

                @pl.when(j + 1 < nch)
                def _():
                    @pl.when(j >= 1)
                    def _():
                        write(j - 1, 1 - b).wait()
                    gather(j + 1, 1 - b).start()

                write(j, b).start()

        write(nch - 2, 0).wait()
        write(nch - 1, 1).wait()

    return gather_kernel(table, idx.reshape(nw, nch, ch))


def kernel(x, c, w_ada, b_ada, w_in, q_norm_g, k_norm_g, w_fourier, w_attn_o, w_out, ln1_g, ln1_b,
           w_router, b_router, w_gate_up, b_gate_up, w_down, b_down, ln2_g, ln2_b):
    bsz, s, d = x.shape
    n = bsz * s
    tm = min(ROW_TILE, s)
    l = 0

    mod = _ada(c, w_ada[l], b_ada[l])
    sh1, sc1, g1, sh2, sc2, g2 = [m[:, None, :] for m in jnp.split(mod, N_MOD, axis=-1)]

    dft_c, m1, m2, cw, sw, p, q = _dft_tables(s)
    cos, sin_a, sin_b = _rope_tables(s)

    fa, fb, qh, kh, vt, gf, ga = _inproj(
        x, sc1, sh1, w_in[l].astype(BF16), dft_c,
        q_norm_g[l].reshape(1, HEAD_DIM), k_norm_g[l].reshape(1, HEAD_DIM), cos, sin_a, sin_b, tm)

    zr, zi = _fft1(fa, fb, m1, cw, sw, p, q)
    yfm = _fft2(zr, zi, m2)

    k_bound = (math.sqrt(HEAD_DIM) * (1.0 + 2.0 ** -7)) * jnp.max(jnp.abs(k_norm_g[l])).reshape(1, 1)
    attn = _attention(qh, kh, vt, k_bound, min(ATTN_TQ, s))

    wf_b, wo_b, wout_b = w_fourier[l].astype(BF16), w_attn_o[l].astype(BF16), w_out[l].astype(BF16)
    bgu, bd = b_gate_up[l][:, None, :], b_down[l][:, None, :]
    tmb = MOE_BLOCK
    eids = jnp.arange(N_EXPERTS, dtype=jnp.int32)

    ngroups = MOE_BATCH_GROUPS if bsz % MOE_BATCH_GROUPS == 0 else 1
    gb = bsz // ngroups
    ng = gb * s
    out = None
    for grp in range(ngroups):
        b0 = grp * gb
        x1, u2p, idx_t, wgt_t, rank_t, counts = _mix(
            yfm, attn, gf, ga, x, g1, sc2, sh2, wf_b, wo_b, wout_b,
            ln1_g[l].reshape(1, d), ln1_b[l].reshape(1, d),
            w_router[l].T, b_router[l].reshape(N_EXPERTS, 1), tm, b0, gb)

        counts = counts.reshape(N_EXPERTS)
        pcounts = ((counts + tmb - 1) // tmb) * tmb
        pends = jnp.cumsum(pcounts)
        pstarts = pends - pcounts
        n_pad = ng * TOP_K + N_EXPERTS * tmb
        nblk = n_pad // tmb
        start_of = jnp.sum(jnp.where(idx_t[None] == eids[:, None, None], pstarts[:, None, None], 0), axis=0)
        dest_t = start_of + rank_t
        block_starts = jnp.arange(nblk, dtype=jnp.int32) * tmb
        block_e = jnp.sum((block_starts[:, None] >= pends[None, :]).astype(jnp.int32), axis=1)
        block_e = jnp.minimum(block_e, N_EXPERTS - 1)
        valid_end = jnp.sum(jnp.where(block_e[:, None] == eids[None, :], (pstarts + counts)[None, :], 0), axis=1)
        block_nvalid = jnp.clip(valid_end - block_starts, 0, tmb)
        block_new = jnp.concatenate([jnp.ones((1,), jnp.int32),
                                     (block_e[1:] != block_e[:-1]).astype(jnp.int32)])

        xs = _scatter_rows(u2p.reshape(ng, d // 2), dest_t, n_pad)
        ys = _experts(block_e, block_nvalid, block_new, xs, w_gate_up[l], bgu, w_down[l], bd, tmb)
        yg = _gather_rows(ys, dest_t.reshape(-1)).reshape(TOP_K, ng, d // 2)
        out = _final(yg, wgt_t.T, x1, g2, ln2_g[l].reshape(1, d), ln2_b[l].reshape(1, d), tm,
                     b0, bsz, out)
    return out
```

```python
import functools
import math

import jax
import jax.numpy as jnp
import numpy as np
from jax import lax
from jax.experimental import pallas as pl
from jax.experimental.pallas import tpu as pltpu
from jax.experimental.pallas import tpu_sc as plsc

F32 = jnp.float32
BF16 = jnp.bfloat16

D_MODEL = 1024
GRID_W = 64
N_Q_HEADS = 8
N_KV_HEADS = 2
HEAD_DIM = 128
Q_GROUP = N_Q_HEADS // N_KV_HEADS
ROPE_THETA = 10000.0
ROPE_AXIS_DIM = HEAD_DIM // 2
N_FOURIER_GROUPS = 4
FOURIER_GROUP_DIM = 128
FOURIER_DIM = N_FOURIER_GROUPS * FOURIER_GROUP_DIM
Q_DIM = N_Q_HEADS * HEAD_DIM
KV_DIM = N_KV_HEADS * HEAD_DIM
N_EXPERTS = 32
TOP_K = 4
D_EXPERT = 1024
SWIGLU_LIMIT = 7.0
SWIGLU_ALPHA = 1.702
N_MOD = 6
DEPTH = 1
DEEPNORM_ALPHA = (2 * DEPTH) ** 0.25
LN_EPS = 1e-5
ADA_EPS = 1e-6
QK_EPS = 1e-6
LOG2E = 1.4426950408889634

V7X_VMEM_BYTES = 64 * 1024 * 1024
VMEM_LIMIT = V7X_VMEM_BYTES - 8 * 1024 * 1024
LANES = 128
BF16_SUBLANES = 16

FFT_Q = 128
ROW_TILE = 512
ATTN_TQ = 1024
ATTN_FAST_KCHUNKS = 2
DENOMINATOR_GUARD = 2.0 ** -80
INPROJ_ROW_GROUPS = 2
MIX_ROW_GROUPS = 2
MOE_BLOCK = 512
MOE_BATCH_GROUPS = 2

SC_CORES = 2
SC_SUBCORES = 16
GATHER_CHUNK = 64


def _cparams(sem):
    return pltpu.CompilerParams(dimension_semantics=sem, vmem_limit_bytes=VMEM_LIMIT)


def _layer_norm(x, eps):
    mu = jnp.mean(x, axis=-1, keepdims=True)
    xc = x - mu
    var = jnp.mean(xc * xc, axis=-1, keepdims=True)
    return xc * lax.rsqrt(var + eps)


def _pack_bf16_pairs(x):
    n = x.shape[1] // 2
    bits = pltpu.bitcast(x.astype(BF16).astype(F32), jnp.uint32)
    word = (bits[:, :n] >> 16) | (bits[:, n:] & jnp.uint32(0xFFFF0000))
    return pltpu.bitcast(word, jnp.int32)


def _unpack_bf16_pairs(w):
    bits = pltpu.bitcast(w, jnp.uint32)
    lo = pltpu.bitcast(bits << 16, F32)
    hi = pltpu.bitcast(bits & jnp.uint32(0xFFFF0000), F32)
    return lo, hi


def _ada_kernel(c_ref, w_ref, b_ref, o_ref):
    c = c_ref[...]
    cond = c * jax.nn.sigmoid(c)
    o_ref[...] = jnp.dot(cond, w_ref[...], preferred_element_type=F32,
                         precision=lax.Precision.HIGHEST) + b_ref[...]


def _ada(c, w_ada, b_ada):
    bsz, d = c.shape
    n = w_ada.shape[1]
    tn = 1536
    return pl.pallas_call(
        _ada_kernel,
        grid=(n // tn,),
        in_specs=[pl.BlockSpec((bsz, d), lambda j: (0, 0)),
                  pl.BlockSpec((d, tn), lambda j: (0, j)),
                  pl.BlockSpec((1, tn), lambda j: (0, j))],
        out_specs=pl.BlockSpec((bsz, tn), lambda j: (0, j)),
        out_shape=jax.ShapeDtypeStruct((bsz, n), F32),
        compiler_params=_cparams(("arbitrary",)),
        name="ada",
    )(c, w_ada, b_ada.reshape(1, n))


def _rope(xh, cos, sin_a, sin_b):
    a = pltpu.roll(xh, 32, 1)
    b = pltpu.roll(xh, 96, 1)
    return xh * cos + b * sin_b + a * sin_a


def _inproj_kernel(x_ref, sc_ref, sh_ref, w_ref, dft_ref, qg_ref, kg_ref,
                   cos_ref, sa_ref, sb_ref,
                   fa_ref, fb_ref, q_ref, k_ref, vt_ref, gf_ref, ga_ref):
    tm = x_ref.shape[0]
    rg = tm // INPROJ_ROW_GROUPS
    q_scale = LOG2E / math.sqrt(HEAD_DIM)
    for i in range(INPROJ_ROW_GROUPS):
        rows = slice(i * rg, (i + 1) * rg)
        u = _layer_norm(x_ref[rows, :], ADA_EPS) * (1.0 + sc_ref[...]) + sh_ref[...]
        ub = u.astype(BF16)
        cos, sa, sb = cos_ref[rows, :], sa_ref[rows, :], sb_ref[rows, :]

        def norm_rope(z, gain, scale):
            ms = jnp.mean(z * z, axis=-1, keepdims=True)
            zn = z * lax.rsqrt(ms + QK_EPS) * gain
            return (_rope(zn, cos, sa, sb) * scale).astype(BF16)

        c0 = 0
        zf = jnp.dot(ub, w_ref[:, c0:c0 + FOURIER_DIM], preferred_element_type=F32).astype(BF16)
        for g in range(N_FOURIER_GROUPS):
            sl = slice(g * FOURIER_GROUP_DIM, (g + 1) * FOURIER_GROUP_DIM)
            ab = jnp.dot(zf[:, sl], dft_ref[...], preferred_element_type=F32)
            fa_ref[rows, sl] = ab[:, :FOURIER_GROUP_DIM].astype(BF16)
            fb_ref[rows, sl] = ab[:, FOURIER_GROUP_DIM:].astype(BF16)
        c0 += FOURIER_DIM

        zq = jnp.dot(ub, w_ref[:, c0:c0 + Q_DIM], preferred_element_type=F32)
        for h in range(N_Q_HEADS):
            sl = slice(h * HEAD_DIM, (h + 1) * HEAD_DIM)
            q_ref[rows, sl] = norm_rope(zq[:, sl], qg_ref[...], q_scale)
        c0 += Q_DIM

        zk = jnp.dot(ub, w_ref[:, c0:c0 + KV_DIM], preferred_element_type=F32)
        for h in range(N_KV_HEADS):
            sl = slice(h * HEAD_DIM, (h + 1) * HEAD_DIM)
            k_ref[rows, sl] = norm_rope(zk[:, sl], kg_ref[...], 1.0)
        c0 += KV_DIM

        zv = jnp.dot(ub, w_ref[:, c0:c0 + KV_DIM], preferred_element_type=F32)
        for h in range(N_KV_HEADS):
            vt_ref[h, :, rows] = zv[:, h * HEAD_DIM:(h + 1) * HEAD_DIM].T.astype(BF16)
        c0 += KV_DIM

        zg = jnp.dot(ub, w_ref[:, c0:c0 + D_MODEL], preferred_element_type=F32)
        gf_ref[rows, :] = jax.nn.sigmoid(zg).astype(BF16)
        c0 += D_MODEL
        zg = jnp.dot(ub, w_ref[:, c0:c0 + D_MODEL], preferred_element_type=F32)
        ga_ref[rows, :] = jax.nn.sigmoid(zg).astype(BF16)


def _inproj(x, sc1, sh1, w_in, dft_c, qg, kg, cos, sin_a, sin_b, tm):
    bsz, s, d = x.shape
    ncol = w_in.shape[1]
    row = lambda w: pl.BlockSpec((None, tm, w), lambda b, i: (b, i, 0))
    mod = pl.BlockSpec((None, 1, d), lambda b, i: (b, 0, 0))
    full = lambda shp: pl.BlockSpec(shp, lambda b, i: (0,) * len(shp))
    tab = pl.BlockSpec((tm, HEAD_DIM), lambda b, i: (i, 0))
    vt_rows = HEAD_DIM
    vt_spec = pl.BlockSpec((None, N_KV_HEADS, None, vt_rows, tm), lambda b, i: (b, 0, i, 0, 0))
    vt_shape = jax.ShapeDtypeStruct((bsz, N_KV_HEADS, s // tm, vt_rows, tm), BF16)
    rows_out = lambda w: (row(w), jax.ShapeDtypeStruct((bsz, s, w), BF16))
    outs = [rows_out(FOURIER_DIM), rows_out(FOURIER_DIM), rows_out(Q_DIM), rows_out(KV_DIM),
            (vt_spec, vt_shape), rows_out(D_MODEL), rows_out(D_MODEL)]
    return pl.pallas_call(
        _inproj_kernel,
        grid=(bsz, s // tm),
        in_specs=[row(d), mod, mod, full((d, ncol)), full(dft_c.shape),
                  full((1, HEAD_DIM)), full((1, HEAD_DIM)), tab, tab, tab],
        out_specs=[o[0] for o in outs],
        out_shape=[o[1] for o in outs],
        compiler_params=_cparams(("parallel", "parallel")),
        name="inproj",
    )(x, sc1, sh1, w_in, dft_c, qg, kg, cos, sin_a, sin_b)


def _fft1_kernel(a_ref, b_ref, m_ref, cw_ref, sw_ref, zr_ref, zi_ref, a32, b32):
    p, nb, _ = a_ref.shape
    a32[...] = a_ref[...].astype(F32)
    b32[...] = b_ref[...].astype(F32)
    for j in range(nb):
        ab = jnp.concatenate([a32[:, j, :], b32[:, j, :]], axis=0).astype(BF16)
        t = jnp.dot(m_ref[...], ab, preferred_element_type=F32)
        tr, ti = t[:p], t[p:]
        cw, sw = cw_ref[j], sw_ref[j]
        zr_ref[j] = (tr * cw - ti * sw).astype(BF16)
        zi_ref[j] = (ti * cw + tr * sw).astype(BF16)


def _fft1(fa, fb, m1, cw, sw, p, q):
    bsz, s, c = fa.shape
    nb = BF16_SUBLANES
    inp = pl.BlockSpec((None, p, nb, c), lambda b, j: (b, 0, j, 0))
    tw = pl.BlockSpec((nb, p, 1), lambda b, j: (j, 0, 0))
    out = pl.BlockSpec((None, nb, p, c), lambda b, j: (b, j, 0, 0))
    return pl.pallas_call(
        _fft1_kernel,
        grid=(bsz, q // nb),
        in_specs=[inp, inp, pl.BlockSpec(m1.shape, lambda b, j: (0, 0)), tw, tw],
        out_specs=[out, out],
        out_shape=[jax.ShapeDtypeStruct((bsz, q, p, c), BF16)] * 2,
        scratch_shapes=[pltpu.VMEM((p, nb, c), F32)] * 2,
        compiler_params=_cparams(("parallel", "parallel")),
        name="fft1",
    )(fa.reshape(bsz, p, q, c), fb.reshape(bsz, p, q, c), m1, cw, sw)


def _fft2_kernel(zr_ref, zi_ref, m_ref, y_ref, zr32, zi32, y32):
    zr32[...] = zr_ref[...].astype(F32)
    zi32[...] = zi_ref[...].astype(F32)
    for j in range(zr_ref.shape[1]):
        z = jnp.concatenate([zr32[:, j, :], zi32[:, j, :]], axis=0).astype(BF16)
        y32[:, j, :] = jnp.dot(m_ref[...], z, preferred_element_type=F32)
    y_ref[...] = y32[...].astype(BF16)


def _fft2(zr, zi, m2):
    bsz, q, p, c = zr.shape
    pc = BF16_SUBLANES
    blk = pl.BlockSpec((None, q, pc, c), lambda b, j: (b, 0, j, 0))
    y = pl.pallas_call(
        _fft2_kernel,
        grid=(bsz, p // pc),
        in_specs=[blk, blk, pl.BlockSpec(m2.shape, lambda b, j: (0, 0))],
        out_specs=blk,
        out_shape=jax.ShapeDtypeStruct((bsz, q, p, c), BF16),
        scratch_shapes=[pltpu.VMEM((q, pc, c), F32)] * 3,
        compiler_params=_cparams(("parallel", "parallel")),
        name="fft2",
    )(zr, zi, m2)
    return y.reshape(bsz, q * p, c)


def _dft_tables(s):
    q = FFT_Q
    p = s // q

    def cs(n_rows, n_cols, n):
        r = np.arange(n_rows, dtype=np.int64)[:, None]
        c = np.arange(n_cols, dtype=np.int64)[None, :]
        ang = ((r * c) % n).astype(np.float64) * (2.0 * math.pi / n)
        return np.cos(ang).astype(np.float32), np.sin(ang).astype(np.float32)

    cc, sc = cs(FOURIER_GROUP_DIM, FOURIER_GROUP_DIM, FOURIER_GROUP_DIM)
    dft_c = jnp.asarray(np.concatenate([cc, sc], axis=1), dtype=BF16)
    cp, sp = cs(p, p, p)
    m1 = jnp.asarray(np.block([[cp, -sp], [sp, cp]]), dtype=BF16)
    cq, sq = cs(q, q, q)
    scale = np.float32(1.0 / math.sqrt(s * FOURIER_GROUP_DIM))
    m2 = jnp.asarray(np.concatenate([cq, -sq], axis=1) * scale, dtype=BF16)
    cw, sw = cs(q, p, s)
    return dft_c, m1, m2, jnp.asarray(cw[:, :, None]), jnp.asarray(sw[:, :, None]), p, q


def _rope_tables(s):
    f32 = np.float32
    rows = s // GRID_W
    row_ids = np.repeat(np.arange(rows, dtype=f32), GRID_W)
    col_ids = np.tile(np.arange(GRID_W, dtype=f32), rows)
    freqs = (f32(ROPE_THETA) ** (-np.arange(0, ROPE_AXIS_DIM, 2, dtype=f32) / f32(ROPE_AXIS_DIM))).astype(f32)
    ang_r = (row_ids[:, None] * freqs).astype(f32)
    ang_c = (col_ids[:, None] * freqs).astype(f32)
    cr, sr, cc, sc = np.cos(ang_r), np.sin(ang_r), np.cos(ang_c), np.sin(ang_c)
    z = np.zeros_like(sr)
    cos = np.concatenate([cr, cr, cc, cc], axis=1)
    sin_b = np.concatenate([-sr, z, -sc, z], axis=1)
    sin_a = np.concatenate([z, sr, z, sc], axis=1)
    return jnp.asarray(cos, F32), jnp.asarray(sin_a, F32), jnp.asarray(sin_b, F32)


def _attn_kernel(kb_ref, q_ref, k_ref, vt_ref, o_ref, m_ref, acc_ref, al0, al1, s0, s1, p0, p1,
                 qt_ref, mx0, mx1, qa_ref, l_ref, ps0, ps1):
    al_ref, s_ref, p_ref, mx_ref, ps_ref = (al0, al1), (s0, s1), (p0, p1), (mx0, mx1), (ps0, ps1)
    nchunk = vt_ref.shape[0]
    tk = vt_ref.shape[2]
    tq = q_ref.shape[0]
    heads = range(Q_GROUP)

    def k_chunk(c):
        return k_ref[pl.ds(pl.multiple_of(c * tk, tk), tk), :]

    def write_out():
        for g in heads:
            out_t = acc_ref[g] / l_ref[g]
            o_ref[:, g * HEAD_DIM:(g + 1) * HEAD_DIM] = out_t.T.astype(BF16)

    row0 = lax.broadcasted_iota(jnp.int32, (HEAD_DIM, tq), 0) == 0
    for g in heads:
        qt = q_ref[:, g * HEAD_DIM:(g + 1) * HEAD_DIM].astype(F32).T
        bound = jnp.sqrt(jnp.sum(qt * qt, axis=0, keepdims=True)) * kb_ref[...]
        qa_ref[g, :HEAD_DIM, :] = qt.astype(BF16)
        qa_ref[g, HEAD_DIM:, :] = jnp.where(row0, -bound, 0.0).astype(BF16)
    kf = ATTN_FAST_KCHUNKS
    tkf = kf * tk
    nfast = nchunk // kf
    ones_col = (lax.broadcasted_iota(jnp.int32, (tkf, HEAD_DIM), 1) == 0).astype(BF16)

    def probs_into(c, slot):
        kc = k_ref[pl.ds(pl.multiple_of(c * tkf, tkf), tkf), :]
        ka = jnp.concatenate([kc, ones_col], axis=1)
        for g in heads:
            p32 = jnp.exp2(jnp.dot(ka, qa_ref[g], preferred_element_type=F32))
            l_ref[g] += jnp.sum(p32, axis=0, keepdims=True)
            p_ref[slot][g] = p32.astype(BF16)

    def pv_add(c, slot):
        vc = jnp.concatenate([vt_ref[kf * c + j] for j in range(kf)], axis=1)
        for g in heads:
            acc_ref[g] += jnp.dot(vc, p_ref[slot][g], preferred_element_type=F32)

    acc_ref[...] = jnp.zeros(acc_ref.shape, F32)
    l_ref[...] = jnp.zeros(l_ref.shape, F32)
    probs_into(0, 0)

    def fast_pair(i, carry):
        probs_into(2 * i + 1, 1)
        pv_add(2 * i, 0)
        probs_into(2 * i + 2, 0)
        pv_add(2 * i + 1, 1)
        return carry

    lax.fori_loop(0, (nfast - 2) // 2, fast_pair, 0)
    probs_into(nfast - 1, 1)
    pv_add(nfast - 2, 0)
    pv_add(nfast - 1, 1)
    write_out()
    den_min = l_ref[0]
    for g in heads[1:]:
        den_min = jnp.minimum(den_min, l_ref[g])
    safe = jnp.min(den_min) > DENOMINATOR_GUARD

    def scores_into(c, slot):
        kc = k_chunk(c)
        for g in heads:
            sc = jnp.dot(kc, qt_ref[g], preferred_element_type=F32)
            s_ref[slot][g] = sc
            parts = [jnp.max(sc[r * (tk // 4):(r + 1) * (tk // 4)], axis=0, keepdims=True) for r in range(4)]
            mx_ref[slot][g] = jnp.maximum(jnp.maximum(parts[0], parts[1]), jnp.maximum(parts[2], parts[3]))

    def softmax(slot):
        for g in heads:
            m_old = m_ref[g]
            m_new = jnp.maximum(m_old, mx_ref[slot][g])
            al_ref[slot][g] = jnp.exp2(m_old - m_new)
            m_ref[g] = m_new
            p32 = jnp.exp2(s_ref[slot][g] - m_new)
            ps_ref[slot][g] = jnp.sum(p32, axis=0, keepdims=True)
            p_ref[slot][g, :tk, :] = p32.astype(BF16)

    def pv_update(c, slot):
        vc = vt_ref[c]
        for g in heads:
            acc_ref[g] = al_ref[slot][g] * acc_ref[g] + jnp.dot(vc, p_ref[slot][g, :tk, :],
                                                                preferred_element_type=F32)
            l_ref[g] = al_ref[slot][g] * l_ref[g] + ps_ref[slot][g]

    def step(c, slot):
        scores_into(c + 1, 1 - slot)
        pv_update(c - 1, 1 - slot)
        softmax(slot)

    @pl.when(jnp.logical_not(safe))
    def _():
        for g in heads:
            qt_ref[g] = qa_ref[g, :HEAD_DIM, :]
        scores_into(0, 0)
        m_ref[...] = jnp.full(m_ref.shape, -jnp.inf, F32)
        acc_ref[...] = jnp.zeros(acc_ref.shape, F32)
        l_ref[...] = jnp.zeros(l_ref.shape, F32)
        softmax(0)
        scores_into(1, 1)

        def pair(i, carry):
            step(2 * i + 1, 1)
            step(2 * i + 2, 0)
            return carry

        lax.fori_loop(0, (nchunk - 2) // 2, pair, 0)
        pv_update(nchunk - 2, 0)
        softmax(1)
        pv_update(nchunk - 1, 1)
        write_out()


def _attention(q, k, vt, k_norm_bound, tq):
    bsz, s, _ = q.shape
    rows, tk = vt.shape[3], vt.shape[4]
    nchunk = s // tk
    assert nchunk % (2 * ATTN_FAST_KCHUNKS) == 0 and nchunk >= 4 * ATTN_FAST_KCHUNKS
    gw = Q_GROUP * HEAD_DIM
    qspec = pl.BlockSpec((None, tq, gw), lambda b, h, i: (b, i, h))
    kspec = pl.BlockSpec((None, s, HEAD_DIM), lambda b, h, i: (b, 0, h))
    vspec = pl.BlockSpec((None, None, nchunk, rows, tk), lambda b, h, i: (b, h, 0, 0, 0))
    head_vec = pltpu.VMEM((Q_GROUP, 1, tq), F32)
    return pl.pallas_call(
        _attn_kernel,
        grid=(bsz, N_KV_HEADS, s // tq),
        in_specs=[pl.BlockSpec((1, 1), lambda b, h, i: (0, 0)), qspec, kspec, vspec],
        out_specs=qspec,
        out_shape=jax.ShapeDtypeStruct((bsz, s, Q_DIM), BF16),
        scratch_shapes=[head_vec, pltpu.VMEM((Q_GROUP, rows, tq), F32), head_vec, head_vec]
        + [pltpu.VMEM((Q_GROUP, tk, tq), F32)] * 2
        + [pltpu.VMEM((Q_GROUP, ATTN_FAST_KCHUNKS * tk, tq), BF16)] * 2
        + [pltpu.VMEM((Q_GROUP, HEAD_DIM, tq), BF16), head_vec, head_vec,
           pltpu.VMEM((Q_GROUP, 2 * HEAD_DIM, tq), BF16), head_vec, head_vec, head_vec],
        compiler_params=_cparams(("parallel", "parallel", "parallel")),
        name="attn",
    )(k_norm_bound, q, k, vt)


def _mix_kernel(yf_ref, at_ref, gf_ref, ga_ref, x_ref, g1_ref, sc2_ref, sh2_ref,
                wf_ref, wo_ref, wout_ref, l1g_ref, l1b_ref, wr_ref, br_ref, tri_ref,
                x1_ref, u2_ref, idx_ref, wgt_ref, rank_ref, cnt_ref, cnt_acc):
    first = jnp.logical_and(pl.program_id(0) == 0, pl.program_id(1) == 0)

    @pl.when(first)
    def _():
        cnt_acc[...] = jnp.zeros_like(cnt_acc)

    tm_all = x_ref.shape[0]
    rg = tm_all // MIX_ROW_GROUPS
    u_his, u_los = [], []
    groups = [slice(i * rg, (i + 1) * rg) for i in range(MIX_ROW_GROUPS)]
    ms = []
    for rows in groups:
        yf = jnp.dot(yf_ref[rows, :], wf_ref[...], preferred_element_type=F32)
        ya = jnp.dot(at_ref[rows, :], wo_ref[...], preferred_element_type=F32)
        ms.append((gf_ref[rows, :].astype(F32) * yf + ga_ref[rows, :].astype(F32) * ya).astype(BF16))
    hs = [jnp.dot(m, wout_ref[...], preferred_element_type=F32) for m in ms]
    for rows, h in zip(groups, hs):
        r = DEEPNORM_ALPHA * x_ref[rows, :] + g1_ref[...] * h
        x1 = _layer_norm(r, LN_EPS) * l1g_ref[...] + l1b_ref[...]
        x1_ref[rows, :] = x1
        u2 = _layer_norm(x1, ADA_EPS) * (1.0 + sc2_ref[...]) + sh2_ref[...]
        u2_ref[rows, :] = _pack_bf16_pairs(u2)
        u_hi = u2.astype(BF16)
        u_his.append(u_hi)
        u_los.append((u2 - u_hi.astype(F32)).astype(BF16))

    nt = (((1,), (1,)), ((), ()))
    u_hi = jnp.concatenate(u_his, axis=0)
    u_lo = jnp.concatenate(u_los, axis=0)
    w = wr_ref[...]
    w_hi = w.astype(BF16)
    w_lo = (w - w_hi.astype(F32)).astype(BF16)
    logits = (lax.dot_general(w_hi, u_hi, nt, preferred_element_type=F32)
              + lax.dot_general(w_lo, u_hi, nt, preferred_element_type=F32)
              + lax.dot_general(w_hi, u_lo, nt, preferred_element_type=F32)) + br_ref[...]

    ne, tm = logits.shape
    eid = lax.broadcasted_iota(jnp.int32, (ne, tm), 0)
    work = logits
    vals, idxs, hots = [], [], []
    for _ in range(TOP_K):
        mx = jnp.max(work, axis=0, keepdims=True)
        ix = jnp.min(jnp.where(work == mx, eid, ne), axis=0, keepdims=True)
        hot = eid == ix
        work = jnp.where(hot, -jnp.inf, work)
        vals.append(mx)
        idxs.append(ix)
        hots.append(hot)
    ex = [jnp.exp(v - vals[0]) for v in vals]
    den = ex[0] + ex[1] + ex[2] + ex[3]
    wgt_ref[...] = jnp.concatenate([e / den for e in ex], axis=0)
    idx_ref[...] = jnp.concatenate(idxs, axis=0)

    sel = hots[0] | hots[1] | hots[2] | hots[3]
    mask = jnp.where(sel, 1.0, 0.0)
    prefix = jnp.dot(mask.astype(BF16), tri_ref[...], preferred_element_type=F32)
    pos = prefix + cnt_acc[...]
    ranks = [jnp.sum(jnp.where(hot, pos, 0.0), axis=0, keepdims=True) for hot in hots]
    rank_ref[...] = jnp.concatenate(ranks, axis=0).astype(jnp.int32)
    cnt_acc[...] += jnp.sum(mask, axis=1, keepdims=True)
    cnt_ref[...] = cnt_acc[...].astype(jnp.int32)


def _mix(yfm, attn, gf, ga, x, g1, sc2, sh2, wf, wo, wout, l1g, l1b, wr_t, br, tm, b0, bsz):
    _, s, d = x.shape
    n = bsz * s
    nt = s // tm
    tri = (jnp.arange(tm)[:, None] < jnp.arange(tm)[None, :]).astype(BF16)
    row_in = lambda w: pl.BlockSpec((None, tm, w), lambda b, i: (b + b0, i, 0))
    row = lambda w: pl.BlockSpec((None, tm, w), lambda b, i: (b, i, 0))
    mod = pl.BlockSpec((None, 1, d), lambda b, i: (b + b0, 0, 0))
    full = lambda shp: pl.BlockSpec(shp, lambda b, i: (0,) * len(shp))
    tok = pl.BlockSpec((TOP_K, tm), lambda b, i: (0, b * nt + i))
    return pl.pallas_call(
        _mix_kernel,
        grid=(bsz, nt),
        in_specs=[row_in(FOURIER_DIM), row_in(Q_DIM), row_in(d), row_in(d), row_in(d), mod, mod, mod,
                  full(wf.shape), full(wo.shape), full(wout.shape), full((1, d)), full((1, d)),
                  full(wr_t.shape), full((N_EXPERTS, 1)), full((tm, tm))],
        out_specs=[row(d), row(d // 2), tok, tok, tok, full((N_EXPERTS, 1))],
        out_shape=[jax.ShapeDtypeStruct((bsz, s, d), F32),
                   jax.ShapeDtypeStruct((bsz, s, d // 2), jnp.int32),
                   jax.ShapeDtypeStruct((TOP_K, n), jnp.int32),
                   jax.ShapeDtypeStruct((TOP_K, n), F32),
                   jax.ShapeDtypeStruct((TOP_K, n), jnp.int32),
                   jax.ShapeDtypeStruct((N_EXPERTS, 1), jnp.int32)],
        scratch_shapes=[pltpu.VMEM((N_EXPERTS, 1), F32)],
        compiler_params=_cparams(("arbitrary", "arbitrary")),
        name="mix",
    )(yfm, attn, gf, ga, x, g1, sc2, sh2, wf, wo, wout, l1g, l1b, wr_t, br, tri)


def _expert_kernel(be_ref, nv_ref, new_ref, xs_ref, wgu_ref, bgu_ref, wd_ref, bd_ref, ys_ref,
                   wgu_bf, wd_bf):
    del be_ref
    i = pl.program_id(0)
    nvalid = nv_ref[i]

    @pl.when(new_ref[i] == 1)
    def _():
        wgu_bf[...] = wgu_ref[...].astype(BF16)
        wd_bf[...] = wd_ref[...].astype(BF16)

    @pl.when(nvalid > 0)
    def _():
        xw = xs_ref[...]
        row = lax.broadcasted_iota(jnp.int32, xw.shape, 0)
        lo, hi = _unpack_bf16_pairs(jnp.where(row < nvalid, xw, 0))
        xb = jnp.concatenate([lo.astype(BF16), hi.astype(BF16)], axis=1)
        h = jnp.dot(xb, wgu_bf[...], preferred_element_type=F32) + bgu_ref[...]
        gate = jnp.minimum(h[:, :D_EXPERT], SWIGLU_LIMIT)
        up = jnp.clip(h[:, D_EXPERT:], -SWIGLU_LIMIT, SWIGLU_LIMIT)
        act = (up + 1.0) * gate * jax.nn.sigmoid(SWIGLU_ALPHA * gate)
        y = jnp.dot(act.astype(BF16), wd_bf[...], preferred_element_type=F32) + bd_ref[...]
        ys_ref[...] = _pack_bf16_pairs(y)

    @pl.when(nvalid == 0)
    def _():
        ys_ref[...] = jnp.zeros_like(ys_ref)


def _experts(block_e, block_nvalid, block_new, xs, wgu, bgu, wd, bd, tmb):
    n_pad, dh = xs.shape
    d = 2 * dh
    grid_spec = pltpu.PrefetchScalarGridSpec(
        num_scalar_prefetch=3,
        grid=(n_pad // tmb,),
        in_specs=[pl.BlockSpec((tmb, dh), lambda i, be, nv, nw: (i, 0)),
                  pl.BlockSpec((None, d, 2 * D_EXPERT), lambda i, be, nv, nw: (be[i], 0, 0)),
                  pl.BlockSpec((None, 1, 2 * D_EXPERT), lambda i, be, nv, nw: (be[i], 0, 0)),
                  pl.BlockSpec((None, D_EXPERT, d), lambda i, be, nv, nw: (be[i], 0, 0)),
                  pl.BlockSpec((None, 1, d), lambda i, be, nv, nw: (be[i], 0, 0))],
        out_specs=pl.BlockSpec((tmb, dh), lambda i, be, nv, nw: (i, 0)),
        scratch_shapes=[pltpu.VMEM((d, 2 * D_EXPERT), BF16), pltpu.VMEM((D_EXPERT, d), BF16)],
    )
    return pl.pallas_call(
        _expert_kernel,
        grid_spec=grid_spec,
        out_shape=jax.ShapeDtypeStruct((n_pad, dh), jnp.int32),
        compiler_params=_cparams(("arbitrary",)),
        name="experts",
    )(block_e, block_nvalid, block_new, xs, wgu, bgu, wd, bd)


def _final_kernel(yg_ref, w_ref, x1_ref, g2_ref, lg_ref, lb_ref, o_ref):
    w = w_ref[...]
    acc_lo = None
    for j in range(TOP_K):
        lo, hi = _unpack_bf16_pairs(yg_ref[j])
        wj = w[:, j:j + 1]
        acc_lo = lo * wj if acc_lo is None else acc_lo + lo * wj
        acc_hi = hi * wj if j == 0 else acc_hi + hi * wj
    h = jnp.concatenate([acc_lo, acc_hi], axis=1)
    r = DEEPNORM_ALPHA * x1_ref[...] + g2_ref[...] * h
    o_ref[...] = _layer_norm(r, LN_EPS) * lg_ref[...] + lb_ref[...]


def _final(yg, w_tok, x1, g2, lg, lb, tm, b0, total_bsz, out_so_far):
    bsz, s, d = x1.shape
    nt = s // tm
    row = pl.BlockSpec((None, tm, d), lambda b, i: (b, i, 0))
    full = pl.BlockSpec((1, d), lambda b, i: (0, 0))
    in_specs = [pl.BlockSpec((TOP_K, tm, d // 2), lambda b, i: (0, b * nt + i, 0)),
                pl.BlockSpec((tm, TOP_K), lambda b, i: (b * nt + i, 0)),
                row, pl.BlockSpec((None, 1, d), lambda b, i: (b + b0, 0, 0)), full, full]
    args = [yg, w_tok, x1, g2, lg, lb]
    aliases = {}
    if out_so_far is not None:
        in_specs.append(pl.BlockSpec(memory_space=pl.ANY))
        args.append(out_so_far)
        aliases = {len(args) - 1: 0}
    body = _final_kernel if out_so_far is None else (lambda *refs: _final_kernel(*refs[:6], refs[7]))
    return pl.pallas_call(
        body,
        grid=(bsz, nt),
        in_specs=in_specs,
        out_specs=pl.BlockSpec((None, tm, d), lambda b, i: (b + b0, i, 0)),
        out_shape=jax.ShapeDtypeStruct((total_bsz, s, d), F32),
        input_output_aliases=aliases,
        compiler_params=_cparams(("parallel", "parallel")),
        name="final",
    )(*args)


def _scatter_rows(rows, dest, m_out):
    n, w = rows.shape
    nj = dest.shape[0]
    ch = GATHER_CHUNK
    nw = SC_CORES * SC_SUBCORES
    nch = n // (nw * ch)
    assert n == nw * nch * ch and nch % 2 == 0
    per_w = nch * ch
    mesh = plsc.VectorSubcoreMesh(core_axis_name="c", subcore_axis_name="s",
                                  num_cores=SC_CORES, num_subcores=SC_SUBCORES)

    @functools.partial(
        pl.kernel, mesh=mesh,
        out_type=jax.ShapeDtypeStruct((m_out, w), rows.dtype),
        scratch_types=[pltpu.VMEM((nj, nch, ch), jnp.int32), pltpu.VMEM((2, ch, w), rows.dtype),
                       pltpu.SemaphoreType.DMA((2,)), pltpu.SemaphoreType.DMA((2,))],
    )
    def scatter_kernel(rows_hbm, dest_hbm, out_hbm, idx_v, rows_v, rsem, ssem):
        wid = lax.axis_index("s") * SC_CORES + lax.axis_index("c")
        base = wid * per_w
        for j in range(nj):
            pltpu.sync_copy(dest_hbm.at[j, wid], idx_v.at[j])

        def read(i, b):
            off = pl.multiple_of(base + i * ch, ch)
            return pltpu.make_async_copy(rows_hbm.at[pl.ds(off, ch)], rows_v.at[b], rsem.at[b])

        def scatter(i, b, j):
            return pltpu.make_async_copy(rows_v.at[b], out_hbm.at[idx_v.at[j, i]], ssem.at[b])

        read(0, 0).start()

        @pl.loop(0, nch, step=2)
        def _(i0):
            for b in range(2):
                i = i0 + b
                read(i, b).wait()

                @pl.when(i + 1 < nch)
                def _():
                    @pl.when(i >= 1)
                    def _():
                        for j in range(nj):
                            scatter(i - 1, 1 - b, j).wait()
                    read(i + 1, 1 - b).start()

                for j in range(nj):
                    scatter(i, b, j).start()

        for b in range(2):
            for j in range(nj):
                scatter(nch - 2 + b, b, j).wait()

    return scatter_kernel(rows, dest.reshape(nj, nw, nch, ch))


def _gather_rows(table, idx):
    m = idx.shape[0]
    w = table.shape[1]
    ch = GATHER_CHUNK
    nw = SC_CORES * SC_SUBCORES
    nch = m // (nw * ch)
    assert m == nw * nch * ch and nch % 2 == 0
    per_w = nch * ch
    mesh = plsc.VectorSubcoreMesh(core_axis_name="c", subcore_axis_name="s",
                                  num_cores=SC_CORES, num_subcores=SC_SUBCORES)

    @functools.partial(
        pl.kernel, mesh=mesh,
        out_type=jax.ShapeDtypeStruct((m, w), table.dtype),
        scratch_types=[pltpu.VMEM((nch, ch), jnp.int32), pltpu.VMEM((2, ch, w), table.dtype),
                       pltpu.SemaphoreType.DMA((2,)), pltpu.SemaphoreType.DMA((2,))],
    )
    def gather_kernel(table_hbm, idx_hbm, out_hbm, idx_v, rows_v, gsem, wsem):
        wid = lax.axis_index("s") * SC_CORES + lax.axis_index("c")
        base = wid * per_w
        pltpu.sync_copy(idx_hbm.at[wid], idx_v)

        def gather(j, b):
            return pltpu.make_async_copy(table_hbm.at[idx_v.at[j]], rows_v.at[b], gsem.at[b])

        def write(j, b):
            off = pl.multiple_of(base + j * ch, ch)
            return pltpu.make_async_copy(rows_v.at[b], out_hbm.at[pl.ds(off, ch)], wsem.at[b])

        gather(0, 0).start()

        @pl.loop(0, nch, step=2)
        def _(i):
            for b in range(2):
                j = i + b
                gather(j, b).wait()

                @pl.when(j + 1 < nch)
                def _():
                    @pl.when(j >= 1)
                    def _():
                        write(j - 1, 1 - b).wait()
                    gather(j + 1, 1 - b).start()

                write(j, b).start()

        write(nch - 2, 0).wait()
        write(nch - 1, 1).wait()

    return gather_kernel(table, idx.reshape(nw, nch, ch))


def kernel(x, c, w_ada, b_ada, w_in, q_norm_g, k_norm_g, w_fourier, w_attn_o, w_out, ln1_g, ln1_b,
           w_router, b_router, w_gate_up, b_gate_up, w_down, b_down, ln2_g, ln2_b):
    bsz, s, d = x.shape
    n = bsz * s
    tm = min(ROW_TILE, s)
    l = 0

    mod = _ada(c, w_ada[l], b_ada[l])
    sh1, sc1, g1, sh2, sc2, g2 = [m[:, None, :] for m in jnp.split(mod, N_MOD, axis=-1)]

    dft_c, m1, m2, cw, sw, p, q = _dft_tables(s)
    cos, sin_a, sin_b = _rope_tables(s)

    fa, fb, qh, kh, vt, gf, ga = _inproj(
        x, sc1, sh1, w_in[l].astype(BF16), dft_c,
        q_norm_g[l].reshape(1, HEAD_DIM), k_norm_g[l].reshape(1, HEAD_DIM), cos, sin_a, sin_b, tm)

    zr, zi = _fft1(fa, fb, m1, cw, sw, p, q)
    yfm = _fft2(zr, zi, m2)

    k_bound = (math.sqrt(HEAD_DIM) * (1.0 + 2.0 ** -7)) * jnp.max(jnp.abs(k_norm_g[l])).reshape(1, 1)
    attn = _attention(qh, kh, vt, k_bound, min(ATTN_TQ, s))

    wf_b, wo_b, wout_b = w_fourier[l].astype(BF16), w_attn_o[l].astype(BF16), w_out[l].astype(BF16)
    bgu, bd = b_gate_up[l][:, None, :], b_down[l][:, None, :]
    tmb = MOE_BLOCK
    eids = jnp.arange(N_EXPERTS, dtype=jnp.int32)

    ngroups = MOE_BATCH_GROUPS if bsz % MOE_BATCH_GROUPS == 0 else 1
    gb = bsz // ngroups
    ng = gb * s
    out = None
    for grp in range(ngroups):
        b0 = grp * gb
        x1, u2p, idx_t, wgt_t, rank_t, counts = _mix(
            yfm, attn, gf, ga, x, g1, sc2, sh2, wf_b, wo_b, wout_b,
            ln1_g[l].reshape(1, d), ln1_b[l].reshape(1, d),
            w_router[l].T, b_router[l].reshape(N_EXPERTS, 1), tm, b0, gb)

        counts = counts.reshape(N_EXPERTS)
        pcounts = ((counts + tmb - 1) // tmb) * tmb
        pends = jnp.cumsum(pcounts)
        pstarts = pends - pcounts
        n_pad = ng * TOP_K + N_EXPERTS * tmb
        nblk = n_pad // tmb
        start_of = jnp.sum(jnp.where(idx_t[None] == eids[:, None, None], pstarts[:, None, None], 0), axis=0)
        dest_t = start_of + rank_t
        block_starts = jnp.arange(nblk, dtype=jnp.int32) * tmb
        block_e = jnp.sum((block_starts[:, None] >= pends[None, :]).astype(jnp.int32), axis=1)
        block_e = jnp.minimum(block_e, N_EXPERTS - 1)
        valid_end = jnp.sum(jnp.where(block_e[:, None] == eids[None, :], (pstarts + counts)[None, :], 0), axis=1)
        block_nvalid = jnp.clip(valid_end - block_starts, 0, tmb)
        block_new = jnp.concatenate([jnp.ones((1,), jnp.int32),
                                     (block_e[1:] != block_e[:-1]).astype(jnp.int32)])

        xs = _scatter_rows(u2p.reshape(ng, d // 2), dest_t, n_pad)
        ys = _experts(block_e, block_nvalid, block_new, xs, w_gate_up[l], bgu, w_down[l], bd, tmb)
        yg = _gather_rows(ys, dest_t.reshape(-1)).reshape(TOP_K, ng, d // 2)
        out = _final(yg, wgt_t.T, x1, g2, ln2_g[l].reshape(1, d), ln2_b[l].reshape(1, d), tm,
                     b0, bsz, out)
    return out
```

```python
import functools
import math

import jax
import jax.numpy as jnp
import numpy as np
from jax import lax
from jax.experimental import pallas as pl
from jax.experimental.pallas import tpu as pltpu
from jax.experimental.pallas import tpu_sc as plsc

F32 = jnp.float32
BF16 = jnp.bfloat16

D_MODEL = 1024
GRID_W = 64
N_Q_HEADS = 8
N_KV_HEADS = 2
HEAD_DIM = 128
Q_GROUP = N_Q_HEADS // N_KV_HEADS
ROPE_THETA = 10000.0
ROPE_AXIS_DIM = HEAD_DIM // 2
N_FOURIER_GROUPS = 4
FOURIER_GROUP_DIM = 128
FOURIER_DIM = N_FOURIER_GROUPS * FOURIER_GROUP_DIM
Q_DIM = N_Q_HEADS * HEAD_DIM
KV_DIM = N_KV_HEADS * HEAD_DIM
N_EXPERTS = 32
TOP_K = 4
D_EXPERT = 1024
SWIGLU_LIMIT = 7.0
SWIGLU_ALPHA = 1.702
N_MOD = 6
DEPTH = 1
DEEPNORM_ALPHA = (2 * DEPTH) ** 0.25
LN_EPS = 1e-5
ADA_EPS = 1e-6
QK_EPS = 1e-6
LOG2E = 1.4426950408889634

V7X_VMEM_BYTES = 64 * 1024 * 1024
VMEM_LIMIT = V7X_VMEM_BYTES - 8 * 1024 * 1024
LANES = 128
BF16_SUBLANES = 16

FFT_Q = 128
ROW_TILE = 512
ATTN_TQ = 1024
ATTN_FAST_KCHUNKS = 2
DENOMINATOR_GUARD = 2.0 ** -80
INPROJ_ROW_GROUPS = 2
MIX_ROW_GROUPS = 2
MOE_BLOCK = 512
MOE_LAST_GROUP_DIVISOR = 4

SC_CORES = 2
SC_SUBCORES = 16
GATHER_CHUNK = 64


def _cparams(sem):
    return pltpu.CompilerParams(dimension_semantics=sem, vmem_limit_bytes=VMEM_LIMIT)


def _layer_norm(x, eps):
    mu = jnp.mean(x, axis=-1, keepdims=True)
    xc = x - mu
    var = jnp.mean(xc * xc, axis=-1, keepdims=True)
    return xc * lax.rsqrt(var + eps)


def _pack_bf16_pairs(x):
    n = x.shape[1] // 2
    bits = pltpu.bitcast(x.astype(BF16).astype(F32), jnp.uint32)
    word = (bits[:, :n] >> 16) | (bits[:, n:] & jnp.uint32(0xFFFF0000))
    return pltpu.bitcast(word, jnp.int32)


def _unpack_bf16_pairs(w):
    bits = pltpu.bitcast(w, jnp.uint32)
    lo = pltpu.bitcast(bits << 16, F32)
    hi = pltpu.bitcast(bits & jnp.uint32(0xFFFF0000), F32)
    return lo, hi


def _ada_kernel(c_ref, w_ref, b_ref, o_ref):
    c = c_ref[...]
    cond = c * jax.nn.sigmoid(c)
    o_ref[...] = jnp.dot(cond, w_ref[...], preferred_element_type=F32,
                         precision=lax.Precision.HIGHEST) + b_ref[...]


def _ada(c, w_ada, b_ada):
    bsz, d = c.shape
    n = w_ada.shape[1]
    tn = 1536
    return pl.pallas_call(
        _ada_kernel,
        grid=(n // tn,),
        in_specs=[pl.BlockSpec((bsz, d), lambda j: (0, 0)),
                  pl.BlockSpec((d, tn), lambda j: (0, j)),
                  pl.BlockSpec((1, tn), lambda j: (0, j))],
        out_specs=pl.BlockSpec((bsz, tn), lambda j: (0, j)),
        out_shape=jax.ShapeDtypeStruct((bsz, n), F32),
        compiler_params=_cparams(("arbitrary",)),
        name="ada",
    )(c, w_ada, b_ada.reshape(1, n))


def _rope(xh, cos, sin_a, sin_b):
    a = pltpu.roll(xh, 32, 1)
    b = pltpu.roll(xh, 96, 1)
    return xh * cos + b * sin_b + a * sin_a


def _inproj_kernel(x_ref, sc_ref, sh_ref, w_ref, dft_ref, qg_ref, kg_ref,
                   cos_ref, sa_ref, sb_ref,
                   fa_ref, fb_ref, q_ref, k_ref, vt_ref, gf_ref, ga_ref):
    tm = x_ref.shape[0]
    rg = tm // INPROJ_ROW_GROUPS
    q_scale = LOG2E / math.sqrt(HEAD_DIM)
    for i in range(INPROJ_ROW_GROUPS):
        rows = slice(i * rg, (i + 1) * rg)
        u = _layer_norm(x_ref[rows, :], ADA_EPS) * (1.0 + sc_ref[...]) + sh_ref[...]
        ub = u.astype(BF16)
        cos, sa, sb = cos_ref[rows, :], sa_ref[rows, :], sb_ref[rows, :]

        def norm_rope(z, gain, scale):
            ms = jnp.mean(z * z, axis=-1, keepdims=True)
            zn = z * lax.rsqrt(ms + QK_EPS) * gain
            return (_rope(zn, cos, sa, sb) * scale).astype(BF16)

        c0 = 0
        zf = jnp.dot(ub, w_ref[:, c0:c0 + FOURIER_DIM], preferred_element_type=F32).astype(BF16)
        for g in range(N_FOURIER_GROUPS):
            sl = slice(g * FOURIER_GROUP_DIM, (g + 1) * FOURIER_GROUP_DIM)
            ab = jnp.dot(zf[:, sl], dft_ref[...], preferred_element_type=F32)
            fa_ref[rows, sl] = ab[:, :FOURIER_GROUP_DIM].astype(BF16)
            fb_ref[rows, sl] = ab[:, FOURIER_GROUP_DIM:].astype(BF16)
        c0 += FOURIER_DIM

        zq = jnp.dot(ub, w_ref[:, c0:c0 + Q_DIM], preferred_element_type=F32)
        for h in range(N_Q_HEADS):
            sl = slice(h * HEAD_DIM, (h + 1) * HEAD_DIM)
            q_ref[rows, sl] = norm_rope(zq[:, sl], qg_ref[...], q_scale)
        c0 += Q_DIM

        zk = jnp.dot(ub, w_ref[:, c0:c0 + KV_DIM], preferred_element_type=F32)
        for h in range(N_KV_HEADS):
            sl = slice(h * HEAD_DIM, (h + 1) * HEAD_DIM)
            k_ref[rows, sl] = norm_rope(zk[:, sl], kg_ref[...], 1.0)
        c0 += KV_DIM

        zv = jnp.dot(ub, w_ref[:, c0:c0 + KV_DIM], preferred_element_type=F32)
        for h in range(N_KV_HEADS):
            vt_ref[h, :, rows] = zv[:, h * HEAD_DIM:(h + 1) * HEAD_DIM].T.astype(BF16)
        c0 += KV_DIM

        zg = jnp.dot(ub, w_ref[:, c0:c0 + D_MODEL], preferred_element_type=F32)
        gf_ref[rows, :] = jax.nn.sigmoid(zg).astype(BF16)
        c0 += D_MODEL
        zg = jnp.dot(ub, w_ref[:, c0:c0 + D_MODEL], preferred_element_type=F32)
        ga_ref[rows, :] = jax.nn.sigmoid(zg).astype(BF16)


def _inproj(x, sc1, sh1, w_in, dft_c, qg, kg, cos, sin_a, sin_b, tm):
    bsz, s, d = x.shape
    ncol = w_in.shape[1]
    row = lambda w: pl.BlockSpec((None, tm, w), lambda b, i: (b, i, 0))
    mod = pl.BlockSpec((None, 1, d), lambda b, i: (b, 0, 0))
    full = lambda shp: pl.BlockSpec(shp, lambda b, i: (0,) * len(shp))
    tab = pl.BlockSpec((tm, HEAD_DIM), lambda b, i: (i, 0))
    vt_rows = HEAD_DIM
    vt_spec = pl.BlockSpec((None, N_KV_HEADS, None, vt_rows, tm), lambda b, i: (b, 0, i, 0, 0))
    vt_shape = jax.ShapeDtypeStruct((bsz, N_KV_HEADS, s // tm, vt_rows, tm), BF16)
    rows_out = lambda w: (row(w), jax.ShapeDtypeStruct((bsz, s, w), BF16))
    outs = [rows_out(FOURIER_DIM), rows_out(FOURIER_DIM), rows_out(Q_DIM), rows_out(KV_DIM),
            (vt_spec, vt_shape), rows_out(D_MODEL), rows_out(D_MODEL)]
    return pl.pallas_call(
        _inproj_kernel,
        grid=(bsz, s // tm),
        in_specs=[row(d), mod, mod, full((d, ncol)), full(dft_c.shape),
                  full((1, HEAD_DIM)), full((1, HEAD_DIM)), tab, tab, tab],
        out_specs=[o[0] for o in outs],
        out_shape=[o[1] for o in outs],
        compiler_params=_cparams(("parallel", "parallel")),
        name="inproj",
    )(x, sc1, sh1, w_in, dft_c, qg, kg, cos, sin_a, sin_b)


def _fft1_kernel(a_ref, b_ref, m_ref, cw_ref, sw_ref, zr_ref, zi_ref, a32, b32):
    p, nb, _ = a_ref.shape
    a32[...] = a_ref[...].astype(F32)
    b32[...] = b_ref[...].astype(F32)
    for j in range(nb):
        ab = jnp.concatenate([a32[:, j, :], b32[:, j, :]], axis=0).astype(BF16)
        t = jnp.dot(m_ref[...], ab, preferred_element_type=F32)
        tr, ti = t[:p], t[p:]
        cw, sw = cw_ref[j], sw_ref[j]
        zr_ref[j] = (tr * cw - ti * sw).astype(BF16)
        zi_ref[j] = (ti * cw + tr * sw).astype(BF16)


def _fft1(fa, fb, m1, cw, sw, p, q):
    bsz, s, c = fa.shape
    nb = BF16_SUBLANES
    inp = pl.BlockSpec((None, p, nb, c), lambda b, j: (b, 0, j, 0))
    tw = pl.BlockSpec((nb, p, 1), lambda b, j: (j, 0, 0))
    out = pl.BlockSpec((None, nb, p, c), lambda b, j: (b, j, 0, 0))
    return pl.pallas_call(
        _fft1_kernel,
        grid=(bsz, q // nb),
        in_specs=[inp, inp, pl.BlockSpec(m1.shape, lambda b, j: (0, 0)), tw, tw],
        out_specs=[out, out],
        out_shape=[jax.ShapeDtypeStruct((bsz, q, p, c), BF16)] * 2,
        scratch_shapes=[pltpu.VMEM((p, nb, c), F32)] * 2,
        compiler_params=_cparams(("parallel", "parallel")),
        name="fft1",
    )(fa.reshape(bsz, p, q, c), fb.reshape(bsz, p, q, c), m1, cw, sw)


def _fft2_kernel(zr_ref, zi_ref, m_ref, y_ref, zr32, zi32, y32):
    zr32[...] = zr_ref[...].astype(F32)
    zi32[...] = zi_ref[...].astype(F32)
    for j in range(zr_ref.shape[1]):
        z = jnp.concatenate([zr32[:, j, :], zi32[:, j, :]], axis=0).astype(BF16)
        y32[:, j, :] = jnp.dot(m_ref[...], z, preferred_element_type=F32)
    y_ref[...] = y32[...].astype(BF16)


def _fft2(zr, zi, m2):
    bsz, q, p, c = zr.shape
    pc = BF16_SUBLANES
    blk = pl.BlockSpec((None, q, pc, c), lambda b, j: (b, 0, j, 0))
    y = pl.pallas_call(
        _fft2_kernel,
        grid=(bsz, p // pc),
        in_specs=[blk, blk, pl.BlockSpec(m2.shape, lambda b, j: (0, 0))],
        out_specs=blk,
        out_shape=jax.ShapeDtypeStruct((bsz, q, p, c), BF16),
        scratch_shapes=[pltpu.VMEM((q, pc, c), F32)] * 3,
        compiler_params=_cparams(("parallel", "parallel")),
        name="fft2",
    )(zr, zi, m2)
    return y.reshape(bsz, q * p, c)


def _dft_tables(s):
    q = FFT_Q
    p = s // q

    def cs(n_rows, n_cols, n):
        r = np.arange(n_rows, dtype=np.int64)[:, None]
        c = np.arange(n_cols, dtype=np.int64)[None, :]
        ang = ((r * c) % n).astype(np.float64) * (2.0 * math.pi / n)
        return np.cos(ang).astype(np.float32), np.sin(ang).astype(np.float32)

    cc, sc = cs(FOURIER_GROUP_DIM, FOURIER_GROUP_DIM, FOURIER_GROUP_DIM)
    dft_c = jnp.asarray(np.concatenate([cc, sc], axis=1), dtype=BF16)
    cp, sp = cs(p, p, p)
    m1 = jnp.asarray(np.block([[cp, -sp], [sp, cp]]), dtype=BF16)
    cq, sq = cs(q, q, q)
    scale = np.float32(1.0 / math.sqrt(s * FOURIER_GROUP_DIM))
    m2 = jnp.asarray(np.concatenate([cq, -sq], axis=1) * scale, dtype=BF16)
    cw, sw = cs(q, p, s)
    return dft_c, m1, m2, jnp.asarray(cw[:, :, None]), jnp.asarray(sw[:, :, None]), p, q


def _rope_tables(s):
    f32 = np.float32
    rows = s // GRID_W
    row_ids = np.repeat(np.arange(rows, dtype=f32), GRID_W)
    col_ids = np.tile(np.arange(GRID_W, dtype=f32), rows)
    freqs = (f32(ROPE_THETA) ** (-np.arange(0, ROPE_AXIS_DIM, 2, dtype=f32) / f32(ROPE_AXIS_DIM))).astype(f32)
    ang_r = (row_ids[:, None] * freqs).astype(f32)
    ang_c = (col_ids[:, None] * freqs).astype(f32)
    cr, sr, cc, sc = np.cos(ang_r), np.sin(ang_r), np.cos(ang_c), np.sin(ang_c)
    z = np.zeros_like(sr)
    cos = np.concatenate([cr, cr, cc, cc], axis=1)
    sin_b = np.concatenate([-sr, z, -sc, z], axis=1)
    sin_a = np.concatenate([z, sr, z, sc], axis=1)
    return jnp.asarray(cos, F32), jnp.asarray(sin_a, F32), jnp.asarray(sin_b, F32)


def _attn_kernel(kb_ref, q_ref, k_ref, vt_ref, o_ref, m_ref, acc_ref, al0, al1, s0, s1, p0, p1,
                 qt_ref, mx0, mx1, qa_ref, l_ref, ps0, ps1):
    al_ref, s_ref, p_ref, mx_ref, ps_ref = (al0, al1), (s0, s1), (p0, p1), (mx0, mx1), (ps0, ps1)
    nchunk = vt_ref.shape[0]
    tk = vt_ref.shape[2]
    tq = q_ref.shape[0]
    heads = range(Q_GROUP)

    def k_chunk(c):
        return k_ref[pl.ds(pl.multiple_of(c * tk, tk), tk), :]

    def write_out():
        for g in heads:
            out_t = acc_ref[g] / l_ref[g]
            o_ref[:, g * HEAD_DIM:(g + 1) * HEAD_DIM] = out_t.T.astype(BF16)

    row0 = lax.broadcasted_iota(jnp.int32, (HEAD_DIM, tq), 0) == 0
    for g in heads:
        qt = q_ref[:, g * HEAD_DIM:(g + 1) * HEAD_DIM].astype(F32).T
        bound = jnp.sqrt(jnp.sum(qt * qt, axis=0, keepdims=True)) * kb_ref[...]
        qa_ref[g, :HEAD_DIM, :] = qt.astype(BF16)
        qa_ref[g, HEAD_DIM:, :] = jnp.where(row0, -bound, 0.0).astype(BF16)
    kf = ATTN_FAST_KCHUNKS
    tkf = kf * tk
    nfast = nchunk // kf
    ones_col = (lax.broadcasted_iota(jnp.int32, (tkf, HEAD_DIM), 1) == 0).astype(BF16)

    def probs_into(c, slot):
        kc = k_ref[pl.ds(pl.multiple_of(c * tkf, tkf), tkf), :]
        ka = jnp.concatenate([kc, ones_col], axis=1)
        for g in heads:
            p32 = jnp.exp2(jnp.dot(ka, qa_ref[g], preferred_element_type=F32))
            l_ref[g] += jnp.sum(p32, axis=0, keepdims=True)
            p_ref[slot][g] = p32.astype(BF16)

    def pv_add(c, slot):
        vc = jnp.concatenate([vt_ref[kf * c + j] for j in range(kf)], axis=1)
        for g in heads:
            acc_ref[g] += jnp.dot(vc, p_ref[slot][g], preferred_element_type=F32)

    acc_ref[...] = jnp.zeros(acc_ref.shape, F32)
    l_ref[...] = jnp.zeros(l_ref.shape, F32)
    probs_into(0, 0)

    def fast_pair(i, carry):
        probs_into(2 * i + 1, 1)
        pv_add(2 * i, 0)
        probs_into(2 * i + 2, 0)
        pv_add(2 * i + 1, 1)
        return carry

    lax.fori_loop(0, (nfast - 2) // 2, fast_pair, 0)
    probs_into(nfast - 1, 1)
    pv_add(nfast - 2, 0)
    pv_add(nfast - 1, 1)
    write_out()
    den_min = l_ref[0]
    for g in heads[1:]:
        den_min = jnp.minimum(den_min, l_ref[g])
    safe = jnp.min(den_min) > DENOMINATOR_GUARD

    def scores_into(c, slot):
        kc = k_chunk(c)
        for g in heads:
            sc = jnp.dot(kc, qt_ref[g], preferred_element_type=F32)
            s_ref[slot][g] = sc
            parts = [jnp.max(sc[r * (tk // 4):(r + 1) * (tk // 4)], axis=0, keepdims=True) for r in range(4)]
            mx_ref[slot][g] = jnp.maximum(jnp.maximum(parts[0], parts[1]), jnp.maximum(parts[2], parts[3]))

    def softmax(slot):
        for g in heads:
            m_old = m_ref[g]
            m_new = jnp.maximum(m_old, mx_ref[slot][g])
            al_ref[slot][g] = jnp.exp2(m_old - m_new)
            m_ref[g] = m_new
            p32 = jnp.exp2(s_ref[slot][g] - m_new)
            ps_ref[slot][g] = jnp.sum(p32, axis=0, keepdims=True)
            p_ref[slot][g, :tk, :] = p32.astype(BF16)

    def pv_update(c, slot):
        vc = vt_ref[c]
        for g in heads:
            acc_ref[g] = al_ref[slot][g] * acc_ref[g] + jnp.dot(vc, p_ref[slot][g, :tk, :],
                                                                preferred_element_type=F32)
            l_ref[g] = al_ref[slot][g] * l_ref[g] + ps_ref[slot][g]

    def step(c, slot):
        scores_into(c + 1, 1 - slot)
        pv_update(c - 1, 1 - slot)
        softmax(slot)

    @pl.when(jnp.logical_not(safe))
    def _():
        for g in heads:
            qt_ref[g] = qa_ref[g, :HEAD_DIM, :]
        scores_into(0, 0)
        m_ref[...] = jnp.full(m_ref.shape, -jnp.inf, F32)
        acc_ref[...] = jnp.zeros(acc_ref.shape, F32)
        l_ref[...] = jnp.zeros(l_ref.shape, F32)
        softmax(0)
        scores_into(1, 1)

        def pair(i, carry):
            step(2 * i + 1, 1)
            step(2 * i + 2, 0)
            return carry

        lax.fori_loop(0, (nchunk - 2) // 2, pair, 0)
        pv_update(nchunk - 2, 0)
        softmax(1)
        pv_update(nchunk - 1, 1)
        write_out()


def _attention(q, k, vt, k_norm_bound, tq):
    bsz, s, _ = q.shape
    rows, tk = vt.shape[3], vt.shape[4]
    nchunk = s // tk
    assert nchunk % (2 * ATTN_FAST_KCHUNKS) == 0 and nchunk >= 4 * ATTN_FAST_KCHUNKS
    gw = Q_GROUP * HEAD_DIM
    qspec = pl.BlockSpec((None, tq, gw), lambda b, h, i: (b, i, h))
    kspec = pl.BlockSpec((None, s, HEAD_DIM), lambda b, h, i: (b, 0, h))
    vspec = pl.BlockSpec((None, None, nchunk, rows, tk), lambda b, h, i: (b, h, 0, 0, 0))
    head_vec = pltpu.VMEM((Q_GROUP, 1, tq), F32)
    return pl.pallas_call(
        _attn_kernel,
        grid=(bsz, N_KV_HEADS, s // tq),
        in_specs=[pl.BlockSpec((1, 1), lambda b, h, i: (0, 0)), qspec, kspec, vspec],
        out_specs=qspec,
        out_shape=jax.ShapeDtypeStruct((bsz, s, Q_DIM), BF16),
        scratch_shapes=[head_vec, pltpu.VMEM((Q_GROUP, rows, tq), F32), head_vec, head_vec]
        + [pltpu.VMEM((Q_GROUP, tk, tq), F32)] * 2
        + [pltpu.VMEM((Q_GROUP, ATTN_FAST_KCHUNKS * tk, tq), BF16)] * 2
        + [pltpu.VMEM((Q_GROUP, HEAD_DIM, tq), BF16), head_vec, head_vec,
           pltpu.VMEM((Q_GROUP, 2 * HEAD_DIM, tq), BF16), head_vec, head_vec, head_vec],
        compiler_params=_cparams(("parallel", "parallel", "parallel")),
        name="attn",
    )(k_norm_bound, q, k, vt)


def _mix_kernel(yf_ref, at_ref, gf_ref, ga_ref, x_ref, g1_ref, sc2_ref, sh2_ref,
                wf_ref, wo_ref, wout_ref, l1g_ref, l1b_ref, wr_ref, br_ref, tri_ref,
                x1_ref, u2_ref, idx_ref, wgt_ref, rank_ref, cnt_ref, cnt_acc):
    first = jnp.logical_and(pl.program_id(0) == 0, pl.program_id(1) == 0)

    @pl.when(first)
    def _():
        cnt_acc[...] = jnp.zeros_like(cnt_acc)

    tm_all = x_ref.shape[0]
    rg = tm_all // MIX_ROW_GROUPS
    u_his, u_los = [], []
    groups = [slice(i * rg, (i + 1) * rg) for i in range(MIX_ROW_GROUPS)]
    ms = []
    for rows in groups:
        yf = jnp.dot(yf_ref[rows, :], wf_ref[...], preferred_element_type=F32)
        ya = jnp.dot(at_ref[rows, :], wo_ref[...], preferred_element_type=F32)
        ms.append((gf_ref[rows, :].astype(F32) * yf + ga_ref[rows, :].astype(F32) * ya).astype(BF16))
    hs = [jnp.dot(m, wout_ref[...], preferred_element_type=F32) for m in ms]
    for rows, h in zip(groups, hs):
        r = DEEPNORM_ALPHA * x_ref[rows, :] + g1_ref[...] * h
        x1 = _layer_norm(r, LN_EPS) * l1g_ref[...] + l1b_ref[...]
        x1_ref[rows, :] = x1
        u2 = _layer_norm(x1, ADA_EPS) * (1.0 + sc2_ref[...]) + sh2_ref[...]
        u2_ref[rows, :] = _pack_bf16_pairs(u2)
        u_hi = u2.astype(BF16)
        u_his.append(u_hi)
        u_los.append((u2 - u_hi.astype(F32)).astype(BF16))

    nt = (((1,), (1,)), ((), ()))
    u_hi = jnp.concatenate(u_his, axis=0)
    u_lo = jnp.concatenate(u_los, axis=0)
    w = wr_ref[...]
    w_hi = w.astype(BF16)
    w_lo = (w - w_hi.astype(F32)).astype(BF16)
    logits = (lax.dot_general(w_hi, u_hi, nt, preferred_element_type=F32)
              + lax.dot_general(w_lo, u_hi, nt, preferred_element_type=F32)
              + lax.dot_general(w_hi, u_lo, nt, preferred_element_type=F32)) + br_ref[...]

    ne, tm = logits.shape
    eid = lax.broadcasted_iota(jnp.int32, (ne, tm), 0)
    work = logits
    vals, idxs, hots = [], [], []
    for _ in range(TOP_K):
        mx = jnp.max(work, axis=0, keepdims=True)
        ix = jnp.min(jnp.where(work == mx, eid, ne), axis=0, keepdims=True)
        hot = eid == ix
        work = jnp.where(hot, -jnp.inf, work)
        vals.append(mx)
        idxs.append(ix)
        hots.append(hot)
    ex = [jnp.exp(v - vals[0]) for v in vals]
    den = ex[0] + ex[1] + ex[2] + ex[3]
    wgt_ref[...] = jnp.concatenate([e / den for e in ex], axis=0)
    idx_ref[...] = jnp.concatenate(idxs, axis=0)

    sel = hots[0] | hots[1] | hots[2] | hots[3]
    mask = jnp.where(sel, 1.0, 0.0)
    prefix = jnp.dot(mask.astype(BF16), tri_ref[...], preferred_element_type=F32)
    pos = prefix + cnt_acc[...]
    ranks = [jnp.sum(jnp.where(hot, pos, 0.0), axis=0, keepdims=True) for hot in hots]
    rank_ref[...] = jnp.concatenate(ranks, axis=0).astype(jnp.int32)
    cnt_acc[...] += jnp.sum(mask, axis=1, keepdims=True)
    cnt_ref[...] = cnt_acc[...].astype(jnp.int32)


def _mix(yfm, attn, gf, ga, x, g1, sc2, sh2, wf, wo, wout, l1g, l1b, wr_t, br, tm, b0, bsz):
    _, s, d = x.shape
    n = bsz * s
    nt = s // tm
    tri = (jnp.arange(tm)[:, None] < jnp.arange(tm)[None, :]).astype(BF16)
    row_in = lambda w: pl.BlockSpec((None, tm, w), lambda b, i: (b + b0, i, 0))
    row = lambda w: pl.BlockSpec((None, tm, w), lambda b, i: (b, i, 0))
    mod = pl.BlockSpec((None, 1, d), lambda b, i: (b + b0, 0, 0))
    full = lambda shp: pl.BlockSpec(shp, lambda b, i: (0,) * len(shp))
    tok = pl.BlockSpec((TOP_K, tm), lambda b, i: (0, b * nt + i))
    return pl.pallas_call(
        _mix_kernel,
        grid=(bsz, nt),
        in_specs=[row_in(FOURIER_DIM), row_in(Q_DIM), row_in(d), row_in(d), row_in(d), mod, mod, mod,
                  full(wf.shape), full(wo.shape), full(wout.shape), full((1, d)), full((1, d)),
                  full(wr_t.shape), full((N_EXPERTS, 1)), full((tm, tm))],
        out_specs=[row(d), row(d // 2), tok, tok, tok, full((N_EXPERTS, 1))],
        out_shape=[jax.ShapeDtypeStruct((bsz, s, d), F32),
                   jax.ShapeDtypeStruct((bsz, s, d // 2), jnp.int32),
                   jax.ShapeDtypeStruct((TOP_K, n), jnp.int32),
                   jax.ShapeDtypeStruct((TOP_K, n), F32),
                   jax.ShapeDtypeStruct((TOP_K, n), jnp.int32),
                   jax.ShapeDtypeStruct((N_EXPERTS, 1), jnp.int32)],
        scratch_shapes=[pltpu.VMEM((N_EXPERTS, 1), F32)],
        compiler_params=_cparams(("arbitrary", "arbitrary")),
        name="mix",
    )(yfm, attn, gf, ga, x, g1, sc2, sh2, wf, wo, wout, l1g, l1b, wr_t, br, tri)


def _expert_kernel(be_ref, nv_ref, new_ref, xs_ref, wgu_ref, bgu_ref, wd_ref, bd_ref, ys_ref,
                   wgu_bf, wd_bf):
    del be_ref
    i = pl.program_id(0)
    nvalid = nv_ref[i]

    @pl.when(new_ref[i] == 1)
    def _():
        wgu_bf[...] = wgu_ref[...].astype(BF16)
        wd_bf[...] = wd_ref[...].astype(BF16)

    @pl.when(nvalid > 0)
    def _():
        xw = xs_ref[...]
        row = lax.broadcasted_iota(jnp.int32, xw.shape, 0)
        lo, hi = _unpack_bf16_pairs(jnp.where(row < nvalid, xw, 0))
        xb = jnp.concatenate([lo.astype(BF16), hi.astype(BF16)], axis=1)
        h = jnp.dot(xb, wgu_bf[...], preferred_element_type=F32) + bgu_ref[...]
        gate = jnp.minimum(h[:, :D_EXPERT], SWIGLU_LIMIT)
        up = jnp.clip(h[:, D_EXPERT:], -SWIGLU_LIMIT, SWIGLU_LIMIT)
        act = (up + 1.0) * gate * jax.nn.sigmoid(SWIGLU_ALPHA * gate)
        y = jnp.dot(act.astype(BF16), wd_bf[...], preferred_element_type=F32) + bd_ref[...]
        ys_ref[...] = _pack_bf16_pairs(y)

    @pl.when(nvalid == 0)
    def _():
        ys_ref[...] = jnp.zeros_like(ys_ref)


def _experts(block_e, block_nvalid, block_new, xs, wgu, bgu, wd, bd, tmb):
    n_pad, dh = xs.shape
    d = 2 * dh
    grid_spec = pltpu.PrefetchScalarGridSpec(
        num_scalar_prefetch=3,
        grid=(n_pad // tmb,),
        in_specs=[pl.BlockSpec((tmb, dh), lambda i, be, nv, nw: (i, 0)),
                  pl.BlockSpec((None, d, 2 * D_EXPERT), lambda i, be, nv, nw: (be[i], 0, 0)),
                  pl.BlockSpec((None, 1, 2 * D_EXPERT), lambda i, be, nv, nw: (be[i], 0, 0)),
                  pl.BlockSpec((None, D_EXPERT, d), lambda i, be, nv, nw: (be[i], 0, 0)),
                  pl.BlockSpec((None, 1, d), lambda i, be, nv, nw: (be[i], 0, 0))],
        out_specs=pl.BlockSpec((tmb, dh), lambda i, be, nv, nw: (i, 0)),
        scratch_shapes=[pltpu.VMEM((d, 2 * D_EXPERT), BF16), pltpu.VMEM((D_EXPERT, d), BF16)],
    )
    return pl.pallas_call(
        _expert_kernel,
        grid_spec=grid_spec,
        out_shape=jax.ShapeDtypeStruct((n_pad, dh), jnp.int32),
        compiler_params=_cparams(("arbitrary",)),
        name="experts",
    )(block_e, block_nvalid, block_new, xs, wgu, bgu, wd, bd)


def _final_kernel(yg_ref, w_ref, x1_ref, g2_ref, lg_ref, lb_ref, o_ref):
    w = w_ref[...]
    acc_lo = None
    for j in range(TOP_K):
        lo, hi = _unpack_bf16_pairs(yg_ref[j])
        wj = w[:, j:j + 1]
        acc_lo = lo * wj if acc_lo is None else acc_lo + lo * wj
        acc_hi = hi * wj if j == 0 else acc_hi + hi * wj
    h = jnp.concatenate([acc_lo, acc_hi], axis=1)
    r = DEEPNORM_ALPHA * x1_ref[...] + g2_ref[...] * h
    o_ref[...] = _layer_norm(r, LN_EPS) * lg_ref[...] + lb_ref[...]


def _final(yg, w_tok, x1, g2, lg, lb, tm, b0, total_bsz, out_so_far):
    bsz, s, d = x1.shape
    nt = s // tm
    row = pl.BlockSpec((None, tm, d), lambda b, i: (b, i, 0))
    full = pl.BlockSpec((1, d), lambda b, i: (0, 0))
    in_specs = [pl.BlockSpec((TOP_K, tm, d // 2), lambda b, i: (0, b * nt + i, 0)),
                pl.BlockSpec((tm, TOP_K), lambda b, i: (b * nt + i, 0)),
                row, pl.BlockSpec((None, 1, d), lambda b, i: (b + b0, 0, 0)), full, full]
    args = [yg, w_tok, x1, g2, lg, lb]
    aliases = {}
    if out_so_far is not None:
        in_specs.append(pl.BlockSpec(memory_space=pl.ANY))
        args.append(out_so_far)
        aliases = {len(args) - 1: 0}
    body = _final_kernel if out_so_far is None else (lambda *refs: _final_kernel(*refs[:6], refs[7]))
    return pl.pallas_call(
        body,
        grid=(bsz, nt),
        in_specs=in_specs,
        out_specs=pl.BlockSpec((None, tm, d), lambda b, i: (b + b0, i, 0)),
        out_shape=jax.ShapeDtypeStruct((total_bsz, s, d), F32),
        input_output_aliases=aliases,
        compiler_params=_cparams(("parallel", "parallel")),
        name="final",
    )(*args)


def _scatter_rows(rows, dest, m_out):
    n, w = rows.shape
    nj = dest.shape[0]
    ch = GATHER_CHUNK
    nw = SC_CORES * SC_SUBCORES
    nch = n // (nw * ch)
    assert n == nw * nch * ch and nch % 2 == 0
    per_w = nch * ch
    mesh = plsc.VectorSubcoreMesh(core_axis_name="c", subcore_axis_name="s",
                                  num_cores=SC_CORES, num_subcores=SC_SUBCORES)

    @functools.partial(
        pl.kernel, mesh=mesh,
        out_type=jax.ShapeDtypeStruct((m_out, w), rows.dtype),
        scratch_types=[pltpu.VMEM((nj, nch, ch), jnp.int32), pltpu.VMEM((2, ch, w), rows.dtype),
                       pltpu.SemaphoreType.DMA((2,)), pltpu.SemaphoreType.DMA((2,))],
    )
    def scatter_kernel(rows_hbm, dest_hbm, out_hbm, idx_v, rows_v, rsem, ssem):
        wid = lax.axis_index("s") * SC_CORES + lax.axis_index("c")
        base = wid * per_w
        for j in range(nj):
            pltpu.sync_copy(dest_hbm.at[j, wid], idx_v.at[j])

        def read(i, b):
            off = pl.multiple_of(base + i * ch, ch)
            return pltpu.make_async_copy(rows_hbm.at[pl.ds(off, ch)], rows_v.at[b], rsem.at[b])

        def scatter(i, b, j):
            return pltpu.make_async_copy(rows_v.at[b], out_hbm.at[idx_v.at[j, i]], ssem.at[b])

        read(0, 0).start()

        @pl.loop(0, nch, step=2)
        def _(i0):
            for b in range(2):
                i = i0 + b
                read(i, b).wait()

                @pl.when(i + 1 < nch)
                def _():
                    @pl.when(i >= 1)
                    def _():
                        for j in range(nj):
                            scatter(i - 1, 1 - b, j).wait()
                    read(i + 1, 1 - b).start()

                for j in range(nj):
                    scatter(i, b, j).start()

        for b in range(2):
            for j in range(nj):
                scatter(nch - 2 + b, b, j).wait()

    return scatter_kernel(rows, dest.reshape(nj, nw, nch, ch))


def _gather_rows(table, idx):
    m = idx.shape[0]
    w = table.shape[1]
    ch = GATHER_CHUNK
    nw = SC_CORES * SC_SUBCORES
    nch = m // (nw * ch)
    assert m == nw * nch * ch and nch % 2 == 0
    per_w = nch * ch
    mesh = plsc.VectorSubcoreMesh(core_axis_name="c", subcore_axis_name="s",
                                  num_cores=SC_CORES, num_subcores=SC_SUBCORES)

    @functools.partial(
        pl.kernel, mesh=mesh,
        out_type=jax.ShapeDtypeStruct((m, w), table.dtype),
        scratch_types=[pltpu.VMEM((nch, ch), jnp.int32), pltpu.VMEM((2, ch, w), table.dtype),
                       pltpu.SemaphoreType.DMA((2,)), pltpu.SemaphoreType.DMA((2,))],
    )
    def gather_kernel(table_hbm, idx_hbm, out_hbm, idx_v, rows_v, gsem, wsem):
        wid = lax.axis_index("s") * SC_CORES + lax.axis_index("c")
        base = wid * per_w
        pltpu.sync_copy(idx_hbm.at[wid], idx_v)

        def gather(j, b):
            return pltpu.make_async_copy(table_hbm.at[idx_v.at[j]], rows_v.at[b], gsem.at[b])

        def write(j, b):
            off = pl.multiple_of(base + j * ch, ch)
            return pltpu.make_async_copy(rows_v.at[b], out_hbm.at[pl.ds(off, ch)], wsem.at[b])

        gather(0, 0).start()

        @pl.loop(0, nch, step=2)
        def _(i):
            for b in range(2):
                j = i + b
                gather(j, b).wait()

                @pl.when(j + 1 < nch)
                def _():
                    @pl.when(j >= 1)
                    def _():
                        write(j - 1, 1 - b).wait()
                    gather(j + 1, 1 - b).start()

                write(j, b).start()

        write(nch - 2, 0).wait()
        write(nch - 1, 1).wait()

    return gather_kernel(table, idx.reshape(nw, nch, ch))


def kernel(x, c, w_ada, b_ada, w_in, q_norm_g, k_norm_g, w_fourier, w_attn_o, w_out, ln1_g, ln1_b,
           w_router, b_router, w_gate_up, b_gate_up, w_down, b_down, ln2_g, ln2_b):
    bsz, s, d = x.shape
    n = bsz * s
    tm = min(ROW_TILE, s)
    l = 0

    mod = _ada(c, w_ada[l], b_ada[l])
    sh1, sc1, g1, sh2, sc2, g2 = [m[:, None, :] for m in jnp.split(mod, N_MOD, axis=-1)]

    dft_c, m1, m2, cw, sw, p, q = _dft_tables(s)
    cos, sin_a, sin_b = _rope_tables(s)

    fa, fb, qh, kh, vt, gf, ga = _inproj(
        x, sc1, sh1, w_in[l].astype(BF16), dft_c,
        q_norm_g[l].reshape(1, HEAD_DIM), k_norm_g[l].reshape(1, HEAD_DIM), cos, sin_a, sin_b, tm)

    zr, zi = _fft1(fa, fb, m1, cw, sw, p, q)
    yfm = _fft2(zr, zi, m2)

    k_bound = (math.sqrt(HEAD_DIM) * (1.0 + 2.0 ** -7)) * jnp.max(jnp.abs(k_norm_g[l])).reshape(1, 1)
    attn = _attention(qh, kh, vt, k_bound, min(ATTN_TQ, s))

    wf_b, wo_b, wout_b = w_fourier[l].astype(BF16), w_attn_o[l].astype(BF16), w_out[l].astype(BF16)
    bgu, bd = b_gate_up[l][:, None, :], b_down[l][:, None, :]
    tmb = MOE_BLOCK
    eids = jnp.arange(N_EXPERTS, dtype=jnp.int32)

    small = bsz // MOE_LAST_GROUP_DIVISOR if bsz % MOE_LAST_GROUP_DIVISOR == 0 else 0
    group_sizes = [bsz - small, small] if small else [bsz]
    out = None
    b0 = 0
    for gb in group_sizes:
        ng = gb * s
        x1, u2p, idx_t, wgt_t, rank_t, counts = _mix(
            yfm, attn, gf, ga, x, g1, sc2, sh2, wf_b, wo_b, wout_b,
            ln1_g[l].reshape(1, d), ln1_b[l].reshape(1, d),
            w_router[l].T, b_router[l].reshape(N_EXPERTS, 1), tm, b0, gb)

        counts = counts.reshape(N_EXPERTS)
        pcounts = ((counts + tmb - 1) // tmb) * tmb
        pends = jnp.cumsum(pcounts)
        pstarts = pends - pcounts
        n_pad = ng * TOP_K + N_EXPERTS * tmb
        nblk = n_pad // tmb
        start_of = jnp.sum(jnp.where(idx_t[None] == eids[:, None, None], pstarts[:, None, None], 0), axis=0)
        dest_t = start_of + rank_t
        block_starts = jnp.arange(nblk, dtype=jnp.int32) * tmb
        block_e = jnp.sum((block_starts[:, None] >= pends[None, :]).astype(jnp.int32), axis=1)
        block_e = jnp.minimum(block_e, N_EXPERTS - 1)
        valid_end = jnp.sum(jnp.where(block_e[:, None] == eids[None, :], (pstarts + counts)[None, :], 0), axis=1)
        block_nvalid = jnp.clip(valid_end - block_starts, 0, tmb)
        block_new = jnp.concatenate([jnp.ones((1,), jnp.int32),
                                     (block_e[1:] != block_e[:-1]).astype(jnp.int32)])

        xs = _scatter_rows(u2p.reshape(ng, d // 2), dest_t, n_pad)
        ys = _experts(block_e, block_nvalid, block_new, xs, w_gate_up[l], bgu, w_down[l], bd, tmb)
        yg = _gather_rows(ys, dest_t.reshape(-1)).reshape(TOP_K, ng, d // 2)
        out = _final(yg, wgt_t.T, x1, g2, ln2_g[l].reshape(1, d), ln2_b[l].reshape(1, d), tm,
                     b0, bsz, out)
        b0 += gb
    return out
```

```python
import functools
import math

import jax
import jax.numpy as jnp
import numpy as np
from jax import lax
from jax.experimental import pallas as pl
from jax.experimental.pallas import tpu as pltpu
from jax.experimental.pallas import tpu_sc as plsc

F32 = jnp.float32
BF16 = jnp.bfloat16

D_MODEL = 1024
GRID_W = 64
N_Q_HEADS = 8
N_KV_HEADS = 2
HEAD_DIM = 128
Q_GROUP = N_Q_HEADS // N_KV_HEADS
ROPE_THETA = 10000.0
ROPE_AXIS_DIM = HEAD_DIM // 2
N_FOURIER_GROUPS = 4
FOURIER_GROUP_DIM = 128
FOURIER_DIM = N_FOURIER_GROUPS * FOURIER_GROUP_DIM
Q_DIM = N_Q_HEADS * HEAD_DIM
KV_DIM = N_KV_HEADS * HEAD_DIM
N_EXPERTS = 32
TOP_K = 4
D_EXPERT = 1024
SWIGLU_LIMIT = 7.0
SWIGLU_ALPHA = 1.702
N_MOD = 6
DEPTH = 1
DEEPNORM_ALPHA = (2 * DEPTH) ** 0.25
LN_EPS = 1e-5
ADA_EPS = 1e-6
QK_EPS = 1e-6
LOG2E = 1.4426950408889634

V7X_VMEM_BYTES = 64 * 1024 * 1024
VMEM_LIMIT = V7X_VMEM_BYTES - 8 * 1024 * 1024
LANES = 128
BF16_SUBLANES = 16

FFT_Q = 128
ROW_TILE = 512
ATTN_TQ = 1024
ATTN_FAST_KCHUNKS = 2
DENOMINATOR_GUARD = 2.0 ** -80
INPROJ_ROW_GROUPS = 2
MIX_ROW_GROUPS = 2
MOE_BLOCK = 512
MOE_BATCH_GROUPS = 2

SC_CORES = 2
SC_SUBCORES = 16
GATHER_CHUNK = 64


def _cparams(sem):
    return pltpu.CompilerParams(dimension_semantics=sem, vmem_limit_bytes=VMEM_LIMIT)


def _layer_norm(x, eps):
    mu = jnp.mean(x, axis=-1, keepdims=True)
    xc = x - mu
    var = jnp.mean(xc * xc, axis=-1, keepdims=True)
    return xc * lax.rsqrt(var + eps)


def _pack_bf16_pairs(x):
    n = x.shape[1] // 2
    bits = pltpu.bitcast(x.astype(BF16).astype(F32), jnp.uint32)
    word = (bits[:, :n] >> 16) | (bits[:, n:] & jnp.uint32(0xFFFF0000))
    return pltpu.bitcast(word, jnp.int32)


def _unpack_bf16_pairs(w):
    bits = pltpu.bitcast(w, jnp.uint32)
    lo = pltpu.bitcast(bits << 16, F32)
    hi = pltpu.bitcast(bits & jnp.uint32(0xFFFF0000), F32)
    return lo, hi


def _ada_kernel(c_ref, w_ref, b_ref, o_ref):
    c = c_ref[...]
    cond = c * jax.nn.sigmoid(c)
    o_ref[...] = jnp.dot(cond, w_ref[...], preferred_element_type=F32,
                         precision=lax.Precision.HIGHEST) + b_ref[...]


def _ada(c, w_ada, b_ada):
    bsz, d = c.shape
    n = w_ada.shape[1]
    tn = 1536
    return pl.pallas_call(
        _ada_kernel,
        grid=(n // tn,),
        in_specs=[pl.BlockSpec((bsz, d), lambda j: (0, 0)),
                  pl.BlockSpec((d, tn), lambda j: (0, j)),
                  pl.BlockSpec((1, tn), lambda j: (0, j))],
        out_specs=pl.BlockSpec((bsz, tn), lambda j: (0, j)),
        out_shape=jax.ShapeDtypeStruct((bsz, n), F32),
        compiler_params=_cparams(("arbitrary",)),
        name="ada",
    )(c, w_ada, b_ada.reshape(1, n))


def _rope(xh, cos, sin_a, sin_b):
    a = pltpu.roll(xh, 32, 1)
    b = pltpu.roll(xh, 96, 1)
    return xh * cos + b * sin_b + a * sin_a


def _inproj_kernel(x_ref, sc_ref, sh_ref, w_ref, dft_ref, qg_ref, kg_ref,
                   cos_ref, sa_ref, sb_ref,
                   fa_ref, fb_ref, q_ref, k_ref, vt_ref, gf_ref, ga_ref):
    tm = x_ref.shape[0]
    rg = tm // INPROJ_ROW_GROUPS
    q_scale = LOG2E / math.sqrt(HEAD_DIM)
    for i in range(INPROJ_ROW_GROUPS):
        rows = slice(i * rg, (i + 1) * rg)
        u = _layer_norm(x_ref[rows, :], ADA_EPS) * (1.0 + sc_ref[...]) + sh_ref[...]
        ub = u.astype(BF16)
        cos, sa, sb = cos_ref[rows, :], sa_ref[rows, :], sb_ref[rows, :]

        def norm_rope(z, gain, scale):
            ms = jnp.mean(z * z, axis=-1, keepdims=True)
            zn = z * lax.rsqrt(ms + QK_EPS) * gain
            return (_rope(zn, cos, sa, sb) * scale).astype(BF16)

        c0 = 0
        zf = jnp.dot(ub, w_ref[:, c0:c0 + FOURIER_DIM], preferred_element_type=F32).astype(BF16)
        for g in range(N_FOURIER_GROUPS):
            sl = slice(g * FOURIER_GROUP_DIM, (g + 1) * FOURIER_GROUP_DIM)
            ab = jnp.dot(zf[:, sl], dft_ref[...], preferred_element_type=F32)
            fa_ref[rows, sl] = ab[:, :FOURIER_GROUP_DIM].astype(BF16)
            fb_ref[rows, sl] = ab[:, FOURIER_GROUP_DIM:].astype(BF16)
        c0 += FOURIER_DIM

        zq = jnp.dot(ub, w_ref[:, c0:c0 + Q_DIM], preferred_element_type=F32)
        for h in range(N_Q_HEADS):
            sl = slice(h * HEAD_DIM, (h + 1) * HEAD_DIM)
            q_ref[rows, sl] = norm_rope(zq[:, sl], qg_ref[...], q_scale)
        c0 += Q_DIM

        zk = jnp.dot(ub, w_ref[:, c0:c0 + KV_DIM], preferred_element_type=F32)
        for h in range(N_KV_HEADS):
            sl = slice(h * HEAD_DIM, (h + 1) * HEAD_DIM)
            k_ref[rows, sl] = norm_rope(zk[:, sl], kg_ref[...], 1.0)
        c0 += KV_DIM

        zv = jnp.dot(ub, w_ref[:, c0:c0 + KV_DIM], preferred_element_type=F32)
        for h in range(N_KV_HEADS):
            vt_ref[h, :, rows] = zv[:, h * HEAD_DIM:(h + 1) * HEAD_DIM].T.astype(BF16)
        c0 += KV_DIM

        zg = jnp.dot(ub, w_ref[:, c0:c0 + D_MODEL], preferred_element_type=F32)
        gf_ref[rows, :] = jax.nn.sigmoid(zg).astype(BF16)
        c0 += D_MODEL
        zg = jnp.dot(ub, w_ref[:, c0:c0 + D_MODEL], preferred_element_type=F32)
        ga_ref[rows, :] = jax.nn.sigmoid(zg).astype(BF16)


def _inproj(x, sc1, sh1, w_in, dft_c, qg, kg, cos, sin_a, sin_b, tm):
    bsz, s, d = x.shape
    ncol = w_in.shape[1]
    row = lambda w: pl.BlockSpec((None, tm, w), lambda b, i: (b, i, 0))
    mod = pl.BlockSpec((None, 1, d), lambda b, i: (b, 0, 0))
    full = lambda shp: pl.BlockSpec(shp, lambda b, i: (0,) * len(shp))
    tab = pl.BlockSpec((tm, HEAD_DIM), lambda b, i: (i, 0))
    vt_rows = HEAD_DIM
    vt_spec = pl.BlockSpec((None, N_KV_HEADS, None, vt_rows, tm), lambda b, i: (b, 0, i, 0, 0))
    vt_shape = jax.ShapeDtypeStruct((bsz, N_KV_HEADS, s // tm, vt_rows, tm), BF16)
    rows_out = lambda w: (row(w), jax.ShapeDtypeStruct((bsz, s, w), BF16))
    outs = [rows_out(FOURIER_DIM), rows_out(FOURIER_DIM), rows_out(Q_DIM), rows_out(KV_DIM),
            (vt_spec, vt_shape), rows_out(D_MODEL), rows_out(D_MODEL)]
    return pl.pallas_call(
        _inproj_kernel,
        grid=(bsz, s // tm),
        in_specs=[row(d), mod, mod, full((d, ncol)), full(dft_c.shape),
                  full((1, HEAD_DIM)), full((1, HEAD_DIM)), tab, tab, tab],
        out_specs=[o[0] for o in outs],
        out_shape=[o[1] for o in outs],
        compiler_params=_cparams(("parallel", "parallel")),
        name="inproj",
    )(x, sc1, sh1, w_in, dft_c, qg, kg, cos, sin_a, sin_b)


def _fft1_kernel(a_ref, b_ref, m_ref, cw_ref, sw_ref, zr_ref, zi_ref, a32, b32):
    p, nb, _ = a_ref.shape
    a32[...] = a_ref[...].astype(F32)
    b32[...] = b_ref[...].astype(F32)
    for j in range(nb):
        ab = jnp.concatenate([a32[:, j, :], b32[:, j, :]], axis=0).astype(BF16)
        t = jnp.dot(m_ref[...], ab, preferred_element_type=F32)
        tr, ti = t[:p], t[p:]
        cw, sw = cw_ref[j], sw_ref[j]
        zr_ref[j] = (tr * cw - ti * sw).astype(BF16)
        zi_ref[j] = (ti * cw + tr * sw).astype(BF16)


def _fft1(fa, fb, m1, cw, sw, p, q):
    bsz, s, c = fa.shape
    nb = BF16_SUBLANES
    inp = pl.BlockSpec((None, p, nb, c), lambda b, j: (b, 0, j, 0))
    tw = pl.BlockSpec((nb, p, 1), lambda b, j: (j, 0, 0))
    out = pl.BlockSpec((None, nb, p, c), lambda b, j: (b, j, 0, 0))
    return pl.pallas_call(
        _fft1_kernel,
        grid=(bsz, q // nb),
        in_specs=[inp, inp, pl.BlockSpec(m1.shape, lambda b, j: (0, 0)), tw, tw],
        out_specs=[out, out],
        out_shape=[jax.ShapeDtypeStruct((bsz, q, p, c), BF16)] * 2,
        scratch_shapes=[pltpu.VMEM((p, nb, c), F32)] * 2,
        compiler_params=_cparams(("parallel", "parallel")),
        name="fft1",
    )(fa.reshape(bsz, p, q, c), fb.reshape(bsz, p, q, c), m1, cw, sw)


def _fft2_kernel(zr_ref, zi_ref, m_ref, y_ref, zr32, zi32, y32):
    zr32[...] = zr_ref[...].astype(F32)
    zi32[...] = zi_ref[...].astype(F32)
    for j in range(zr_ref.shape[1]):
        z = jnp.concatenate([zr32[:, j, :], zi32[:, j, :]], axis=0).astype(BF16)
        y32[:, j, :] = jnp.dot(m_ref[...], z, preferred_element_type=F32)
    y_ref[...] = y32[...].astype(BF16)


def _fft2(zr, zi, m2):
    bsz, q, p, c = zr.shape
    pc = BF16_SUBLANES
    blk = pl.BlockSpec((None, q, pc, c), lambda b, j: (b, 0, j, 0))
    y = pl.pallas_call(
        _fft2_kernel,
        grid=(bsz, p // pc),
        in_specs=[blk, blk, pl.BlockSpec(m2.shape, lambda b, j: (0, 0))],
        out_specs=blk,
        out_shape=jax.ShapeDtypeStruct((bsz, q, p, c), BF16),
        scratch_shapes=[pltpu.VMEM((q, pc, c), F32)] * 3,
        compiler_params=_cparams(("parallel", "parallel")),
        name="fft2",
    )(zr, zi, m2)
    return y.reshape(bsz, q * p, c)


def _dft_tables(s):
    q = FFT_Q
    p = s // q

    def cs(n_rows, n_cols, n):
        r = np.arange(n_rows, dtype=np.int64)[:, None]
        c = np.arange(n_cols, dtype=np.int64)[None, :]
        ang = ((r * c) % n).astype(np.float64) * (2.0 * math.pi / n)
        return np.cos(ang).astype(np.float32), np.sin(ang).astype(np.float32)

    cc, sc = cs(FOURIER_GROUP_DIM, FOURIER_GROUP_DIM, FOURIER_GROUP_DIM)
    dft_c = jnp.asarray(np.concatenate([cc, sc], axis=1), dtype=BF16)
    cp, sp = cs(p, p, p)
    m1 = jnp.asarray(np.block([[cp, -sp], [sp, cp]]), dtype=BF16)
    cq, sq = cs(q, q, q)
    scale = np.float32(1.0 / math.sqrt(s * FOURIER_GROUP_DIM))
    m2 = jnp.asarray(np.concatenate([cq, -sq], axis=1) * scale, dtype=BF16)
    cw, sw = cs(q, p, s)
    return dft_c, m1, m2, jnp.asarray(cw[:, :, None]), jnp.asarray(sw[:, :, None]), p, q


def _rope_tables(s):
    f32 = np.float32
    rows = s // GRID_W
    row_ids = np.repeat(np.arange(rows, dtype=f32), GRID_W)
    col_ids = np.tile(np.arange(GRID_W, dtype=f32), rows)
    freqs = (f32(ROPE_THETA) ** (-np.arange(0, ROPE_AXIS_DIM, 2, dtype=f32) / f32(ROPE_AXIS_DIM))).astype(f32)
    ang_r = (row_ids[:, None] * freqs).astype(f32)
    ang_c = (col_ids[:, None] * freqs).astype(f32)
    cr, sr, cc, sc = np.cos(ang_r), np.sin(ang_r), np.cos(ang_c), np.sin(ang_c)
    z = np.zeros_like(sr)
    cos = np.concatenate([cr, cr, cc, cc], axis=1)
    sin_b = np.concatenate([-sr, z, -sc, z], axis=1)
    sin_a = np.concatenate([z, sr, z, sc], axis=1)
    return jnp.asarray(cos, F32), jnp.asarray(sin_a, F32), jnp.asarray(sin_b, F32)


def _attn_kernel(kb_ref, q_ref, k_ref, vt_ref, o_ref, m_ref, acc_ref, al0, al1, s0, s1, p0, p1,
                 qt_ref, mx0, mx1, qa_ref, l_ref, ps0, ps1):
    al_ref, s_ref, p_ref, mx_ref, ps_ref = (al0, al1), (s0, s1), (p0, p1), (mx0, mx1), (ps0, ps1)
    nchunk = vt_ref.shape[0]
    tk = vt_ref.shape[2]
    tq = q_ref.shape[0]
    heads = range(Q_GROUP)

    def k_chunk(c):
        return k_ref[pl.ds(pl.multiple_of(c * tk, tk), tk), :]

    def write_out():
        for g in heads:
            out_t = acc_ref[g] / l_ref[g]
            o_ref[:, g * HEAD_DIM:(g + 1) * HEAD_DIM] = out_t.T.astype(BF16)

    row0 = lax.broadcasted_iota(jnp.int32, (HEAD_DIM, tq), 0) == 0
    for g in heads:
        qt = q_ref[:, g * HEAD_DIM:(g + 1) * HEAD_DIM].astype(F32).T
        bound = jnp.sqrt(jnp.sum(qt * qt, axis=0, keepdims=True)) * kb_ref[...]
        qa_ref[g, :HEAD_DIM, :] = qt.astype(BF16)
        qa_ref[g, HEAD_DIM:, :] = jnp.where(row0, -bound, 0.0).astype(BF16)
    kf = ATTN_FAST_KCHUNKS
    tkf = kf * tk
    nfast = nchunk // kf
    ones_col = (lax.broadcasted_iota(jnp.int32, (tkf, HEAD_DIM), 1) == 0).astype(BF16)

    def probs_into(c, slot):
        kc = k_ref[pl.ds(pl.multiple_of(c * tkf, tkf), tkf), :]
        ka = jnp.concatenate([kc, ones_col], axis=1)
        for g in heads:
            p32 = jnp.exp2(jnp.dot(ka, qa_ref[g], preferred_element_type=F32))
            l_ref[g] += jnp.sum(p32, axis=0, keepdims=True)
            p_ref[slot][g] = p32.astype(BF16)

    def pv_add(c, slot):
        vc = jnp.concatenate([vt_ref[kf * c + j] for j in range(kf)], axis=1)
        for g in heads:
            acc_ref[g] += jnp.dot(vc, p_ref[slot][g], preferred_element_type=F32)

    acc_ref[...] = jnp.zeros(acc_ref.shape, F32)
    l_ref[...] = jnp.zeros(l_ref.shape, F32)
    probs_into(0, 0)

    def fast_pair(i, carry):
        probs_into(2 * i + 1, 1)
        pv_add(2 * i, 0)
        probs_into(2 * i + 2, 0)
        pv_add(2 * i + 1, 1)
        return carry

    lax.fori_loop(0, (nfast - 2) // 2, fast_pair, 0)
    probs_into(nfast - 1, 1)
    pv_add(nfast - 2, 0)
    pv_add(nfast - 1, 1)
    write_out()
    den_min = l_ref[0]
    for g in heads[1:]:
        den_min = jnp.minimum(den_min, l_ref[g])
    safe = jnp.min(den_min) > DENOMINATOR_GUARD

    def scores_into(c, slot):
        kc = k_chunk(c)
        for g in heads:
            sc = jnp.dot(kc, qt_ref[g], preferred_element_type=F32)
            s_ref[slot][g] = sc
            parts = [jnp.max(sc[r * (tk // 4):(r + 1) * (tk // 4)], axis=0, keepdims=True) for r in range(4)]
            mx_ref[slot][g] = jnp.maximum(jnp.maximum(parts[0], parts[1]), jnp.maximum(parts[2], parts[3]))

    def softmax(slot):
        for g in heads:
            m_old = m_ref[g]
            m_new = jnp.maximum(m_old, mx_ref[slot][g])
            al_ref[slot][g] = jnp.exp2(m_old - m_new)
            m_ref[g] = m_new
            p32 = jnp.exp2(s_ref[slot][g] - m_new)
            ps_ref[slot][g] = jnp.sum(p32, axis=0, keepdims=True)
            p_ref[slot][g, :tk, :] = p32.astype(BF16)

    def pv_update(c, slot):
        vc = vt_ref[c]
        for g in heads:
            acc_ref[g] = al_ref[slot][g] * acc_ref[g] + jnp.dot(vc, p_ref[slot][g, :tk, :],
                                                                preferred_element_type=F32)
            l_ref[g] = al_ref[slot][g] * l_ref[g] + ps_ref[slot][g]

    def step(c, slot):
        scores_into(c + 1, 1 - slot)
        pv_update(c - 1, 1 - slot)
        softmax(slot)

    @pl.when(jnp.logical_not(safe))
    def _():
        for g in heads:
            qt_ref[g] = qa_ref[g, :HEAD_DIM, :]
        scores_into(0, 0)
        m_ref[...] = jnp.full(m_ref.shape, -jnp.inf, F32)
        acc_ref[...] = jnp.zeros(acc_ref.shape, F32)
        l_ref[...] = jnp.zeros(l_ref.shape, F32)
        softmax(0)
        scores_into(1, 1)

        def pair(i, carry):
            step(2 * i + 1, 1)
            step(2 * i + 2, 0)
            return carry

        lax.fori_loop(0, (nchunk - 2) // 2, pair, 0)
        pv_update(nchunk - 2, 0)
        softmax(1)
        pv_update(nchunk - 1, 1)
        write_out()


def _attention(q, k, vt, k_norm_bound, tq):
    bsz, s, _ = q.shape
    rows, tk = vt.shape[3], vt.shape[4]
    nchunk = s // tk
    assert nchunk % (2 * ATTN_FAST_KCHUNKS) == 0 and nchunk >= 4 * ATTN_FAST_KCHUNKS
    gw = Q_GROUP * HEAD_DIM
    qspec = pl.BlockSpec((None, tq, gw), lambda b, h, i: (b, i, h))
    kspec = pl.BlockSpec((None, s, HEAD_DIM), lambda b, h, i: (b, 0, h))
    vspec = pl.BlockSpec((None, None, nchunk, rows, tk), lambda b, h, i: (b, h, 0, 0, 0))
    head_vec = pltpu.VMEM((Q_GROUP, 1, tq), F32)
    return pl.pallas_call(
        _attn_kernel,
        grid=(bsz, N_KV_HEADS, s // tq),
        in_specs=[pl.BlockSpec((1, 1), lambda b, h, i: (0, 0)), qspec, kspec, vspec],
        out_specs=qspec,
        out_shape=jax.ShapeDtypeStruct((bsz, s, Q_DIM), BF16),
        scratch_shapes=[head_vec, pltpu.VMEM((Q_GROUP, rows, tq), F32), head_vec, head_vec]
        + [pltpu.VMEM((Q_GROUP, tk, tq), F32)] * 2
        + [pltpu.VMEM((Q_GROUP, ATTN_FAST_KCHUNKS * tk, tq), BF16)] * 2
        + [pltpu.VMEM((Q_GROUP, HEAD_DIM, tq), BF16), head_vec, head_vec,
           pltpu.VMEM((Q_GROUP, 2 * HEAD_DIM, tq), BF16), head_vec, head_vec, head_vec],
        compiler_params=_cparams(("parallel", "parallel", "parallel")),
        name="attn",
    )(k_norm_bound, q, k, vt)


def _mix_kernel(yf_ref, at_ref, gf_ref, ga_ref, x_ref, g1_ref, sc2_ref, sh2_ref,
                wf_ref, wo_ref, wout_ref, l1g_ref, l1b_ref, wr_ref, br_ref, tri_ref,
                x1_ref, u2_ref, idx_ref, wgt_ref, rank_ref, cnt_ref, cnt_acc):
    first = jnp.logical_and(pl.program_id(0) == 0, pl.program_id(1) == 0)

    @pl.when(first)
    def _():
        cnt_acc[...] = jnp.zeros_like(cnt_acc)

    tm_all = x_ref.shape[0]
    rg = tm_all // MIX_ROW_GROUPS
    u_his, u_los = [], []
    groups = [slice(i * rg, (i + 1) * rg) for i in range(MIX_ROW_GROUPS)]
    ms = []
    for rows in groups:
        yf = jnp.dot(yf_ref[rows, :], wf_ref[...], preferred_element_type=F32)
        ya = jnp.dot(at_ref[rows, :], wo_ref[...], preferred_element_type=F32)
        ms.append((gf_ref[rows, :].astype(F32) * yf + ga_ref[rows, :].astype(F32) * ya).astype(BF16))
    hs = [jnp.dot(m, wout_ref[...], preferred_element_type=F32) for m in ms]
    for rows, h in zip(groups, hs):
        r = DEEPNORM_ALPHA * x_ref[rows, :] + g1_ref[...] * h
        x1 = _layer_norm(r, LN_EPS) * l1g_ref[...] + l1b_ref[...]
        x1_ref[rows, :] = x1
        u2 = _layer_norm(x1, ADA_EPS) * (1.0 + sc2_ref[...]) + sh2_ref[...]
        u2_ref[rows, :] = _pack_bf16_pairs(u2)
        u_hi = u2.astype(BF16)
        u_his.append(u_hi)
        u_los.append((u2 - u_hi.astype(F32)).astype(BF16))

    nt = (((1,), (1,)), ((), ()))
    u_hi = jnp.concatenate(u_his, axis=0)
    u_lo = jnp.concatenate(u_los, axis=0)
    w = wr_ref[...]
    w_hi = w.astype(BF16)
    w_lo = (w - w_hi.astype(F32)).astype(BF16)
    hh_lh = lax.dot_general(jnp.concatenate([w_hi, w_lo], axis=0), u_hi, nt, preferred_element_type=F32)
    logits = (hh_lh[:N_EXPERTS] + hh_lh[N_EXPERTS:]
              + lax.dot_general(w_hi, u_lo, nt, preferred_element_type=F32)) + br_ref[...]

    ne, tm = logits.shape
    eid = lax.broadcasted_iota(jnp.int32, (ne, tm), 0)
    work = logits
    vals, idxs, hots = [], [], []
    for _ in range(TOP_K):
        mx = jnp.max(work, axis=0, keepdims=True)
        ix = jnp.min(jnp.where(work == mx, eid, ne), axis=0, keepdims=True)
        hot = eid == ix
        work = jnp.where(hot, -jnp.inf, work)
        vals.append(mx)
        idxs.append(ix)
        hots.append(hot)
    ex = [jnp.exp(v - vals[0]) for v in vals]
    den = ex[0] + ex[1] + ex[2] + ex[3]
    wgt_ref[...] = jnp.concatenate([e / den for e in ex], axis=0)
    idx_ref[...] = jnp.concatenate(idxs, axis=0)

    sel = hots[0] | hots[1] | hots[2] | hots[3]
    mask = jnp.where(sel, 1.0, 0.0)
    prefix = jnp.dot(mask.astype(BF16), tri_ref[...], preferred_element_type=F32)
    pos = prefix + cnt_acc[...]
    ranks = [jnp.sum(jnp.where(hot, pos, 0.0), axis=0, keepdims=True) for hot in hots]
    rank_ref[...] = jnp.concatenate(ranks, axis=0).astype(jnp.int32)
    cnt_acc[...] += jnp.sum(mask, axis=1, keepdims=True)
    cnt_ref[...] = cnt_acc[...].astype(jnp.int32)


def _mix(yfm, attn, gf, ga, x, g1, sc2, sh2, wf, wo, wout, l1g, l1b, wr_t, br, tm, b0, bsz):
    _, s, d = x.shape
    n = bsz * s
    nt = s // tm
    tri = (jnp.arange(tm)[:, None] < jnp.arange(tm)[None, :]).astype(BF16)
    row_in = lambda w: pl.BlockSpec((None, tm, w), lambda b, i: (b + b0, i, 0))
    row = lambda w: pl.BlockSpec((None, tm, w), lambda b, i: (b, i, 0))
    mod = pl.BlockSpec((None, 1, d), lambda b, i: (b + b0, 0, 0))
    full = lambda shp: pl.BlockSpec(shp, lambda b, i: (0,) * len(shp))
    tok = pl.BlockSpec((TOP_K, tm), lambda b, i: (0, b * nt + i))
    return pl.pallas_call(
        _mix_kernel,
        grid=(bsz, nt),
        in_specs=[row_in(FOURIER_DIM), row_in(Q_DIM), row_in(d), row_in(d), row_in(d), mod, mod, mod,
                  full(wf.shape), full(wo.shape), full(wout.shape), full((1, d)), full((1, d)),
                  full(wr_t.shape), full((N_EXPERTS, 1)), full((tm, tm))],
        out_specs=[row(d), row(d // 2), tok, tok, tok, full((N_EXPERTS, 1))],
        out_shape=[jax.ShapeDtypeStruct((bsz, s, d), F32),
                   jax.ShapeDtypeStruct((bsz, s, d // 2), jnp.int32),
                   jax.ShapeDtypeStruct((TOP_K, n), jnp.int32),
                   jax.ShapeDtypeStruct((TOP_K, n), F32),
                   jax.ShapeDtypeStruct((TOP_K, n), jnp.int32),
                   jax.ShapeDtypeStruct((N_EXPERTS, 1), jnp.int32)],
        scratch_shapes=[pltpu.VMEM((N_EXPERTS, 1), F32)],
        compiler_params=_cparams(("arbitrary", "arbitrary")),
        name="mix",
    )(yfm, attn, gf, ga, x, g1, sc2, sh2, wf, wo, wout, l1g, l1b, wr_t, br, tri)


def _expert_kernel(be_ref, nv_ref, new_ref, xs_ref, wgu_ref, bgu_ref, wd_ref, bd_ref, ys_ref,
                   wgu_bf, wd_bf):
    del be_ref
    i = pl.program_id(0)
    nvalid = nv_ref[i]

    @pl.when(new_ref[i] == 1)
    def _():
        wgu_bf[...] = wgu_ref[...].astype(BF16)
        wd_bf[...] = wd_ref[...].astype(BF16)

    @pl.when(nvalid > 0)
    def _():
        xw = xs_ref[...]
        row = lax.broadcasted_iota(jnp.int32, xw.shape, 0)
        lo, hi = _unpack_bf16_pairs(jnp.where(row < nvalid, xw, 0))
        xb = jnp.concatenate([lo.astype(BF16), hi.astype(BF16)], axis=1)
        h = jnp.dot(xb, wgu_bf[...], preferred_element_type=F32) + bgu_ref[...]
        gate = jnp.minimum(h[:, :D_EXPERT], SWIGLU_LIMIT)
        up = jnp.clip(h[:, D_EXPERT:], -SWIGLU_LIMIT, SWIGLU_LIMIT)
        act = (up + 1.0) * gate * jax.nn.sigmoid(SWIGLU_ALPHA * gate)
        y = jnp.dot(act.astype(BF16), wd_bf[...], preferred_element_type=F32) + bd_ref[...]
        ys_ref[...] = _pack_bf16_pairs(y)

    @pl.when(nvalid == 0)
    def _():
        ys_ref[...] = jnp.zeros_like(ys_ref)


def _experts(block_e, block_nvalid, block_new, xs, wgu, bgu, wd, bd, tmb):
    n_pad, dh = xs.shape
    d = 2 * dh
    grid_spec = pltpu.PrefetchScalarGridSpec(
        num_scalar_prefetch=3,
        grid=(n_pad // tmb,),
        in_specs=[pl.BlockSpec((tmb, dh), lambda i, be, nv, nw: (i, 0)),
                  pl.BlockSpec((None, d, 2 * D_EXPERT), lambda i, be, nv, nw: (be[i], 0, 0)),
                  pl.BlockSpec((None, 1, 2 * D_EXPERT), lambda i, be, nv, nw: (be[i], 0, 0)),
                  pl.BlockSpec((None, D_EXPERT, d), lambda i, be, nv, nw: (be[i], 0, 0)),
                  pl.BlockSpec((None, 1, d), lambda i, be, nv, nw: (be[i], 0, 0))],
        out_specs=pl.BlockSpec((tmb, dh), lambda i, be, nv, nw: (i, 0)),
        scratch_shapes=[pltpu.VMEM((d, 2 * D_EXPERT), BF16), pltpu.VMEM((D_EXPERT, d), BF16)],
    )
    return pl.pallas_call(
        _expert_kernel,
        grid_spec=grid_spec,
        out_shape=jax.ShapeDtypeStruct((n_pad, dh), jnp.int32),
        compiler_params=_cparams(("arbitrary",)),
        name="experts",
    )(block_e, block_nvalid, block_new, xs, wgu, bgu, wd, bd)


def _final_kernel(yg_ref, w_ref, x1_ref, g2_ref, lg_ref, lb_ref, o_ref):
    w = w_ref[...]
    acc_lo = None
    for j in range(TOP_K):
        lo, hi = _unpack_bf16_pairs(yg_ref[j])
        wj = w[:, j:j + 1]
        acc_lo = lo * wj if acc_lo is None else acc_lo + lo * wj
        acc_hi = hi * wj if j == 0 else acc_hi + hi * wj
    h = jnp.concatenate([acc_lo, acc_hi], axis=1)
    r = DEEPNORM_ALPHA * x1_ref[...] + g2_ref[...] * h
    o_ref[...] = _layer_norm(r, LN_EPS) * lg_ref[...] + lb_ref[...]


def _final(yg, w_tok, x1, g2, lg, lb, tm, b0, total_bsz, out_so_far):
    bsz, s, d = x1.shape
    nt = s // tm
    row = pl.BlockSpec((None, tm, d), lambda b, i: (b, i, 0))
    full = pl.BlockSpec((1, d), lambda b, i: (0, 0))
    in_specs = [pl.BlockSpec((TOP_K, tm, d // 2), lambda b, i: (0, b * nt + i, 0)),
                pl.BlockSpec((tm, TOP_K), lambda b, i: (b * nt + i, 0)),
                row, pl.BlockSpec((None, 1, d), lambda b, i: (b + b0, 0, 0)), full, full]
    args = [yg, w_tok, x1, g2, lg, lb]
    aliases = {}
    if out_so_far is not None:
        in_specs.append(pl.BlockSpec(memory_space=pl.ANY))
        args.append(out_so_far)
        aliases = {len(args) - 1: 0}
    body = _final_kernel if out_so_far is None else (lambda *refs: _final_kernel(*refs[:6], refs[7]))
    return pl.pallas_call(
        body,
        grid=(bsz, nt),
        in_specs=in_specs,
        out_specs=pl.BlockSpec((None, tm, d), lambda b, i: (b + b0, i, 0)),
        out_shape=jax.ShapeDtypeStruct((total_bsz, s, d), F32),
        input_output_aliases=aliases,
        compiler_params=_cparams(("parallel", "parallel")),
        name="final",
    )(*args)


def _scatter_rows(rows, dest, m_out):
    n, w = rows.shape
    nj = dest.shape[0]
    ch = GATHER_CHUNK
    nw = SC_CORES * SC_SUBCORES
    nch = n // (nw * ch)
    assert n == nw * nch * ch and nch % 2 == 0
    per_w = nch * ch
    mesh = plsc.VectorSubcoreMesh(core_axis_name="c", subcore_axis_name="s",
                                  num_cores=SC_CORES, num_subcores=SC_SUBCORES)

    @functools.partial(
        pl.kernel, mesh=mesh,
        out_type=jax.ShapeDtypeStruct((m_out, w), rows.dtype),
        scratch_types=[pltpu.VMEM((nj, nch, ch), jnp.int32), pltpu.VMEM((2, ch, w), rows.dtype),
                       pltpu.SemaphoreType.DMA((2,)), pltpu.SemaphoreType.DMA((2,))],
    )
    def scatter_kernel(rows_hbm, dest_hbm, out_hbm, idx_v, rows_v, rsem, ssem):
        wid = lax.axis_index("s") * SC_CORES + lax.axis_index("c")
        base = wid * per_w
        for j in range(nj):
            pltpu.sync_copy(dest_hbm.at[j, wid], idx_v.at[j])

        def read(i, b):
            off = pl.multiple_of(base + i * ch, ch)
            return pltpu.make_async_copy(rows_hbm.at[pl.ds(off, ch)], rows_v.at[b], rsem.at[b])

        def scatter(i, b, j):
            return pltpu.make_async_copy(rows_v.at[b], out_hbm.at[idx_v.at[j, i]], ssem.at[b])

        read(0, 0).start()

        @pl.loop(0, nch, step=2)
        def _(i0):
            for b in range(2):
                i = i0 + b
                read(i, b).wait()

                @pl.when(i + 1 < nch)
                def _():
                    @pl.when(i >= 1)
                    def _():
                        for j in range(nj):
                            scatter(i - 1, 1 - b, j).wait()
                    read(i + 1, 1 - b).start()

                for j in range(nj):
                    scatter(i, b, j).start()

        for b in range(2):
            for j in range(nj):
                scatter(nch - 2 + b, b, j).wait()

    return scatter_kernel(rows, dest.reshape(nj, nw, nch, ch))


def _gather_rows(table, idx):
    m = idx.shape[0]
    w = table.shape[1]
    ch = GATHER_CHUNK
    nw = SC_CORES * SC_SUBCORES
    nch = m // (nw * ch)
    assert m == nw * nch * ch and nch % 2 == 0
    per_w = nch * ch
    mesh = plsc.VectorSubcoreMesh(core_axis_name="c", subcore_axis_name="s",
                                  num_cores=SC_CORES, num_subcores=SC_SUBCORES)

    @functools.partial(
        pl.kernel, mesh=mesh,
        out_type=jax.ShapeDtypeStruct((m, w), table.dtype),
        scratch_types=[pltpu.VMEM((nch, ch), jnp.int32), pltpu.VMEM((2, ch, w), table.dtype),
                       pltpu.SemaphoreType.DMA((2,)), pltpu.SemaphoreType.DMA((2,))],
    )
    def gather_kernel(table_hbm, idx_hbm, out_hbm, idx_v, rows_v, gsem, wsem):
        wid = lax.axis_index("s") * SC_CORES + lax.axis_index("c")
        base = wid * per_w
        pltpu.sync_copy(idx_hbm.at[wid], idx_v)

        def gather(j, b):
            return pltpu.make_async_copy(table_hbm.at[idx_v.at[j]], rows_v.at[b], gsem.at[b])

        def write(j, b):
            off = pl.multiple_of(base + j * ch, ch)
            return pltpu.make_async_copy(rows_v.at[b], out_hbm.at[pl.ds(off, ch)], wsem.at[b])

        gather(0, 0).start()

        @pl.loop(0, nch, step=2)
        def _(i):
            for b in range(2):
                j = i + b
                gather(j, b).wait()

                @pl.when(j + 1 < nch)
                def _():
                    @pl.when(j >= 1)
                    def _():
                        write(j - 1, 1 - b).wait()
                    gather(j + 1, 1 - b).start()

                write(j, b).start()

        write(nch - 2, 0).wait()
        write(nch - 1, 1).wait()

    return gather_kernel(table, idx.reshape(nw, nch, ch))


def kernel(x, c, w_ada, b_ada, w_in, q_norm_g, k_norm_g, w_fourier, w_attn_o, w_out, ln1_g, ln1_b,
           w_router, b_router, w_gate_up, b_gate_up, w_down, b_down, ln2_g, ln2_b):
    bsz, s, d = x.shape
    n = bsz * s
    tm = min(ROW_TILE, s)
    l = 0

    mod = _ada(c, w_ada[l], b_ada[l])
    sh1, sc1, g1, sh2, sc2, g2 = [m[:, None, :] for m in jnp.split(mod, N_MOD, axis=-1)]

    dft_c, m1, m2, cw, sw, p, q = _dft_tables(s)
    cos, sin_a, sin_b = _rope_tables(s)

    fa, fb, qh, kh, vt, gf, ga = _inproj(
        x, sc1, sh1, w_in[l].astype(BF16), dft_c,
        q_norm_g[l].reshape(1, HEAD_DIM), k_norm_g[l].reshape(1, HEAD_DIM), cos, sin_a, sin_b, tm)

    zr, zi = _fft1(fa, fb, m1, cw, sw, p, q)
    yfm = _fft2(zr, zi, m2)

    k_bound = (math.sqrt(HEAD_DIM) * (1.0 + 2.0 ** -7)) * jnp.max(jnp.abs(k_norm_g[l])).reshape(1, 1)
    attn = _attention(qh, kh, vt, k_bound, min(ATTN_TQ, s))

    wf_b, wo_b, wout_b = w_fourier[l].astype(BF16), w_attn_o[l].astype(BF16), w_out[l].astype(BF16)
    bgu, bd = b_gate_up[l][:, None, :], b_down[l][:, None, :]
    tmb = MOE_BLOCK
    eids = jnp.arange(N_EXPERTS, dtype=jnp.int32)

    ngroups = MOE_BATCH_GROUPS if bsz % MOE_BATCH_GROUPS == 0 else 1
    gb = bsz // ngroups
    ng = gb * s
    out = None
    for grp in range(ngroups):
        b0 = grp * gb
        x1, u2p, idx_t, wgt_t, rank_t, counts = _mix(
            yfm, attn, gf, ga, x, g1, sc2, sh2, wf_b, wo_b, wout_b,
            ln1_g[l].reshape(1, d), ln1_b[l].reshape(1, d),
            w_router[l].T, b_router[l].reshape(N_EXPERTS, 1), tm, b0, gb)

        counts = counts.reshape(N_EXPERTS)
        pcounts = ((counts + tmb - 1) // tmb) * tmb
        pends = jnp.cumsum(pcounts)
        pstarts = pends - pcounts
        n_pad = ng * TOP_K + N_EXPERTS * tmb
        nblk = n_pad // tmb
        start_of = jnp.sum(jnp.where(idx_t[None] == eids[:, None, None], pstarts[:, None, None], 0), axis=0)
        dest_t = start_of + rank_t
        block_starts = jnp.arange(nblk, dtype=jnp.int32) * tmb
        block_e = jnp.sum((block_starts[:, None] >= pends[None, :]).astype(jnp.int32), axis=1)
        block_e = jnp.minimum(block_e, N_EXPERTS - 1)
        valid_end = jnp.sum(jnp.where(block_e[:, None] == eids[None, :], (pstarts + counts)[None, :], 0), axis=1)
        block_nvalid = jnp.clip(valid_end - block_starts, 0, tmb)
        block_new = jnp.concatenate([jnp.ones((1,), jnp.int32),
                                     (block_e[1:] != block_e[:-1]).astype(jnp.int32)])

        xs = _scatter_rows(u2p.reshape(ng, d // 2), dest_t, n_pad)
        ys = _experts(block_e, block_nvalid, block_new, xs, w_gate_up[l], bgu, w_down[l], bd, tmb)
        yg = _gather_rows(ys, dest_t.reshape(-1)).reshape(TOP_K, ng, d // 2)
        out = _final(yg, wgt_t.T, x1, g2, ln2_g[l].reshape(1, d), ln2_b[l].reshape(1, d), tm,
                     b0, bsz, out)
    return out
```

```python
import functools
import math

import jax
import jax.numpy as jnp
import numpy as np
from jax import lax
from jax.experimental import pallas as pl
from jax.experimental.pallas import tpu as pltpu
from jax.experimental.pallas import tpu_sc as plsc

F32 = jnp.float32
BF16 = jnp.bfloat16

D_MODEL = 1024
GRID_W = 64
N_Q_HEADS = 8
N_KV_HEADS = 2
HEAD_DIM = 128
Q_GROUP = N_Q_HEADS // N_KV_HEADS
ROPE_THETA = 10000.0
ROPE_AXIS_DIM = HEAD_DIM // 2
N_FOURIER_GROUPS = 4
FOURIER_GROUP_DIM = 128
FOURIER_DIM = N_FOURIER_GROUPS * FOURIER_GROUP_DIM
Q_DIM = N_Q_HEADS * HEAD_DIM
KV_DIM = N_KV_HEADS * HEAD_DIM
N_EXPERTS = 32
TOP_K = 4
D_EXPERT = 1024
SWIGLU_LIMIT = 7.0
SWIGLU_ALPHA = 1.702
N_MOD = 6
DEPTH = 1
DEEPNORM_ALPHA = (2 * DEPTH) ** 0.25
LN_EPS = 1e-5
ADA_EPS = 1e-6
QK_EPS = 1e-6
LOG2E = 1.4426950408889634

V7X_VMEM_BYTES = 64 * 1024 * 1024
VMEM_LIMIT = V7X_VMEM_BYTES - 8 * 1024 * 1024
LANES = 128
BF16_SUBLANES = 16

FFT_Q = 128
ROW_TILE = 512
ATTN_TQ = 1024
ATTN_FAST_KCHUNKS = 2
DENOMINATOR_GUARD = 2.0 ** -80
INPROJ_ROW_GROUPS = 2
MIX_ROW_GROUPS = 2
MOE_BLOCK = 512
MOE_BATCH_GROUPS = 2

SC_CORES = 2
SC_SUBCORES = 16
GATHER_CHUNK = 64


def _cparams(sem):
    return pltpu.CompilerParams(dimension_semantics=sem, vmem_limit_bytes=VMEM_LIMIT)


def _layer_norm(x, eps):
    mu = jnp.mean(x, axis=-1, keepdims=True)
    xc = x - mu
    var = jnp.mean(xc * xc, axis=-1, keepdims=True)
    return xc * lax.rsqrt(var + eps)


def _sigmoid(x):
    return 0.5 * jnp.tanh(0.5 * x) + 0.5


def _pack_bf16_pairs(x):
    n = x.shape[1] // 2
    bits = pltpu.bitcast(x.astype(BF16).astype(F32), jnp.uint32)
    word = (bits[:, :n] >> 16) | (bits[:, n:] & jnp.uint32(0xFFFF0000))
    return pltpu.bitcast(word, jnp.int32)


def _unpack_bf16_pairs(w):
    bits = pltpu.bitcast(w, jnp.uint32)
    lo = pltpu.bitcast(bits << 16, F32)
    hi = pltpu.bitcast(bits & jnp.uint32(0xFFFF0000), F32)
    return lo, hi


def _ada_kernel(c_ref, w_ref, b_ref, o_ref):
    c = c_ref[...]
    cond = c * jax.nn.sigmoid(c)
    o_ref[...] = jnp.dot(cond, w_ref[...], preferred_element_type=F32,
                         precision=lax.Precision.HIGHEST) + b_ref[...]


def _ada(c, w_ada, b_ada):
    bsz, d = c.shape
    n = w_ada.shape[1]
    tn = 1536
    return pl.pallas_call(
        _ada_kernel,
        grid=(n // tn,),
        in_specs=[pl.BlockSpec((bsz, d), lambda j: (0, 0)),
                  pl.BlockSpec((d, tn), lambda j: (0, j)),
                  pl.BlockSpec((1, tn), lambda j: (0, j))],
        out_specs=pl.BlockSpec((bsz, tn), lambda j: (0, j)),
        out_shape=jax.ShapeDtypeStruct((bsz, n), F32),
        compiler_params=_cparams(("arbitrary",)),
        name="ada",
    )(c, w_ada, b_ada.reshape(1, n))


def _rope(xh, cos, sin_a, sin_b):
    a = pltpu.roll(xh, 32, 1)
    b = pltpu.roll(xh, 96, 1)
    return xh * cos + b * sin_b + a * sin_a


def _inproj_kernel(x_ref, sc_ref, sh_ref, w_ref, dft_ref, qg_ref, kg_ref,
                   cos_ref, sa_ref, sb_ref,
                   fa_ref, fb_ref, q_ref, k_ref, vt_ref, gf_ref, ga_ref):
    tm = x_ref.shape[0]
    rg = tm // INPROJ_ROW_GROUPS
    q_scale = LOG2E / math.sqrt(HEAD_DIM)
    for i in range(INPROJ_ROW_GROUPS):
        rows = slice(i * rg, (i + 1) * rg)
        u = _layer_norm(x_ref[rows, :], ADA_EPS) * (1.0 + sc_ref[...]) + sh_ref[...]
        ub = u.astype(BF16)
        cos, sa, sb = cos_ref[rows, :], sa_ref[rows, :], sb_ref[rows, :]

        def norm_rope(z, gain, scale):
            ms = jnp.mean(z * z, axis=-1, keepdims=True)
            zn = z * lax.rsqrt(ms + QK_EPS) * gain
            return (_rope(zn, cos, sa, sb) * scale).astype(BF16)

        c0 = 0
        zf = jnp.dot(ub, w_ref[:, c0:c0 + FOURIER_DIM], preferred_element_type=F32).astype(BF16)
        for g in range(N_FOURIER_GROUPS):
            sl = slice(g * FOURIER_GROUP_DIM, (g + 1) * FOURIER_GROUP_DIM)
            ab = jnp.dot(zf[:, sl], dft_ref[...], preferred_element_type=F32)
            fa_ref[rows, sl] = ab[:, :FOURIER_GROUP_DIM].astype(BF16)
            fb_ref[rows, sl] = ab[:, FOURIER_GROUP_DIM:].astype(BF16)
        c0 += FOURIER_DIM

        zq = jnp.dot(ub, w_ref[:, c0:c0 + Q_DIM], preferred_element_type=F32)
        for h in range(N_Q_HEADS):
            sl = slice(h * HEAD_DIM, (h + 1) * HEAD_DIM)
            q_ref[rows, sl] = norm_rope(zq[:, sl], qg_ref[...], q_scale)
        c0 += Q_DIM

        zk = jnp.dot(ub, w_ref[:, c0:c0 + KV_DIM], preferred_element_type=F32)
        for h in range(N_KV_HEADS):
            sl = slice(h * HEAD_DIM, (h + 1) * HEAD_DIM)
            k_ref[rows, sl] = norm_rope(zk[:, sl], kg_ref[...], 1.0)
        c0 += KV_DIM

        zv = jnp.dot(ub, w_ref[:, c0:c0 + KV_DIM], preferred_element_type=F32)
        for h in range(N_KV_HEADS):
            vt_ref[h, :, rows] = zv[:, h * HEAD_DIM:(h + 1) * HEAD_DIM].T.astype(BF16)
        c0 += KV_DIM

        zg = jnp.dot(ub, w_ref[:, c0:c0 + D_MODEL], preferred_element_type=F32)
        gf_ref[rows, :] = _sigmoid(zg).astype(BF16)
        c0 += D_MODEL
        zg = jnp.dot(ub, w_ref[:, c0:c0 + D_MODEL], preferred_element_type=F32)
        ga_ref[rows, :] = _sigmoid(zg).astype(BF16)


def _inproj(x, sc1, sh1, w_in, dft_c, qg, kg, cos, sin_a, sin_b, tm):
    bsz, s, d = x.shape
    ncol = w_in.shape[1]
    row = lambda w: pl.BlockSpec((None, tm, w), lambda b, i: (b, i, 0))
    mod = pl.BlockSpec((None, 1, d), lambda b, i: (b, 0, 0))
    full = lambda shp: pl.BlockSpec(shp, lambda b, i: (0,) * len(shp))
    tab = pl.BlockSpec((tm, HEAD_DIM), lambda b, i: (i, 0))
    vt_rows = HEAD_DIM
    vt_spec = pl.BlockSpec((None, N_KV_HEADS, None, vt_rows, tm), lambda b, i: (b, 0, i, 0, 0))
    vt_shape = jax.ShapeDtypeStruct((bsz, N_KV_HEADS, s // tm, vt_rows, tm), BF16)
    rows_out = lambda w: (row(w), jax.ShapeDtypeStruct((bsz, s, w), BF16))
    outs = [rows_out(FOURIER_DIM), rows_out(FOURIER_DIM), rows_out(Q_DIM), rows_out(KV_DIM),
            (vt_spec, vt_shape), rows_out(D_MODEL), rows_out(D_MODEL)]
    return pl.pallas_call(
        _inproj_kernel,
        grid=(bsz, s // tm),
        in_specs=[row(d), mod, mod, full((d, ncol)), full(dft_c.shape),
                  full((1, HEAD_DIM)), full((1, HEAD_DIM)), tab, tab, tab],
        out_specs=[o[0] for o in outs],
        out_shape=[o[1] for o in outs],
        compiler_params=_cparams(("parallel", "parallel")),
        name="inproj",
    )(x, sc1, sh1, w_in, dft_c, qg, kg, cos, sin_a, sin_b)


def _fft1_kernel(a_ref, b_ref, m_ref, cw_ref, sw_ref, zr_ref, zi_ref, a32, b32):
    p, nb, _ = a_ref.shape
    a32[...] = a_ref[...].astype(F32)
    b32[...] = b_ref[...].astype(F32)
    for j in range(nb):
        ab = jnp.concatenate([a32[:, j, :], b32[:, j, :]], axis=0).astype(BF16)
        t = jnp.dot(m_ref[...], ab, preferred_element_type=F32)
        tr, ti = t[:p], t[p:]
        cw, sw = cw_ref[j], sw_ref[j]
        zr_ref[j] = (tr * cw - ti * sw).astype(BF16)
        zi_ref[j] = (ti * cw + tr * sw).astype(BF16)


def _fft1(fa, fb, m1, cw, sw, p, q):
    bsz, s, c = fa.shape
    nb = BF16_SUBLANES
    inp = pl.BlockSpec((None, p, nb, c), lambda b, j: (b, 0, j, 0))
    tw = pl.BlockSpec((nb, p, 1), lambda b, j: (j, 0, 0))
    out = pl.BlockSpec((None, nb, p, c), lambda b, j: (b, j, 0, 0))
    return pl.pallas_call(
        _fft1_kernel,
        grid=(bsz, q // nb),
        in_specs=[inp, inp, pl.BlockSpec(m1.shape, lambda b, j: (0, 0)), tw, tw],
        out_specs=[out, out],
        out_shape=[jax.ShapeDtypeStruct((bsz, q, p, c), BF16)] * 2,
        scratch_shapes=[pltpu.VMEM((p, nb, c), F32)] * 2,
        compiler_params=_cparams(("parallel", "parallel")),
        name="fft1",
    )(fa.reshape(bsz, p, q, c), fb.reshape(bsz, p, q, c), m1, cw, sw)


def _fft2_kernel(zr_ref, zi_ref, m_ref, y_ref, zr32, zi32, y32):
    zr32[...] = zr_ref[...].astype(F32)
    zi32[...] = zi_ref[...].astype(F32)
    for j in range(zr_ref.shape[1]):
        z = jnp.concatenate([zr32[:, j, :], zi32[:, j, :]], axis=0).astype(BF16)
        y32[:, j, :] = jnp.dot(m_ref[...], z, preferred_element_type=F32)
    y_ref[...] = y32[...].astype(BF16)


def _fft2(zr, zi, m2):
    bsz, q, p, c = zr.shape
    pc = BF16_SUBLANES
    blk = pl.BlockSpec((None, q, pc, c), lambda b, j: (b, 0, j, 0))
    y = pl.pallas_call(
        _fft2_kernel,
        grid=(bsz, p // pc),
        in_specs=[blk, blk, pl.BlockSpec(m2.shape, lambda b, j: (0, 0))],
        out_specs=blk,
        out_shape=jax.ShapeDtypeStruct((bsz, q, p, c), BF16),
        scratch_shapes=[pltpu.VMEM((q, pc, c), F32)] * 3,
        compiler_params=_cparams(("parallel", "parallel")),
        name="fft2",
    )(zr, zi, m2)
    return y.reshape(bsz, q * p, c)


def _dft_tables(s):
    q = FFT_Q
    p = s // q

    def cs(n_rows, n_cols, n):
        r = np.arange(n_rows, dtype=np.int64)[:, None]
        c = np.arange(n_cols, dtype=np.int64)[None, :]
        ang = ((r * c) % n).astype(np.float64) * (2.0 * math.pi / n)
        return np.cos(ang).astype(np.float32), np.sin(ang).astype(np.float32)

    cc, sc = cs(FOURIER_GROUP_DIM, FOURIER_GROUP_DIM, FOURIER_GROUP_DIM)
    dft_c = jnp.asarray(np.concatenate([cc, sc], axis=1), dtype=BF16)
    cp, sp = cs(p, p, p)
    m1 = jnp.asarray(np.block([[cp, -sp], [sp, cp]]), dtype=BF16)
    cq, sq = cs(q, q, q)
    scale = np.float32(1.0 / math.sqrt(s * FOURIER_GROUP_DIM))
    m2 = jnp.asarray(np.concatenate([cq, -sq], axis=1) * scale, dtype=BF16)
    cw, sw = cs(q, p, s)
    return dft_c, m1, m2, jnp.asarray(cw[:, :, None]), jnp.asarray(sw[:, :, None]), p, q


def _rope_tables(s):
    f32 = np.float32
    rows = s // GRID_W
    row_ids = np.repeat(np.arange(rows, dtype=f32), GRID_W)
    col_ids = np.tile(np.arange(GRID_W, dtype=f32), rows)
    freqs = (f32(ROPE_THETA) ** (-np.arange(0, ROPE_AXIS_DIM, 2, dtype=f32) / f32(ROPE_AXIS_DIM))).astype(f32)
    ang_r = (row_ids[:, None] * freqs).astype(f32)
    ang_c = (col_ids[:, None] * freqs).astype(f32)
    cr, sr, cc, sc = np.cos(ang_r), np.sin(ang_r), np.cos(ang_c), np.sin(ang_c)
    z = np.zeros_like(sr)
    cos = np.concatenate([cr, cr, cc, cc], axis=1)
    sin_b = np.concatenate([-sr, z, -sc, z], axis=1)
    sin_a = np.concatenate([z, sr, z, sc], axis=1)
    return jnp.asarray(cos, F32), jnp.asarray(sin_a, F32), jnp.asarray(sin_b, F32)


def _attn_kernel(kb_ref, q_ref, k_ref, vt_ref, o_ref, m_ref, acc_ref, al0, al1, s0, s1, p0, p1,
                 qt_ref, mx0, mx1, qa_ref, l_ref, ps0, ps1):
    al_ref, s_ref, p_ref, mx_ref, ps_ref = (al0, al1), (s0, s1), (p0, p1), (mx0, mx1), (ps0, ps1)
    nchunk = vt_ref.shape[0]
    tk = vt_ref.shape[2]
    tq = q_ref.shape[0]
    heads = range(Q_GROUP)

    def k_chunk(c):
        return k_ref[pl.ds(pl.multiple_of(c * tk, tk), tk), :]

    def write_out():
        for g in heads:
            out_t = acc_ref[g] / l_ref[g]
            o_ref[:, g * HEAD_DIM:(g + 1) * HEAD_DIM] = out_t.T.astype(BF16)

    row0 = lax.broadcasted_iota(jnp.int32, (HEAD_DIM, tq), 0) == 0
    for g in heads:
        qt = q_ref[:, g * HEAD_DIM:(g + 1) * HEAD_DIM].astype(F32).T
        bound = jnp.sqrt(jnp.sum(qt * qt, axis=0, keepdims=True)) * kb_ref[...]
        qa_ref[g, :HEAD_DIM, :] = qt.astype(BF16)
        qa_ref[g, HEAD_DIM:, :] = jnp.where(row0, -bound, 0.0).astype(BF16)
    kf = ATTN_FAST_KCHUNKS
    tkf = kf * tk
    nfast = nchunk // kf
    ones_col = (lax.broadcasted_iota(jnp.int32, (tkf, HEAD_DIM), 1) == 0).astype(BF16)

    def probs_into(c, slot):
        kc = k_ref[pl.ds(pl.multiple_of(c * tkf, tkf), tkf), :]
        ka = jnp.concatenate([kc, ones_col], axis=1)
        for g in heads:
            p32 = jnp.exp2(jnp.dot(ka, qa_ref[g], preferred_element_type=F32))
            l_ref[g] += jnp.sum(p32, axis=0, keepdims=True)
            p_ref[slot][g] = p32.astype(BF16)

    def pv_add(c, slot):
        vc = jnp.concatenate([vt_ref[kf * c + j] for j in range(kf)], axis=1)
        for g in heads:
            acc_ref[g] += jnp.dot(vc, p_ref[slot][g], preferred_element_type=F32)

    acc_ref[...] = jnp.zeros(acc_ref.shape, F32)
    l_ref[...] = jnp.zeros(l_ref.shape, F32)
    probs_into(0, 0)

    def fast_pair(i, carry):
        probs_into(2 * i + 1, 1)
        pv_add(2 * i, 0)
        probs_into(2 * i + 2, 0)
        pv_add(2 * i + 1, 1)
        return carry

    lax.fori_loop(0, (nfast - 2) // 2, fast_pair, 0)
    probs_into(nfast - 1, 1)
    pv_add(nfast - 2, 0)
    pv_add(nfast - 1, 1)
    write_out()
    den_min = l_ref[0]
    for g in heads[1:]:
        den_min = jnp.minimum(den_min, l_ref[g])
    safe = jnp.min(den_min) > DENOMINATOR_GUARD

    def scores_into(c, slot):
        kc = k_chunk(c)
        for g in heads:
            sc = jnp.dot(kc, qt_ref[g], preferred_element_type=F32)
            s_ref[slot][g] = sc
            parts = [jnp.max(sc[r * (tk // 4):(r + 1) * (tk // 4)], axis=0, keepdims=True) for r in range(4)]
            mx_ref[slot][g] = jnp.maximum(jnp.maximum(parts[0], parts[1]), jnp.maximum(parts[2], parts[3]))

    def softmax(slot):
        for g in heads:
            m_old = m_ref[g]
            m_new = jnp.maximum(m_old, mx_ref[slot][g])
            al_ref[slot][g] = jnp.exp2(m_old - m_new)
            m_ref[g] = m_new
            p32 = jnp.exp2(s_ref[slot][g] - m_new)
            ps_ref[slot][g] = jnp.sum(p32, axis=0, keepdims=True)
            p_ref[slot][g, :tk, :] = p32.astype(BF16)

    def pv_update(c, slot):
        vc = vt_ref[c]
        for g in heads:
            acc_ref[g] = al_ref[slot][g] * acc_ref[g] + jnp.dot(vc, p_ref[slot][g, :tk, :],
                                                                preferred_element_type=F32)
            l_ref[g] = al_ref[slot][g] * l_ref[g] + ps_ref[slot][g]

    def step(c, slot):
        scores_into(c + 1, 1 - slot)
        pv_update(c - 1, 1 - slot)
        softmax(slot)

    @pl.when(jnp.logical_not(safe))
    def _():
        for g in heads:
            qt_ref[g] = qa_ref[g, :HEAD_DIM, :]
        scores_into(0, 0)
        m_ref[...] = jnp.full(m_ref.shape, -jnp.inf, F32)
        acc_ref[...] = jnp.zeros(acc_ref.shape, F32)
        l_ref[...] = jnp.zeros(l_ref.shape, F32)
        softmax(0)
        scores_into(1, 1)

        def pair(i, carry):
            step(2 * i + 1, 1)
            step(2 * i + 2, 0)
            return carry

        lax.fori_loop(0, (nchunk - 2) // 2, pair, 0)
        pv_update(nchunk - 2, 0)
        softmax(1)
        pv_update(nchunk - 1, 1)
        write_out()


def _attention(q, k, vt, k_norm_bound, tq):
    bsz, s, _ = q.shape
    rows, tk = vt.shape[3], vt.shape[4]
    nchunk = s // tk
    assert nchunk % (2 * ATTN_FAST_KCHUNKS) == 0 and nchunk >= 4 * ATTN_FAST_KCHUNKS
    gw = Q_GROUP * HEAD_DIM
    qspec = pl.BlockSpec((None, tq, gw), lambda b, h, i: (b, i, h))
    kspec = pl.BlockSpec((None, s, HEAD_DIM), lambda b, h, i: (b, 0, h))
    vspec = pl.BlockSpec((None, None, nchunk, rows, tk), lambda b, h, i: (b, h, 0, 0, 0))
    head_vec = pltpu.VMEM((Q_GROUP, 1, tq), F32)
    return pl.pallas_call(
        _attn_kernel,
        grid=(bsz, N_KV_HEADS, s // tq),
        in_specs=[pl.BlockSpec((1, 1), lambda b, h, i: (0, 0)), qspec, kspec, vspec],
        out_specs=qspec,
        out_shape=jax.ShapeDtypeStruct((bsz, s, Q_DIM), BF16),
        scratch_shapes=[head_vec, pltpu.VMEM((Q_GROUP, rows, tq), F32), head_vec, head_vec]
        + [pltpu.VMEM((Q_GROUP, tk, tq), F32)] * 2
        + [pltpu.VMEM((Q_GROUP, ATTN_FAST_KCHUNKS * tk, tq), BF16)] * 2
        + [pltpu.VMEM((Q_GROUP, HEAD_DIM, tq), BF16), head_vec, head_vec,
           pltpu.VMEM((Q_GROUP, 2 * HEAD_DIM, tq), BF16), head_vec, head_vec, head_vec],
        compiler_params=_cparams(("parallel", "parallel", "parallel")),
        name="attn",
    )(k_norm_bound, q, k, vt)


def _mix_kernel(yf_ref, at_ref, gf_ref, ga_ref, x_ref, g1_ref, sc2_ref, sh2_ref,
                wf_ref, wo_ref, wout_ref, l1g_ref, l1b_ref, wr_ref, br_ref, tri_ref,
                x1_ref, u2_ref, idx_ref, wgt_ref, rank_ref, cnt_ref, cnt_acc):
    first = jnp.logical_and(pl.program_id(0) == 0, pl.program_id(1) == 0)

    @pl.when(first)
    def _():
        cnt_acc[...] = jnp.zeros_like(cnt_acc)

    tm_all = x_ref.shape[0]
    rg = tm_all // MIX_ROW_GROUPS
    u_his, u_los = [], []
    groups = [slice(i * rg, (i + 1) * rg) for i in range(MIX_ROW_GROUPS)]
    ms = []
    for rows in groups:
        yf = jnp.dot(yf_ref[rows, :], wf_ref[...], preferred_element_type=F32)
        ya = jnp.dot(at_ref[rows, :], wo_ref[...], preferred_element_type=F32)
        ms.append((gf_ref[rows, :].astype(F32) * yf + ga_ref[rows, :].astype(F32) * ya).astype(BF16))
    hs = [jnp.dot(m, wout_ref[...], preferred_element_type=F32) for m in ms]
    for rows, h in zip(groups, hs):
        r = DEEPNORM_ALPHA * x_ref[rows, :] + g1_ref[...] * h
        x1 = _layer_norm(r, LN_EPS) * l1g_ref[...] + l1b_ref[...]
        x1_ref[rows, :] = x1
        u2 = _layer_norm(x1, ADA_EPS) * (1.0 + sc2_ref[...]) + sh2_ref[...]
        u2_ref[rows, :] = _pack_bf16_pairs(u2)
        u_hi = u2.astype(BF16)
        u_his.append(u_hi)
        u_los.append((u2 - u_hi.astype(F32)).astype(BF16))

    nt = (((1,), (1,)), ((), ()))
    u_hi = jnp.concatenate(u_his, axis=0)
    u_lo = jnp.concatenate(u_los, axis=0)
    w = wr_ref[...]
    w_hi = w.astype(BF16)
    w_lo = (w - w_hi.astype(F32)).astype(BF16)
    hh_lh = lax.dot_general(jnp.concatenate([w_hi, w_lo], axis=0), u_hi, nt, preferred_element_type=F32)
    logits = (hh_lh[:N_EXPERTS] + hh_lh[N_EXPERTS:]
              + lax.dot_general(w_hi, u_lo, nt, preferred_element_type=F32)) + br_ref[...]

    ne, tm = logits.shape
    eid = lax.broadcasted_iota(jnp.int32, (ne, tm), 0)
    work = logits
    vals, idxs, hots = [], [], []
    for _ in range(TOP_K):
        mx = jnp.max(work, axis=0, keepdims=True)
        ix = jnp.min(jnp.where(work == mx, eid, ne), axis=0, keepdims=True)
        hot = eid == ix
        work = jnp.where(hot, -jnp.inf, work)
        vals.append(mx)
        idxs.append(ix)
        hots.append(hot)
    ex = [jnp.exp(v - vals[0]) for v in vals]
    den = ex[0] + ex[1] + ex[2] + ex[3]
    wgt_ref[...] = jnp.concatenate([e / den for e in ex], axis=0)
    idx_ref[...] = jnp.concatenate(idxs, axis=0)

    sel = hots[0] | hots[1] | hots[2] | hots[3]
    mask = jnp.where(sel, 1.0, 0.0)
    prefix = jnp.dot(mask.astype(BF16), tri_ref[...], preferred_element_type=F32)
    pos = prefix + cnt_acc[...]
    ranks = [jnp.sum(jnp.where(hot, pos, 0.0), axis=0, keepdims=True) for hot in hots]
    rank_ref[...] = jnp.concatenate(ranks, axis=0).astype(jnp.int32)
    cnt_acc[...] += jnp.sum(mask, axis=1, keepdims=True)
    cnt_ref[...] = cnt_acc[...].astype(jnp.int32)


def _mix(yfm, attn, gf, ga, x, g1, sc2, sh2, wf, wo, wout, l1g, l1b, wr_t, br, tm, b0, bsz):
    _, s, d = x.shape
    n = bsz * s
    nt = s // tm
    tri = (jnp.arange(tm)[:, None] < jnp.arange(tm)[None, :]).astype(BF16)
    row_in = lambda w: pl.BlockSpec((None, tm, w), lambda b, i: (b + b0, i, 0))
    row = lambda w: pl.BlockSpec((None, tm, w), lambda b, i: (b, i, 0))
    mod = pl.BlockSpec((None, 1, d), lambda b, i: (b + b0, 0, 0))
    full = lambda shp: pl.BlockSpec(shp, lambda b, i: (0,) * len(shp))
    tok = pl.BlockSpec((TOP_K, tm), lambda b, i: (0, b * nt + i))
    return pl.pallas_call(
        _mix_kernel,
        grid=(bsz, nt),
        in_specs=[row_in(FOURIER_DIM), row_in(Q_DIM), row_in(d), row_in(d), row_in(d), mod, mod, mod,
                  full(wf.shape), full(wo.shape), full(wout.shape), full((1, d)), full((1, d)),
                  full(wr_t.shape), full((N_EXPERTS, 1)), full((tm, tm))],
        out_specs=[row(d), row(d // 2), tok, tok, tok, full((N_EXPERTS, 1))],
        out_shape=[jax.ShapeDtypeStruct((bsz, s, d), F32),
                   jax.ShapeDtypeStruct((bsz, s, d // 2), jnp.int32),
                   jax.ShapeDtypeStruct((TOP_K, n), jnp.int32),
                   jax.ShapeDtypeStruct((TOP_K, n), F32),
                   jax.ShapeDtypeStruct((TOP_K, n), jnp.int32),
                   jax.ShapeDtypeStruct((N_EXPERTS, 1), jnp.int32)],
        scratch_shapes=[pltpu.VMEM((N_EXPERTS, 1), F32)],
        compiler_params=_cparams(("arbitrary", "arbitrary")),
        name="mix",
    )(yfm, attn, gf, ga, x, g1, sc2, sh2, wf, wo, wout, l1g, l1b, wr_t, br, tri)


def _expert_kernel(be_ref, nv_ref, new_ref, xs_ref, wgu_ref, bgu_ref, wd_ref, bd_ref, ys_ref,
                   wgu_bf, wd_bf):
    del be_ref
    i = pl.program_id(0)
    nvalid = nv_ref[i]

    @pl.when(new_ref[i] == 1)
    def _():
        wgu_bf[...] = wgu_ref[...].astype(BF16)
        wd_bf[...] = wd_ref[...].astype(BF16)

    @pl.when(nvalid > 0)
    def _():
        xw = xs_ref[...]
        row = lax.broadcasted_iota(jnp.int32, xw.shape, 0)
        lo, hi = _unpack_bf16_pairs(jnp.where(row < nvalid, xw, 0))
        xb = jnp.concatenate([lo.astype(BF16), hi.astype(BF16)], axis=1)
        h = jnp.dot(xb, wgu_bf[...], preferred_element_type=F32) + bgu_ref[...]
        gate = jnp.minimum(h[:, :D_EXPERT], SWIGLU_LIMIT)
        up = jnp.clip(h[:, D_EXPERT:], -SWIGLU_LIMIT, SWIGLU_LIMIT)
        act = (up + 1.0) * gate * _sigmoid(SWIGLU_ALPHA * gate)
        y = jnp.dot(act.astype(BF16), wd_bf[...], preferred_element_type=F32) + bd_ref[...]
        ys_ref[...] = _pack_bf16_pairs(y)

    @pl.when(nvalid == 0)
    def _():
        ys_ref[...] = jnp.zeros_like(ys_ref)


def _experts(block_e, block_nvalid, block_new, xs, wgu, bgu, wd, bd, tmb):
    n_pad, dh = xs.shape
    d = 2 * dh
    grid_spec = pltpu.PrefetchScalarGridSpec(
        num_scalar_prefetch=3,
        grid=(n_pad // tmb,),
        in_specs=[pl.BlockSpec((tmb, dh), lambda i, be, nv, nw: (i, 0)),
                  pl.BlockSpec((None, d, 2 * D_EXPERT), lambda i, be, nv, nw: (be[i], 0, 0)),
                  pl.BlockSpec((None, 1, 2 * D_EXPERT), lambda i, be, nv, nw: (be[i], 0, 0)),
                  pl.BlockSpec((None, D_EXPERT, d), lambda i, be, nv, nw: (be[i], 0, 0)),
                  pl.BlockSpec((None, 1, d), lambda i, be, nv, nw: (be[i], 0, 0))],
        out_specs=pl.BlockSpec((tmb, dh), lambda i, be, nv, nw: (i, 0)),
        scratch_shapes=[pltpu.VMEM((d, 2 * D_EXPERT), BF16), pltpu.VMEM((D_EXPERT, d), BF16)],
    )
    return pl.pallas_call(
        _expert_kernel,
        grid_spec=grid_spec,
        out_shape=jax.ShapeDtypeStruct((n_pad, dh), jnp.int32),
        compiler_params=_cparams(("arbitrary",)),
        name="experts",
    )(block_e, block_nvalid, block_new, xs, wgu, bgu, wd, bd)


def _final_kernel(yg_ref, w_ref, x1_ref, g2_ref, lg_ref, lb_ref, o_ref):
    w = w_ref[...]
    acc_lo = None
    for j in range(TOP_K):
        lo, hi = _unpack_bf16_pairs(yg_ref[j])
        wj = w[:, j:j + 1]
        acc_lo = lo * wj if acc_lo is None else acc_lo + lo * wj
        acc_hi = hi * wj if j == 0 else acc_hi + hi * wj
    h = jnp.concatenate([acc_lo, acc_hi], axis=1)
    r = DEEPNORM_ALPHA * x1_ref[...] + g2_ref[...] * h
    o_ref[...] = _layer_norm(r, LN_EPS) * lg_ref[...] + lb_ref[...]


def _final(yg, w_tok, x1, g2, lg, lb, tm, b0, total_bsz, out_so_far):
    bsz, s, d = x1.shape
    nt = s // tm
    row = pl.BlockSpec((None, tm, d), lambda b, i: (b, i, 0))
    full = pl.BlockSpec((1, d), lambda b, i: (0, 0))
    in_specs = [pl.BlockSpec((TOP_K, tm, d // 2), lambda b, i: (0, b * nt + i, 0)),
                pl.BlockSpec((tm, TOP_K), lambda b, i: (b * nt + i, 0)),
                row, pl.BlockSpec((None, 1, d), lambda b, i: (b + b0, 0, 0)), full, full]
    args = [yg, w_tok, x1, g2, lg, lb]
    aliases = {}
    if out_so_far is not None:
        in_specs.append(pl.BlockSpec(memory_space=pl.ANY))
        args.append(out_so_far)
        aliases = {len(args) - 1: 0}
    body = _final_kernel if out_so_far is None else (lambda *refs: _final_kernel(*refs[:6], refs[7]))
    return pl.pallas_call(
        body,
        grid=(bsz, nt),
        in_specs=in_specs,
        out_specs=pl.BlockSpec((None, tm, d), lambda b, i: (b + b0, i, 0)),
        out_shape=jax.ShapeDtypeStruct((total_bsz, s, d), F32),
        input_output_aliases=aliases,
        compiler_params=_cparams(("parallel", "parallel")),
        name="final",
    )(*args)


def _scatter_rows(rows, dest, m_out):
    n, w = rows.shape
    nj = dest.shape[0]
    ch = GATHER_CHUNK
    nw = SC_CORES * SC_SUBCORES
    nch = n // (nw * ch)
    assert n == nw * nch * ch and nch % 2 == 0
    per_w = nch * ch
    mesh = plsc.VectorSubcoreMesh(core_axis_name="c", subcore_axis_name="s",
                                  num_cores=SC_CORES, num_subcores=SC_SUBCORES)

    @functools.partial(
        pl.kernel, mesh=mesh,
        out_type=jax.ShapeDtypeStruct((m_out, w), rows.dtype),
        scratch_types=[pltpu.VMEM((nj, nch, ch), jnp.int32), pltpu.VMEM((2, ch, w), rows.dtype),
                       pltpu.SemaphoreType.DMA((2,)), pltpu.SemaphoreType.DMA((2,))],
    )
    def scatter_kernel(rows_hbm, dest_hbm, out_hbm, idx_v, rows_v, rsem, ssem):
        wid = lax.axis_index("s") * SC_CORES + lax.axis_index("c")
        base = wid * per_w
        for j in range(nj):
            pltpu.sync_copy(dest_hbm.at[j, wid], idx_v.at[j])

        def read(i, b):
            off = pl.multiple_of(base + i * ch, ch)
            return pltpu.make_async_copy(rows_hbm.at[pl.ds(off, ch)], rows_v.at[b], rsem.at[b])

        def scatter(i, b, j):
            return pltpu.make_async_copy(rows_v.at[b], out_hbm.at[idx_v.at[j, i]], ssem.at[b])

        read(0, 0).start()

        @pl.loop(0, nch, step=2)
        def _(i0):
            for b in range(2):
                i = i0 + b
                read(i, b).wait()

                @pl.when(i + 1 < nch)
                def _():
                    @pl.when(i >= 1)
                    def _():
                        for j in range(nj):
                            scatter(i - 1, 1 - b, j).wait()
                    read(i + 1, 1 - b).start()

                for j in range(nj):
                    scatter(i, b, j).start()

        for b in range(2):
            for j in range(nj):
                scatter(nch - 2 + b, b, j).wait()

    return scatter_kernel(rows, dest.reshape(nj, nw, nch, ch))


def _gather_rows(table, idx):
    m = idx.shape[0]
    w = table.shape[1]
    ch = GATHER_CHUNK
    nw = SC_CORES * SC_SUBCORES
    nch = m // (nw * ch)
    assert m == nw * nch * ch and nch % 2 == 0
    per_w = nch * ch
    mesh = plsc.VectorSubcoreMesh(core_axis_name="c", subcore_axis_name="s",
                                  num_cores=SC_CORES, num_subcores=SC_SUBCORES)

    @functools.partial(
        pl.kernel, mesh=mesh,
        out_type=jax.ShapeDtypeStruct((m, w), table.dtype),
        scratch_types=[pltpu.VMEM((nch, ch), jnp.int32), pltpu.VMEM((2, ch, w), table.dtype),
                       pltpu.SemaphoreType.DMA((2,)), pltpu.SemaphoreType.DMA((2,))],
    )
    def gather_kernel(table_hbm, idx_hbm, out_hbm, idx_v, rows_v, gsem, wsem):
        wid = lax.axis_index("s") * SC_CORES + lax.axis_index("c")
        base = wid * per_w
        pltpu.sync_copy(idx_hbm.at[wid], idx_v)

        def gather(j, b):
            return pltpu.make_async_copy(table_hbm.at[idx_v.at[j]], rows_v.at[b], gsem.at[b])

        def write(j, b):
            off = pl.multiple_of(base + j * ch, ch)
            return pltpu.make_async_copy(rows_v.at[b], out_hbm.at[pl.ds(off, ch)], wsem.at[b])

        gather(0, 0).start()

        @pl.loop(0, nch, step=2)
        def _(i):
            for b in range(2):
                j = i + b
                gather(j, b).wait()

                @pl.when(j + 1 < nch)
                def _():
                    @pl.when(j >= 1)
                    def _():
                        write(j - 1, 1 - b).wait()
                    gather(j + 1, 1 - b).start()

                write(j, b).start()

        write(nch - 2, 0).wait()
        write(nch - 1, 1).wait()

    return gather_kernel(table, idx.reshape(nw, nch, ch))


def kernel(x, c, w_ada, b_ada, w_in, q_norm_g, k_norm_g, w_fourier, w_attn_o, w_out, ln1_g, ln1_b,
           w_router, b_router, w_gate_up, b_gate_up, w_down, b_down, ln2_g, ln2_b):
    bsz, s, d = x.shape
    n = bsz * s
    tm = min(ROW_TILE, s)
    l = 0

    mod = _ada(c, w_ada[l], b_ada[l])
    sh1, sc1, g1, sh2, sc2, g2 = [m[:, None, :] for m in jnp.split(mod, N_MOD, axis=-1)]

    dft_c, m1, m2, cw, sw, p, q = _dft_tables(s)
    cos, sin_a, sin_b = _rope_tables(s)

    fa, fb, qh, kh, vt, gf, ga = _inproj(
        x, sc1, sh1, w_in[l].astype(BF16), dft_c,
        q_norm_g[l].reshape(1, HEAD_DIM), k_norm_g[l].reshape(1, HEAD_DIM), cos, sin_a, sin_b, tm)

    zr, zi = _fft1(fa, fb, m1, cw, sw, p, q)
    yfm = _fft2(zr, zi, m2)

    k_bound = (math.sqrt(HEAD_DIM) * (1.0 + 2.0 ** -7)) * jnp.max(jnp.abs(k_norm_g[l])).reshape(1, 1)
    attn = _attention(qh, kh, vt, k_bound, min(ATTN_TQ, s))

    wf_b, wo_b, wout_b = w_fourier[l].astype(BF16), w_attn_o[l].astype(BF16), w_out[l].astype(BF16)
    bgu, bd = b_gate_up[l][:, None, :], b_down[l][:, None, :]
    tmb = MOE_BLOCK
    eids = jnp.arange(N_EXPERTS, dtype=jnp.int32)

    ngroups = MOE_BATCH_GROUPS if bsz % MOE_BATCH_GROUPS == 0 else 1
    gb = bsz // ngroups
    ng = gb * s
    out = None
    for grp in range(ngroups):
        b0 = grp * gb
        x1, u2p, idx_t, wgt_t, rank_t, counts = _mix(
            yfm, attn, gf, ga, x, g1, sc2, sh2, wf_b, wo_b, wout_b,
            ln1_g[l].reshape(1, d), ln1_b[l].reshape(1, d),
            w_router[l].T, b_router[l].reshape(N_EXPERTS, 1), tm, b0, gb)

        counts = counts.reshape(N_EXPERTS)
        pcounts = ((counts + tmb - 1) // tmb) * tmb
        pends = jnp.cumsum(pcounts)
        pstarts = pends - pcounts
        n_pad = ng * TOP_K + N_EXPERTS * tmb
        nblk = n_pad // tmb
        start_of = jnp.sum(jnp.where(idx_t[None] == eids[:, None, None], pstarts[:, None, None], 0), axis=0)
        dest_t = start_of + rank_t
        block_starts = jnp.arange(nblk, dtype=jnp.int32) * tmb
        block_e = jnp.sum((block_starts[:, None] >= pends[None, :]).astype(jnp.int32), axis=1)
        block_e = jnp.minimum(block_e, N_EXPERTS - 1)
        valid_end = jnp.sum(jnp.where(block_e[:, None] == eids[None, :], (pstarts + counts)[None, :], 0), axis=1)
        block_nvalid = jnp.clip(valid_end - block_starts, 0, tmb)
        block_new = jnp.concatenate([jnp.ones((1,), jnp.int32),
                                     (block_e[1:] != block_e[:-1]).astype(jnp.int32)])

        xs = _scatter_rows(u2p.reshape(ng, d // 2), dest_t, n_pad)
        ys = _experts(block_e, block_nvalid, block_new, xs, w_gate_up[l], bgu, w_down[l], bd, tmb)
        yg = _gather_rows(ys, dest_t.reshape(-1)).reshape(TOP_K, ng, d // 2)
        out = _final(yg, wgt_t.T, x1, g2, ln2_g[l].reshape(1, d), ln2_b[l].reshape(1, d), tm,
                     b0, bsz, out)
    return out
```
